```python
import jax, jax.numpy as jnp
from jax import lax
import numpy as np

D_MODEL = 1024
BATCH = 8
SEQ = 2048
DEPTH = 2

MIX_W = D_MODEL
N_BRANCH = 3
CONV_WIDTH = 4
DN_HEADS = 8
DN_HEAD_DIM = MIX_W // DN_HEADS
DN_CHUNK = 64
SB_HEADS = 16
SB_HEAD_DIM = MIX_W // SB_HEADS
SB_BLOCK = 128
SSM_HEADS = 16
SSM_HEAD_DIM = MIX_W // SSM_HEADS
SSM_STATE = 128
SSM_GROUPS = 4
SSM_CHUNK = 64
D_FF = 4 * D_MODEL
EPS = 1e-6

DN_QKV = 3 * MIX_W
SSM_CONV_DIM = MIX_W + 2 * SSM_GROUPS * SSM_STATE
IN_SIZES = (DN_QKV, MIX_W, DN_HEADS, DN_HEADS, 3 * MIX_W, MIX_W, SSM_CONV_DIM, SSM_HEADS, N_BRANCH * D_MODEL)
IN_DIM = sum(IN_SIZES)

kernel_name = "hybrid_gdn_stickbreak_mamba2_block"


def rms_norm(x, w):
    xf = x.astype(jnp.float32)
    xf = xf * lax.rsqrt(jnp.mean(xf * xf, axis=-1, keepdims=True) + EPS)
    return (xf * w.astype(jnp.float32)).astype(x.dtype)


def group_rms_norm(x, w, groups):
    shp = x.shape
    xf = x.astype(jnp.float32).reshape(*shp[:-1], groups, shp[-1] // groups)
    xf = xf * lax.rsqrt(jnp.mean(xf * xf, axis=-1, keepdims=True) + EPS)
    return (xf.reshape(shp) * w.astype(jnp.float32)).astype(x.dtype)


def l2_normalize(x):
    return x * lax.rsqrt(jnp.sum(x * x, axis=-1, keepdims=True) + EPS)


def split_columns(t, sizes):
    out, start = [], 0
    for s in sizes:
        out.append(t[..., start:start + s])
        start += s
    return out


def causal_dwconv(x, w, b=None):
    k_width, seq = w.shape[0], x.shape[1]
    xp = jnp.pad(x, ((0, 0), (k_width - 1, 0), (0, 0)))
    y = xp[:, 0:seq] * w[0]
    for k in range(1, k_width):
        y = y + xp[:, k:k + seq] * w[k]
    return y if b is None else y + b


def gated_delta_rule(q, k, v, g, beta):
    dtype = v.dtype
    q, k, v, g, beta = (t.astype(jnp.float32) for t in (q, k, v, g, beta))
    bsz, seq, h, dk = q.shape
    dv = v.shape[-1]
    c = DN_CHUNK
    n = seq // c

    def to_chunks(t):
        t = t.reshape(bsz, n, c, *t.shape[2:])
        return jnp.swapaxes(t, 2, 3)

    q = to_chunks(q) * dk ** -0.5
    k, v = to_chunks(k), to_chunks(v)
    g, beta = to_chunks(g), to_chunks(beta)
    gc = jnp.cumsum(g, axis=-1)
    causal = jnp.tril(jnp.ones((c, c), dtype=bool))
    strict = jnp.tril(jnp.ones((c, c), dtype=bool), -1)
    decay = jnp.exp(jnp.where(causal, gc[..., :, None] - gc[..., None, :], -jnp.inf))

    kb = k * beta[..., None]
    vb = v * beta[..., None]
    lower = jnp.where(strict, jnp.einsum('bnhid,bnhjd->bnhij', kb, k) * decay, 0.0)
    tmat = lower + jnp.eye(c, dtype=jnp.float32)
    rhs = jnp.concatenate([vb, kb * jnp.exp(gc)[..., None]], axis=-1)
    sol = lax.linalg.triangular_solve(tmat, rhs, left_side=True, lower=True, unit_diagonal=True)
    u, w = sol[..., :dv], sol[..., dv:]

    attn = jnp.einsum('bnhid,bnhjd->bnhij', q, k) * decay
    qg = q * jnp.exp(gc)[..., None]
    kd = k * jnp.exp(gc[..., -1:] - gc)[..., None]
    glast = jnp.exp(gc[..., -1])

    def step(state, xs):
        u_c, w_c, attn_c, qg_c, kd_c, gl_c = xs
        v_new = u_c - jnp.einsum('bhcd,bhde->bhce', w_c, state)
        o_c = jnp.einsum('bhcd,bhde->bhce', qg_c, state) + jnp.einsum('bhij,bhje->bhie', attn_c, v_new)
        state = state * gl_c[..., None, None] + jnp.einsum('bhcd,bhce->bhde', kd_c, v_new)
        return state, o_c

    xs = tuple(jnp.moveaxis(t, 1, 0) for t in (u, w, attn, qg, kd, glast))
    s0 = jnp.zeros((bsz, h, dk, dv), jnp.float32)
    _, o = lax.scan(step, s0, xs)
    o = jnp.transpose(o, (1, 0, 3, 2, 4)).reshape(bsz, seq, h, dv)
    return o.astype(dtype)


def gated_deltanet_branch(qkv, gate, a, b, conv_w, a_log, dt_bias, norm_w):
    bsz, seq, _ = qkv.shape
    qkv = jax.nn.silu(causal_dwconv(qkv, conv_w))
    q, k, v = (t.reshape(bsz, seq, DN_HEADS, DN_HEAD_DIM) for t in split_columns(qkv, (MIX_W, MIX_W, MIX_W)))
    q, k = l2_normalize(q), l2_normalize(k)
    beta = jax.nn.sigmoid(b.astype(jnp.float32))
    g = -jnp.exp(a_log.astype(jnp.float32)) * jax.nn.softplus(a.astype(jnp.float32) + dt_bias.astype(jnp.float32))
    o = gated_delta_rule(q, k, v, g, beta)
    o = rms_norm(o, norm_w) * jax.nn.silu(gate.reshape(bsz, seq, DN_HEADS, DN_HEAD_DIM))
    return o.reshape(bsz, seq, MIX_W)


def stick_breaking_branch(qkv):
    bsz, seq, _ = qkv.shape
    dtype = qkv.dtype
    q, k, v = (t.reshape(bsz, seq, SB_HEADS, SB_HEAD_DIM).astype(jnp.float32)
               for t in split_columns(qkv, (MIX_W, MIX_W, MIX_W)))
    scale = SB_HEAD_DIM ** -0.5
    outs = []
    for i in range(seq // SB_BLOCK):
        t0, t1 = i * SB_BLOCK, (i + 1) * SB_BLOCK
        qb, kb, vb = q[:, t0:t1], k[:, :t1], v[:, :t1]
        z = jnp.einsum('bthd,bshd->bhts', qb, kb) * scale
        t_idx = t0 + jnp.arange(SB_BLOCK)
        s_idx = jnp.arange(t1)
        mask = s_idx[None, :] < t_idx[:, None]
        log_keep = jnp.where(mask, -jax.nn.softplus(z), 0.0)
        reach = lax.cumsum(log_keep, axis=3, reverse=True) - log_keep
        log_a = jax.nn.log_sigmoid(z) + reach
        weights = jnp.exp(jnp.where(mask, log_a, -jnp.inf))
        outs.append(jnp.einsum('bhts,bshd->bthd', weights, vb))
    o = jnp.concatenate(outs, axis=1)
    return o.reshape(bsz, seq, MIX_W).astype(dtype)


def ssd_chunked(x, a, bm, cm):
    dtype = x.dtype
    x, a, bm, cm = (t.astype(jnp.float32) for t in (x, a, bm, cm))
    bsz, seq, h, p = x.shape
    g, n_state = bm.shape[2], bm.shape[3]
    r = h // g
    c = SSM_CHUNK
    nc = seq // c
    x = x.reshape(bsz, nc, c, g, r, p)
    a = a.reshape(bsz, nc, c, g, r)
    bm = bm.reshape(bsz, nc, c, g, n_state)
    cm = cm.reshape(bsz, nc, c, g, n_state)
    a_cum = jnp.cumsum(a, axis=2)
    causal = jnp.tril(jnp.ones((c, c), dtype=bool))
    seg = a_cum[:, :, :, None] - a_cum[:, :, None, :]
    lmat = jnp.exp(jnp.where(causal[:, :, None, None], seg, -jnp.inf))
    scores = jnp.einsum('bclgn,bcsgn->bclsg', cm, bm)
    y_diag = jnp.einsum('bclsg,bclsgr,bcsgrp->bclgrp', scores, lmat, x)
    decay_states = jnp.exp(a_cum[:, :, -1:] - a_cum)
    chunk_states = jnp.einsum('bclgn,bclgr,bclgrp->bcgrpn', bm, decay_states, x)
    chunk_decay = jnp.exp(a_cum[:, :, -1])

    def step(state, xs):
        st, dec = xs
        return state * dec[..., None, None] + st, state

    h0 = jnp.zeros((bsz, g, r, p, n_state), jnp.float32)
    _, h_prev = lax.scan(step, h0, (jnp.moveaxis(chunk_states, 1, 0), jnp.moveaxis(chunk_decay, 1, 0)))
    h_prev = jnp.moveaxis(h_prev, 0, 1)
    y_off = jnp.einsum('bclgn,bcgrpn,bclgr->bclgrp', cm, h_prev, jnp.exp(a_cum))
    return (y_diag + y_off).reshape(bsz, seq, h, p).astype(dtype)


def mamba2_branch(z, xbc, dt, conv_w, conv_b, a_log, dt_bias, d_skip, norm_w):
    bsz, seq, _ = z.shape
    gn = SSM_GROUPS * SSM_STATE
    xbc = jax.nn.silu(causal_dwconv(xbc, conv_w, conv_b))
    xs, bm, cm = split_columns(xbc, (MIX_W, gn, gn))
    xs = xs.reshape(bsz, seq, SSM_HEADS, SSM_HEAD_DIM)
    bm = bm.reshape(bsz, seq, SSM_GROUPS, SSM_STATE)
    cm = cm.reshape(bsz, seq, SSM_GROUPS, SSM_STATE)
    dt = jax.nn.softplus(dt.astype(jnp.float32) + dt_bias.astype(jnp.float32))
    a = -jnp.exp(a_log.astype(jnp.float32)) * dt
    y = ssd_chunked(xs * dt[..., None].astype(xs.dtype), a, bm, cm)
    y = y + xs * d_skip[:, None]
    y = y.reshape(bsz, seq, MIX_W) * jax.nn.silu(z)
    return group_rms_norm(y, norm_w, SSM_GROUPS)


def hybrid_mixer(xn, w_in, dn_conv_w, dn_a_log, dn_dt_bias, dn_norm_w,
                 ssm_conv_w, ssm_conv_b, ssm_a_log, ssm_dt_bias, ssm_d, ssm_norm_w,
                 w_branch, w_out):
    bsz, seq, _ = xn.shape
    proj = xn @ w_in
    (dn_qkv, dn_gate, dn_a, dn_b, sb_qkv, ssm_z, ssm_xbc, ssm_dt, gate_logits) = split_columns(proj, IN_SIZES)
    o_dn = gated_deltanet_branch(dn_qkv, dn_gate, dn_a, dn_b, dn_conv_w, dn_a_log, dn_dt_bias, dn_norm_w)
    o_sb = stick_breaking_branch(sb_qkv)
    o_ssm = mamba2_branch(ssm_z, ssm_xbc, ssm_dt, ssm_conv_w, ssm_conv_b, ssm_a_log, ssm_dt_bias, ssm_d, ssm_norm_w)
    branches = jnp.stack([o_dn, o_sb, o_ssm], axis=2)
    projected = jnp.einsum('bsim,imd->bsid', branches, w_branch)
    gates = jax.nn.sigmoid(gate_logits.reshape(bsz, seq, N_BRANCH, D_MODEL))
    merged = jnp.sum(gates * projected, axis=2)
    return merged @ w_out


def setup_inputs(seed: int = 0) -> dict:
    key = jax.random.key(seed)
    ks = jax.random.split(key, 20)
    L = DEPTH

    def nrm(k, shape, scale):
        return jax.random.normal(k, shape, jnp.float32) * scale

    def log_uniform_a(k, shape):
        return jnp.log(jax.random.uniform(k, shape, jnp.float32, minval=1.0, maxval=16.0))

    def dt_bias_init(k, shape):
        dt = jnp.exp(jax.random.uniform(k, shape, jnp.float32, minval=np.log(1e-3), maxval=np.log(1e-1)))
        return dt + jnp.log(-jnp.expm1(-dt))

    return {
        "x": nrm(ks[0], (BATCH, SEQ, D_MODEL), 1.0),
        "norm_mix": 1.0 + nrm(ks[1], (L, D_MODEL), 0.02),
        "w_in": nrm(ks[2], (L, D_MODEL, IN_DIM), D_MODEL ** -0.5),
        "dn_conv_w": nrm(ks[3], (L, CONV_WIDTH, DN_QKV), CONV_WIDTH ** -0.5),
        "dn_a_log": log_uniform_a(ks[4], (L, DN_HEADS)),
        "dn_dt_bias": dt_bias_init(ks[5], (L, DN_HEADS)),
        "dn_norm_w": 1.0 + nrm(ks[6], (L, DN_HEAD_DIM), 0.02),
        "ssm_conv_w": nrm(ks[7], (L, CONV_WIDTH, SSM_CONV_DIM), CONV_WIDTH ** -0.5),
        "ssm_conv_b": nrm(ks[8], (L, SSM_CONV_DIM), 0.02),
        "ssm_a_log": log_uniform_a(ks[9], (L, SSM_HEADS)),
        "ssm_dt_bias": dt_bias_init(ks[10], (L, SSM_HEADS)),
        "ssm_d": 1.0 + nrm(ks[11], (L, SSM_HEADS), 0.02),
        "ssm_norm_w": 1.0 + nrm(ks[12], (L, MIX_W), 0.02),
        "w_branch": nrm(ks[13], (L, N_BRANCH, MIX_W, D_MODEL), MIX_W ** -0.5),
        "w_out": nrm(ks[14], (L, D_MODEL, D_MODEL), D_MODEL ** -0.5),
        "norm_mlp": 1.0 + nrm(ks[15], (L, D_MODEL), 0.02),
        "w_up": nrm(ks[16], (L, D_MODEL, D_FF), D_MODEL ** -0.5),
        "w_down": nrm(ks[17], (L, D_FF, D_MODEL), D_FF ** -0.5),
        "norm_final": 1.0 + nrm(ks[18], (D_MODEL,), 0.02),
    }


def reference(x, norm_mix, w_in, dn_conv_w, dn_a_log, dn_dt_bias, dn_norm_w,
              ssm_conv_w, ssm_conv_b, ssm_a_log, ssm_dt_bias, ssm_d, ssm_norm_w,
              w_branch, w_out, norm_mlp, w_up, w_down, norm_final):
    for l in range(DEPTH):
        h = rms_norm(x, norm_mix[l])
        x = x + hybrid_mixer(h, w_in[l], dn_conv_w[l], dn_a_log[l], dn_dt_bias[l], dn_norm_w[l],
                             ssm_conv_w[l], ssm_conv_b[l], ssm_a_log[l], ssm_dt_bias[l], ssm_d[l],
                             ssm_norm_w[l], w_branch[l], w_out[l])
        h = rms_norm(x, norm_mlp[l])
        x = x + jnp.square(jax.nn.relu(h @ w_up[l])) @ w_down[l]
    return rms_norm(x, norm_final)
```

```python
import functools

import jax
import jax.numpy as jnp
from jax import lax
from jax.experimental import pallas as pl
from jax.experimental.pallas import tpu as pltpu

F32 = jnp.float32
BF16 = jnp.bfloat16

D_MODEL = 1024
MIX_W = D_MODEL
DN_HEADS = 8
DN_HEAD_DIM = 128
SB_HEADS = 16
SB_HEAD_DIM = 64
SSM_HEADS = 16
SSM_HEAD_DIM = 64
SSM_STATE = 128
SSM_GROUPS = 4
CHUNK = 64
D_FF = 4 * D_MODEL
EPS = 1e-6
CONV_WIDTH = 4

LANES = 128
TBLK = 2 * CHUNK
HALO = 8

COL_DN_QKV = 0
COL_DN_GATE = 3072
COL_SB_QKV = 4096
COL_SSM_Z = 7168
COL_SSM_XBC = 8192
COL_GATES = 10240
N_BIG = 13312
LANE_DN_A = 0
LANE_DN_B = 8
LANE_SSM_DT = 16

VMEM_LIMIT = 56 * 1024 * 1024


def _cparams(*sem):
    return pltpu.CompilerParams(dimension_semantics=sem, vmem_limit_bytes=VMEM_LIMIT)


def _bf(x):
    return x.astype(BF16)


def _mm(a, b):
    return jnp.dot(_bf(a), _bf(b), preferred_element_type=F32)


def _mm_nt(a, b):
    return lax.dot_general(_bf(a), _bf(b), (((1,), (1,)), ((), ())), preferred_element_type=F32)


def _split2(a):
    hi = _bf(a)
    lo = _bf(a - hi.astype(F32))
    return hi, lo


def _split3(a):
    hi = _bf(a)
    r = a - hi.astype(F32)
    mid = _bf(r)
    lo = _bf(r - mid.astype(F32))
    return hi, mid, lo


def _mm_x3(a, b):
    ah, al = _split2(a)
    bh, bl = _split2(b)
    lhs = jnp.concatenate([ah, al, ah], axis=1)
    rhs = jnp.concatenate([bh, bh, bl], axis=0)
    return jnp.dot(lhs, rhs, preferred_element_type=F32)


def _mm_sel_lhs(sel, b):
    s = _bf(sel)
    bh, bm, bl = _split3(b)
    return jnp.dot(jnp.concatenate([s, s, s], axis=1), jnp.concatenate([bh, bm, bl], axis=0),
                   preferred_element_type=F32)


def _mm_sel_rhs(a, sel_bf):
    ah, am, al = _split3(a)
    return jnp.dot(jnp.concatenate([ah, am, al], axis=1), jnp.concatenate([sel_bf, sel_bf, sel_bf], axis=0),
                   preferred_element_type=F32)


def _softplus(x):
    return jnp.maximum(x, 0.0) + jnp.log(1.0 + jnp.exp(-jnp.abs(x)))


def _sigmoid(x):
    return 1.0 / (1.0 + jnp.exp(-x))


def _silu(x):
    return x * _sigmoid(x)


def _rms(x, w):
    return x * lax.rsqrt(jnp.mean(x * x, axis=-1, keepdims=True) + EPS) * w


def _causal_conv(halo, cur, w, bias):
    rows = cur.shape[0]
    xx = jnp.concatenate([halo, cur], axis=0)
    y = w[CONV_WIDTH - 1:CONV_WIDTH, :] * cur
    for k in range(CONV_WIDTH - 1):
        shifted = pltpu.roll(xx, CONV_WIDTH - 1 - k, 0)[HALO:HALO + rows]
        y = y + w[k:k + 1, :] * shifted
    if bias is not None:
        y = y + bias
    return y


def _chunk_masks():
    row = lax.broadcasted_iota(jnp.int32, (TBLK, TBLK), 0)
    col = lax.broadcasted_iota(jnp.int32, (TBLK, TBLK), 1)
    same = (row >= CHUNK) == (col >= CHUNK)
    return row, col, same & (col <= row), same & (col < row)


def _inproj_kernel(x_ref, g_ref, wb_ref, ws_ref, big_ref, small_ref, xn_scr):
    @pl.when(pl.program_id(1) == 0)
    def _():
        xn_scr[...] = _bf(_rms(x_ref[...], g_ref[...]))
        small_ref[...] = jnp.dot(xn_scr[...], ws_ref[...], preferred_element_type=F32)

    big_ref[...] = _bf(jnp.dot(xn_scr[...], wb_ref[...], preferred_element_type=F32))


def _inproj(x, g, w_big, w_small, tm=1024, tn=1024):
    t = x.shape[0]
    tm = min(tm, t)
    return pl.pallas_call(
        _inproj_kernel,
        grid=(t // tm, N_BIG // tn),
        in_specs=[
            pl.BlockSpec((tm, D_MODEL), lambda i, j: (i, 0)),
            pl.BlockSpec((1, D_MODEL), lambda i, j: (0, 0)),
            pl.BlockSpec((D_MODEL, tn), lambda i, j: (0, j)),
            pl.BlockSpec((D_MODEL, LANES), lambda i, j: (0, 0)),
        ],
        out_specs=[
            pl.BlockSpec((tm, tn), lambda i, j: (i, j)),
            pl.BlockSpec((tm, LANES), lambda i, j: (i, 0)),
        ],
        out_shape=[jax.ShapeDtypeStruct((t, N_BIG), BF16), jax.ShapeDtypeStruct((t, LANES), F32)],
        scratch_shapes=[pltpu.VMEM((tm, D_MODEL), BF16)],
        compiler_params=_cparams("parallel", "arbitrary"),
        name="inproj",
    )(x, g, w_big, w_small)


def _dn_kernel(qkv_ref, gate_ref, sm_ref, cw_ref, prow_ref, nw_ref, o_ref, halo_scr, state_scr):
    @pl.when(pl.program_id(1) == 0)
    def _():
        halo_scr[...] = jnp.zeros_like(halo_scr)
        state_scr[...] = jnp.zeros_like(state_scr)

    row, col, m_incl, m_strict = _chunk_masks()
    tril = jnp.where(m_incl, 1.0, 0.0)
    eye = jnp.where(row == col, 1.0, 0.0)
    first = row[:, 0:1] < CHUNK

    sm = sm_ref[...]
    g_all = -jnp.exp(prow_ref[0:1, :]) * _softplus(sm + prow_ref[1:2, :])
    beta_all = _sigmoid(sm)
    gc_all = _mm_sel_lhs(tril, g_all)
    gct_all = gc_all.T

    def head(h, carry):
        off = pl.multiple_of(h * DN_HEAD_DIM, DN_HEAD_DIM)

        def conv_silu(seg):
            c0 = pl.multiple_of(seg * MIX_W + off, LANES)
            cur = qkv_ref[:, pl.ds(c0, DN_HEAD_DIM)].astype(F32)
            y = _causal_conv(halo_scr[:, pl.ds(c0, DN_HEAD_DIM)], cur, cw_ref[:, pl.ds(c0, DN_HEAD_DIM)], None)
            return _silu(y)

        q = conv_silu(0)
        k = conv_silu(1)
        v = conv_silu(2)
        q = q * lax.rsqrt(jnp.sum(q * q, axis=-1, keepdims=True) + EPS) * (DN_HEAD_DIM ** -0.5)
        k = k * lax.rsqrt(jnp.sum(k * k, axis=-1, keepdims=True) + EPS)

        gc_col = jnp.sum(jnp.where(col == h + LANE_DN_A, gc_all, 0.0), axis=1, keepdims=True)
        beta = jnp.sum(jnp.where(col == h + LANE_DN_B, beta_all, 0.0), axis=1, keepdims=True)
        gc_row = jnp.sum(jnp.where(row == h + LANE_DN_A, gct_all, 0.0), axis=0, keepdims=True)
        decay = jnp.exp(jnp.where(m_incl, gc_col - gc_row, -jnp.inf))

        kb = k * beta
        vb = v * beta
        x = -jnp.where(m_strict, _mm_nt(kb, k) * decay, 0.0)
        tinv = eye + x
        xp = x
        for _ in range(5):
            xp = _mm_x3(xp, xp)
            tinv = tinv + _mm_x3(tinv, xp)
        egc = jnp.exp(gc_col)
        u = _mm_x3(tinv, vb)
        w = _mm_x3(tinv, kb * egc)
        attn = jnp.where(m_incl, _mm_nt(q, k) * decay, 0.0)
        qg = q * egc
        gl0 = gc_col[CHUNK - 1:CHUNK, :]
        gl1 = gc_col[TBLK - 1:TBLK, :]
        kdt = (k * jnp.exp(jnp.where(first, gl0, gl1) - gc_col)).T

        s = state_scr[h]
        outs = []
        pad = jnp.zeros((CHUNK, DN_HEAD_DIM), F32)
        for c, gl in ((0, gl0), (1, gl1)):
            r0, r1 = c * CHUNK, (c + 1) * CHUNK
            v_new = u[r0:r1] - _mm(w[r0:r1], s)
            v_blk = jnp.concatenate([v_new, pad] if c == 0 else [pad, v_new], axis=0)
            outs.append(_mm(qg[r0:r1], s) + _mm(attn[r0:r1, :], v_blk))
            s = s * jnp.exp(gl) + _mm(kdt, v_blk)
        state_scr[h] = s
        o = jnp.concatenate(outs, axis=0)
        gate = gate_ref[:, pl.ds(off, DN_HEAD_DIM)].astype(F32)
        o_ref[:, pl.ds(off, DN_HEAD_DIM)] = _bf(_rms(o, nw_ref[...]) * _silu(gate))
        return carry

    lax.fori_loop(0, DN_HEADS, head, 0)
    halo_scr[...] = qkv_ref[TBLK - 16:TBLK, :].astype(F32)[16 - HALO:16]


def _dn_mixer(big, small, conv_w, prow, norm_w, batch, seq):
    nd = seq // TBLK
    row_map = lambda b, d: (b * nd + d, 0)
    return pl.pallas_call(
        _dn_kernel,
        grid=(batch, nd),
        in_specs=[
            pl.BlockSpec((TBLK, 3 * MIX_W), lambda b, d: (b * nd + d, COL_DN_QKV // (3 * MIX_W))),
            pl.BlockSpec((TBLK, MIX_W), lambda b, d: (b * nd + d, COL_DN_GATE // MIX_W)),
            pl.BlockSpec((TBLK, LANES), row_map),
            pl.BlockSpec((CONV_WIDTH, 3 * MIX_W), lambda b, d: (0, 0)),
            pl.BlockSpec((8, LANES), lambda b, d: (0, 0)),
            pl.BlockSpec((1, DN_HEAD_DIM), lambda b, d: (0, 0)),
        ],
        out_specs=pl.BlockSpec((TBLK, MIX_W), row_map),
        out_shape=jax.ShapeDtypeStruct((batch * seq, MIX_W), BF16),
        scratch_shapes=[pltpu.VMEM((HALO, 3 * MIX_W), F32), pltpu.VMEM((DN_HEADS, DN_HEAD_DIM, DN_HEAD_DIM), F32)],
        compiler_params=_cparams("parallel", "arbitrary"),
        name="deltanet",
    )(big, big, small, conv_w, prow, norm_w)


SB_TILE = 128


def _sb_kernel(q_ref, k_ref, v_ref, w_ref, o_ref):
    qi = pl.program_id(2)
    row = lax.broadcasted_iota(jnp.int32, (SB_TILE, SB_TILE), 0)
    col = lax.broadcasted_iota(jnp.int32, (SB_TILE, SB_TILE), 1)
    tri = col < row
    q = q_ref[...].astype(F32) * (SB_HEAD_DIM ** -0.5)
    wmat = w_ref[...]

    def tile(qh, kb, r_sum, acc, masked):
        start = pl.multiple_of(kb * SB_TILE, SB_TILE)
        kblk = k_ref[pl.ds(start, SB_TILE), :]
        vblk = v_ref[pl.ds(start, SB_TILE), :]
        z = _mm_nt(qh, kblk)
        sp = _softplus(z)
        lk = -sp
        if masked:
            lk = jnp.where(tri, lk, 0.0)
        hi, lo = _split2(lk)
        cs = jnp.dot(jnp.concatenate([hi, lo], axis=1), wmat, preferred_element_type=F32)
        wgt = jnp.exp(z - sp + cs[:, :SB_TILE] + r_sum)
        if masked:
            wgt = jnp.where(tri, wgt, 0.0)
        return r_sum + cs[:, SB_TILE:], acc + _mm(wgt, vblk)

    outs = []
    for h in range(2):
        qh = _bf(jnp.where((col >= SB_HEAD_DIM) == (h == 1), q, 0.0))
        zero = jnp.zeros((SB_TILE, SB_TILE), F32)
        r_sum, acc = tile(qh, qi, zero, zero, True)

        def body(i, carry, qh=qh):
            return tile(qh, qi - 1 - i, carry[0], carry[1], False)

        r_sum, acc = lax.fori_loop(0, qi, body, (r_sum, acc))
        outs.append(acc)
    o_ref[...] = _bf(jnp.where(col < SB_HEAD_DIM, outs[0], outs[1]))


def _sb_mixer(big, wmat, batch, seq):
    nq = seq // SB_TILE
    pairs = SB_HEADS // 2
    qoff = COL_SB_QKV // LANES
    return pl.pallas_call(
        _sb_kernel,
        grid=(batch, pairs, nq),
        in_specs=[
            pl.BlockSpec((SB_TILE, LANES), lambda b, p, i: (b * nq + i, qoff + p)),
            pl.BlockSpec((seq, LANES), lambda b, p, i: (b, qoff + pairs + p)),
            pl.BlockSpec((seq, LANES), lambda b, p, i: (b, qoff + 2 * pairs + p)),
            pl.BlockSpec((2 * SB_TILE, 2 * SB_TILE), lambda b, p, i: (0, 0)),
        ],
        out_specs=pl.BlockSpec((SB_TILE, LANES), lambda b, p, i: (b * nq + i, p)),
        out_shape=jax.ShapeDtypeStruct((batch * seq, MIX_W), BF16),
        compiler_params=_cparams("parallel", "parallel", "arbitrary"),
        name="stickbreak",
    )(big, big, big, wmat)


GROUP_W = MIX_W // SSM_GROUPS
HEADS_PER_GROUP = SSM_HEADS // SSM_GROUPS


def _ssd_kernel(z_ref, xbc_ref, sm_ref, cw_ref, cb_ref, prow_ref, drow_ref, nw_ref, e_ref, o_ref,
                halo_scr, state_scr):
    @pl.when(pl.program_id(1) == 0)
    def _():
        halo_scr[...] = jnp.zeros_like(halo_scr)
        state_scr[...] = jnp.zeros_like(state_scr)

    row, col, m_incl, _ = _chunk_masks()
    tril = jnp.where(m_incl, 1.0, 0.0)
    first = row[:, 0:1] < CHUNK
    gcol = lax.broadcasted_iota(jnp.int32, (TBLK, GROUP_W), 1)

    cur = xbc_ref[...].astype(F32)
    xbc = _silu(_causal_conv(halo_scr[...], cur, cw_ref[...], cb_ref[...]))
    halo_scr[...] = cur[TBLK - HALO:TBLK]
    gn = SSM_GROUPS * SSM_STATE
    xs = xbc[:, :MIX_W]
    bm = xbc[:, MIX_W:MIX_W + gn]
    cm = xbc[:, MIX_W + gn:]

    dt_all = _softplus(sm_ref[...] + prow_ref[1:2, :])
    a_all = -jnp.exp(prow_ref[0:1, :]) * dt_all
    acum_all = _mm_sel_lhs(tril, a_all)
    acum_t = acum_all.T
    e = e_ref[...]
    dt_exp = _mm_sel_rhs(dt_all, e)
    acum_exp = _mm_sel_rhs(acum_all, e)
    xdt = xs * dt_exp

    for g in range(SSM_GROUPS):
        c_g = cm[:, g * SSM_STATE:(g + 1) * SSM_STATE]
        b_g = bm[:, g * SSM_STATE:(g + 1) * SSM_STATE]
        scores = _mm_nt(c_g, b_g)
        lo, hi = g * GROUP_W, (g + 1) * GROUP_W
        x_g = xdt[:, lo:hi]
        a_g = acum_exp[:, lo:hi]
        y_g = jnp.zeros((TBLK, GROUP_W), F32)
        for hh in range(HEADS_PER_GROUP):
            h = g * HEADS_PER_GROUP + hh
            a_col = acum_exp[:, h * SSM_HEAD_DIM:h * SSM_HEAD_DIM + 1]
            a_row = acum_t[LANE_SSM_DT + h:LANE_SSM_DT + h + 1, :]
            lmat = jnp.exp(jnp.where(m_incl, a_col - a_row, -jnp.inf))
            y_g = jnp.where(gcol // SSM_HEAD_DIM == hh, _mm(scores * lmat, x_g), y_g)

        b_t = b_g.T
        a_last0 = a_g[CHUNK - 1:CHUNK, :]
        a_last1 = a_g[TBLK - 1:TBLK, :]
        x_sc = x_g * jnp.exp(jnp.where(first, a_last0, a_last1) - a_g)
        st = state_scr[g]
        yoff = []
        for c, a_last in ((0, a_last0), (1, a_last1)):
            r0, r1 = c * CHUNK, (c + 1) * CHUNK
            yoff.append(_mm(c_g[r0:r1], st) * jnp.exp(a_g[r0:r1]))
            st = st * jnp.exp(a_last) + _mm(b_t, jnp.where(first == (c == 0), x_sc, 0.0))
        state_scr[g] = st
        y_g = y_g + jnp.concatenate(yoff, axis=0) + xs[:, lo:hi] * drow_ref[:, lo:hi]
        y_g = y_g * _silu(z_ref[:, lo:hi].astype(F32))
        o_ref[:, lo:hi] = _bf(_rms(y_g, nw_ref[:, lo:hi]))


def _ssd_mixer(big, small, conv_w, conv_b, prow, drow, norm_w, emat, batch, seq):
    nd = seq // TBLK
    conv_dim = MIX_W + 2 * SSM_GROUPS * SSM_STATE
    row_map = lambda b, d: (b * nd + d, 0)
    const = lambda b, d: (0, 0)
    return pl.pallas_call(
        _ssd_kernel,
        grid=(batch, nd),
        in_specs=[
            pl.BlockSpec((TBLK, MIX_W), lambda b, d: (b * nd + d, COL_SSM_Z // MIX_W)),
            pl.BlockSpec((TBLK, conv_dim), lambda b, d: (b * nd + d, COL_SSM_XBC // conv_dim)),
            pl.BlockSpec((TBLK, LANES), row_map),
            pl.BlockSpec((CONV_WIDTH, conv_dim), const),
            pl.BlockSpec((1, conv_dim), const),
            pl.BlockSpec((8, LANES), const),
            pl.BlockSpec((1, MIX_W), const),
            pl.BlockSpec((1, MIX_W), const),
            pl.BlockSpec((LANES, MIX_W), const),
        ],
        out_specs=pl.BlockSpec((TBLK, MIX_W), row_map),
        out_shape=jax.ShapeDtypeStruct((batch * seq, MIX_W), BF16),
        scratch_shapes=[pltpu.VMEM((HALO, conv_dim), F32), pltpu.VMEM((SSM_GROUPS, SSM_STATE, GROUP_W), F32)],
        compiler_params=_cparams("parallel", "arbitrary"),
        name="ssd",
    )(big, big, small, conv_w, conv_b, prow, drow, norm_w, emat)


def _merge_kernel(x_ref, odn_ref, osb_ref, ossm_ref, g0_ref, g1_ref, g2_ref, wb_ref, wo_ref, o_ref):
    m = _sigmoid(g0_ref[...].astype(F32)) * jnp.dot(odn_ref[...], wb_ref[0], preferred_element_type=F32)
    m = m + _sigmoid(g1_ref[...].astype(F32)) * jnp.dot(osb_ref[...], wb_ref[1], preferred_element_type=F32)
    m = m + _sigmoid(g2_ref[...].astype(F32)) * jnp.dot(ossm_ref[...], wb_ref[2], preferred_element_type=F32)
    o_ref[...] = x_ref[...] + jnp.dot(_bf(m), wo_ref[...], preferred_element_type=F32)


def _merge(x, o_dn, o_sb, o_ssm, big, w_branch, w_out, tm=512):
    t = x.shape[0]
    tm = min(tm, t)
    rows = lambda i: (i, 0)
    gcol = COL_GATES // D_MODEL
    return pl.pallas_call(
        _merge_kernel,
        grid=(t // tm,),
        in_specs=[
            pl.BlockSpec((tm, D_MODEL), rows),
            pl.BlockSpec((tm, MIX_W), rows),
            pl.BlockSpec((tm, MIX_W), rows),
            pl.BlockSpec((tm, MIX_W), rows),
            pl.BlockSpec((tm, D_MODEL), lambda i: (i, gcol)),
            pl.BlockSpec((tm, D_MODEL), lambda i: (i, gcol + 1)),
            pl.BlockSpec((tm, D_MODEL), lambda i: (i, gcol + 2)),
            pl.BlockSpec((3, MIX_W, D_MODEL), lambda i: (0, 0, 0)),
            pl.BlockSpec((D_MODEL, D_MODEL), lambda i: (0, 0)),
        ],
        out_specs=pl.BlockSpec((tm, D_MODEL), rows),
        out_shape=jax.ShapeDtypeStruct((t, D_MODEL), F32),
        compiler_params=_cparams("parallel"),
        name="merge",
    )(x, o_dn, o_sb, o_ssm, big, big, big, w_branch, w_out)


def _mlp_kernel(x_ref, g_ref, wu_ref, wd_ref, gf_ref, o_ref, xn_scr, acc_scr, *, final_norm):
    j = pl.program_id(1)

    @pl.when(j == 0)
    def _():
        xn_scr[...] = _bf(_rms(x_ref[...], g_ref[...]))
        acc_scr[...] = jnp.zeros_like(acc_scr)

    h = jnp.dot(xn_scr[...], wu_ref[...], preferred_element_type=F32)
    h = jnp.square(jnp.maximum(h, 0.0))
    acc_scr[...] += jnp.dot(_bf(h), wd_ref[...], preferred_element_type=F32)

    @pl.when(j == pl.num_programs(1) - 1)
    def _():
        y = x_ref[...] + acc_scr[...]
        o_ref[...] = _rms(y, gf_ref[...]) if final_norm else y


def _mlp(x, g, w_up, w_down, g_final, final_norm, tm=1024, tf=1024):
    t = x.shape[0]
    tm = min(tm, t)
    return pl.pallas_call(
        functools.partial(_mlp_kernel, final_norm=final_norm),
        grid=(t // tm, D_FF // tf),
        in_specs=[
            pl.BlockSpec((tm, D_MODEL), lambda i, j: (i, 0)),
            pl.BlockSpec((1, D_MODEL), lambda i, j: (0, 0)),
            pl.BlockSpec((D_MODEL, tf), lambda i, j: (0, j)),
            pl.BlockSpec((tf, D_MODEL), lambda i, j: (j, 0)),
            pl.BlockSpec((1, D_MODEL), lambda i, j: (0, 0)),
        ],
        out_specs=pl.BlockSpec((tm, D_MODEL), lambda i, j: (i, 0)),
        out_shape=jax.ShapeDtypeStruct((t, D_MODEL), F32),
        scratch_shapes=[pltpu.VMEM((tm, D_MODEL), BF16), pltpu.VMEM((tm, D_MODEL), F32)],
        compiler_params=_cparams("parallel", "arbitrary"),
        name="mlp",
    )(x, g, w_up, w_down, g_final)


def _lane_row(values, lane0):
    return jnp.zeros((LANES,), F32).at[lane0:lane0 + values.shape[0]].set(values.astype(F32))


def _param_rows(a_log, dt_bias, lane0):
    rows = jnp.zeros((8, LANES), F32)
    return rows.at[0].set(_lane_row(a_log, lane0)).at[1].set(_lane_row(dt_bias, lane0))


def _sb_cumsum_matrix():
    j = jnp.arange(SB_TILE)[:, None]
    s = jnp.arange(SB_TILE)[None, :]
    half = jnp.concatenate([(j > s).astype(BF16), jnp.ones((SB_TILE, SB_TILE), BF16)], axis=1)
    return jnp.concatenate([half, half], axis=0)


def _ssm_expand_matrix():
    lane = jnp.arange(LANES)[:, None]
    ch = jnp.arange(MIX_W)[None, :]
    return (lane == LANE_SSM_DT + ch // SSM_HEAD_DIM).astype(BF16)


def _split_w_in(w):
    big = jnp.concatenate([w[:, 0:4096], w[:, 4112:10256], w[:, 10272:13344]], axis=1)
    small = jnp.concatenate([w[:, 4096:4112], w[:, 10256:10272], jnp.zeros((D_MODEL, LANES - 32), w.dtype)], axis=1)
    return _bf(big), _bf(small)


def kernel(x, norm_mix, w_in, dn_conv_w, dn_a_log, dn_dt_bias, dn_norm_w, ssm_conv_w, ssm_conv_b, ssm_a_log,
           ssm_dt_bias, ssm_d, ssm_norm_w, w_branch, w_out, norm_mlp, w_up, w_down, norm_final):
    batch, seq, _ = x.shape
    depth = w_in.shape[0]
    h = x.reshape(batch * seq, D_MODEL)
    sb_w = _sb_cumsum_matrix()
    emat = _ssm_expand_matrix()
    g_final = norm_final.reshape(1, D_MODEL)
    for l in range(depth):
        w_big, w_small = _split_w_in(w_in[l])
        big, small = _inproj(h, norm_mix[l].reshape(1, D_MODEL), w_big, w_small)
        o_dn = _dn_mixer(big, small, dn_conv_w[l], _param_rows(dn_a_log[l], dn_dt_bias[l], LANE_DN_A),
                         dn_norm_w[l].reshape(1, DN_HEAD_DIM), batch, seq)
        o_sb = _sb_mixer(big, sb_w, batch, seq)
        o_ssm = _ssd_mixer(big, small, ssm_conv_w[l], ssm_conv_b[l].reshape(1, -1),
                           _param_rows(ssm_a_log[l], ssm_dt_bias[l], LANE_SSM_DT),
                           jnp.repeat(ssm_d[l], SSM_HEAD_DIM).reshape(1, MIX_W),
                           ssm_norm_w[l].reshape(1, MIX_W), emat, batch, seq)
        h = _merge(h, o_dn, o_sb, o_ssm, big, _bf(w_branch[l]), _bf(w_out[l]))
        h = _mlp(h, norm_mlp[l].reshape(1, D_MODEL), _bf(w_up[l]), _bf(w_down[l]), g_final,
                 final_norm=(l == depth - 1))
    return h.reshape(batch, seq, D_MODEL)
```

```python
import functools

import jax
import jax.numpy as jnp
from jax import lax
from jax.experimental import pallas as pl
from jax.experimental.pallas import tpu as pltpu

F32 = jnp.float32
BF16 = jnp.bfloat16

D_MODEL = 1024
MIX_W = D_MODEL
DN_HEADS = 8
DN_HEAD_DIM = 128
SB_HEADS = 16
SB_HEAD_DIM = 64
SSM_HEADS = 16
SSM_HEAD_DIM = 64
SSM_STATE = 128
SSM_GROUPS = 4
CHUNK = 64
D_FF = 4 * D_MODEL
EPS = 1e-6
CONV_WIDTH = 4

LANES = 128
TBLK = 2 * CHUNK
HALO = 8

COL_DN_QKV = 0
COL_DN_GATE = 3072
COL_SB_QKV = 4096
COL_SSM_Z = 7168
COL_SSM_XBC = 8192
COL_GATES = 10240
N_BIG = 13312
LANE_DN_A = 0
LANE_DN_B = 8
LANE_SSM_DT = 16

VMEM_LIMIT = 56 * 1024 * 1024


def _cparams(*sem):
    return pltpu.CompilerParams(dimension_semantics=sem, vmem_limit_bytes=VMEM_LIMIT)


def _bf(x):
    return x.astype(BF16)


def _mm(a, b):
    return jnp.dot(_bf(a), _bf(b), preferred_element_type=F32)


def _mm_nt(a, b):
    return lax.dot_general(_bf(a), _bf(b), (((1,), (1,)), ((), ())), preferred_element_type=F32)


def _split2(a):
    hi = _bf(a)
    lo = _bf(a - hi.astype(F32))
    return hi, lo


def _split3(a):
    hi = _bf(a)
    r = a - hi.astype(F32)
    mid = _bf(r)
    lo = _bf(r - mid.astype(F32))
    return hi, mid, lo


def _mm_x3s(a_split, b_split):
    ah, al = a_split
    bh, bl = b_split
    lhs = jnp.concatenate([ah, al, ah], axis=1)
    rhs = jnp.concatenate([bh, bh, bl], axis=0)
    return jnp.dot(lhs, rhs, preferred_element_type=F32)


def _mm_sel_lhs(sel, b):
    s = _bf(sel)
    bh, bm, bl = _split3(b)
    return jnp.dot(jnp.concatenate([s, s, s], axis=1), jnp.concatenate([bh, bm, bl], axis=0),
                   preferred_element_type=F32)


def _mm_sel_rhs(a, sel_bf):
    ah, am, al = _split3(a)
    return jnp.dot(jnp.concatenate([ah, am, al], axis=1), jnp.concatenate([sel_bf, sel_bf, sel_bf], axis=0),
                   preferred_element_type=F32)


def _softplus(x):
    return jnp.maximum(x, 0.0) + jnp.log(1.0 + jnp.exp(-jnp.abs(x)))


def _sigmoid(x):
    return 1.0 / (1.0 + jnp.exp(-x))


def _silu(x):
    return x * _sigmoid(x)


def _rms(x, w):
    return x * lax.rsqrt(jnp.mean(x * x, axis=-1, keepdims=True) + EPS) * w


def _causal_conv(halo, cur, w, bias):
    rows = cur.shape[0]
    xx = jnp.concatenate([halo, cur], axis=0)
    y = w[CONV_WIDTH - 1:CONV_WIDTH, :] * cur
    for k in range(CONV_WIDTH - 1):
        shifted = pltpu.roll(xx, CONV_WIDTH - 1 - k, 0)[HALO:HALO + rows]
        y = y + w[k:k + 1, :] * shifted
    if bias is not None:
        y = y + bias
    return y


def _chunk_masks():
    row = lax.broadcasted_iota(jnp.int32, (TBLK, TBLK), 0)
    col = lax.broadcasted_iota(jnp.int32, (TBLK, TBLK), 1)
    same = (row >= CHUNK) == (col >= CHUNK)
    return row, col, same & (col <= row), same & (col < row)


def _inproj_kernel(x_ref, g_ref, wb_ref, ws_ref, big_ref, small_ref, xn_scr):
    @pl.when(pl.program_id(1) == 0)
    def _():
        xn_scr[...] = _bf(_rms(x_ref[...], g_ref[...]))
        small_ref[...] = jnp.dot(xn_scr[...], ws_ref[...], preferred_element_type=F32)

    big_ref[...] = _bf(jnp.dot(xn_scr[...], wb_ref[...], preferred_element_type=F32))


def _inproj(x, g, w_big, w_small, tm=1024, tn=1024):
    t = x.shape[0]
    tm = min(tm, t)
    return pl.pallas_call(
        _inproj_kernel,
        grid=(t // tm, N_BIG // tn),
        in_specs=[
            pl.BlockSpec((tm, D_MODEL), lambda i, j: (i, 0)),
            pl.BlockSpec((1, D_MODEL), lambda i, j: (0, 0)),
            pl.BlockSpec((D_MODEL, tn), lambda i, j: (0, j)),
            pl.BlockSpec((D_MODEL, LANES), lambda i, j: (0, 0)),
        ],
        out_specs=[
            pl.BlockSpec((tm, tn), lambda i, j: (i, j)),
            pl.BlockSpec((tm, LANES), lambda i, j: (i, 0)),
        ],
        out_shape=[jax.ShapeDtypeStruct((t, N_BIG), BF16), jax.ShapeDtypeStruct((t, LANES), F32)],
        scratch_shapes=[pltpu.VMEM((tm, D_MODEL), BF16)],
        compiler_params=_cparams("parallel", "arbitrary"),
        name="inproj",
    )(x, g, w_big, w_small)


def _dn_kernel(qkv_ref, gate_ref, sm_ref, cw_ref, prow_ref, nw_ref, o_ref, halo_scr, state_scr):
    @pl.when(pl.program_id(1) == 0)
    def _():
        halo_scr[...] = jnp.zeros_like(halo_scr)
        state_scr[...] = jnp.zeros_like(state_scr)

    row, col, m_incl, m_strict = _chunk_masks()
    tril = jnp.where(m_incl, 1.0, 0.0)
    eye = jnp.where(row == col, 1.0, 0.0)
    first = row[:, 0:1] < CHUNK

    sm = sm_ref[...]
    g_all = -jnp.exp(prow_ref[0:1, :]) * _softplus(sm + prow_ref[1:2, :])
    beta_all = _sigmoid(sm)
    gc_all = _mm_sel_lhs(tril, g_all)
    gct_all = gc_all.T

    heads = range(DN_HEADS)

    def conv_silu(seg, h):
        lanes = slice(seg * MIX_W + h * DN_HEAD_DIM, seg * MIX_W + (h + 1) * DN_HEAD_DIM)
        cur = qkv_ref[:, lanes].astype(F32)
        return _silu(_causal_conv(halo_scr[:, lanes], cur, cw_ref[:, lanes], None))

    q, k, v, kb, vb, decay, gc_col, kk, qk = {}, {}, {}, {}, {}, {}, {}, {}, {}
    for h in heads:
        qh = conv_silu(0, h)
        kh = conv_silu(1, h)
        v[h] = conv_silu(2, h)
        q[h] = qh * lax.rsqrt(jnp.sum(qh * qh, axis=-1, keepdims=True) + EPS) * (DN_HEAD_DIM ** -0.5)
        k[h] = kh * lax.rsqrt(jnp.sum(kh * kh, axis=-1, keepdims=True) + EPS)
        gc_col[h] = gc_all[:, LANE_DN_A + h:LANE_DN_A + h + 1]
        beta = beta_all[:, LANE_DN_B + h:LANE_DN_B + h + 1]
        gc_row = gct_all[LANE_DN_A + h:LANE_DN_A + h + 1, :]
        decay[h] = jnp.exp(jnp.where(m_incl, gc_col[h] - gc_row, -jnp.inf))
        kb[h] = k[h] * beta
        vb[h] = v[h] * beta
    for h in heads:
        kk[h] = _mm_nt(kb[h], k[h])
        qk[h] = _mm_nt(q[h], k[h])

    xs, tinv = {}, {}
    for h in heads:
        x = -jnp.where(m_strict, kk[h] * decay[h], 0.0)
        xs[h] = _split2(x)
        tinv[h] = eye + x
    for _ in range(5):
        for h in heads:
            xs[h] = _split2(_mm_x3s(xs[h], xs[h]))
        for h in heads:
            tinv[h] = tinv[h] + _mm_x3s(_split2(tinv[h]), xs[h])

    u, w, attn, qg, kdt, gl = {}, {}, {}, {}, {}, {}
    for h in heads:
        egc = jnp.exp(gc_col[h])
        ts = _split2(tinv[h])
        u[h] = _mm_x3s(ts, _split2(vb[h]))
        w[h] = _mm_x3s(ts, _split2(kb[h] * egc))
        attn[h] = _bf(jnp.where(m_incl, qk[h] * decay[h], 0.0))
        qg[h] = _bf(q[h] * egc)
        gl[h] = (gc_col[h][CHUNK - 1:CHUNK, :], gc_col[h][TBLK - 1:TBLK, :])
        kdt[h] = _bf((k[h] * jnp.exp(jnp.where(first, gl[h][0], gl[h][1]) - gc_col[h])).T)

    s = {h: state_scr[h] for h in heads}
    outs = {h: [] for h in heads}
    pad = jnp.zeros((CHUNK, DN_HEAD_DIM), F32)
    for c in range(2):
        r0, r1 = c * CHUNK, (c + 1) * CHUNK
        v_blk = {}
        for h in heads:
            v_new = u[h][r0:r1] - _mm(w[h][r0:r1], s[h])
            v_blk[h] = _bf(jnp.concatenate([v_new, pad] if c == 0 else [pad, v_new], axis=0))
        for h in heads:
            outs[h].append(_mm(qg[h][r0:r1], s[h]) + _mm(attn[h][r0:r1, :], v_blk[h]))
            s[h] = s[h] * jnp.exp(gl[h][c]) + _mm(kdt[h], v_blk[h])
    for h in heads:
        state_scr[h] = s[h]
        lanes = slice(h * DN_HEAD_DIM, (h + 1) * DN_HEAD_DIM)
        o = jnp.concatenate(outs[h], axis=0)
        o_ref[:, lanes] = _bf(_rms(o, nw_ref[...]) * _silu(gate_ref[:, lanes].astype(F32)))
    halo_scr[...] = qkv_ref[TBLK - 16:TBLK, :].astype(F32)[16 - HALO:16]


def _dn_mixer(big, small, conv_w, prow, norm_w, batch, seq):
    nd = seq // TBLK
    row_map = lambda b, d: (b * nd + d, 0)
    return pl.pallas_call(
        _dn_kernel,
        grid=(batch, nd),
        in_specs=[
            pl.BlockSpec((TBLK, 3 * MIX_W), lambda b, d: (b * nd + d, COL_DN_QKV // (3 * MIX_W))),
            pl.BlockSpec((TBLK, MIX_W), lambda b, d: (b * nd + d, COL_DN_GATE // MIX_W)),
            pl.BlockSpec((TBLK, LANES), row_map),
            pl.BlockSpec((CONV_WIDTH, 3 * MIX_W), lambda b, d: (0, 0)),
            pl.BlockSpec((8, LANES), lambda b, d: (0, 0)),
            pl.BlockSpec((1, DN_HEAD_DIM), lambda b, d: (0, 0)),
        ],
        out_specs=pl.BlockSpec((TBLK, MIX_W), row_map),
        out_shape=jax.ShapeDtypeStruct((batch * seq, MIX_W), BF16),
        scratch_shapes=[pltpu.VMEM((HALO, 3 * MIX_W), F32), pltpu.VMEM((DN_HEADS, DN_HEAD_DIM, DN_HEAD_DIM), F32)],
        compiler_params=_cparams("parallel", "arbitrary"),
        name="deltanet",
    )(big, big, small, conv_w, prow, norm_w)


SB_TILE = 128
SB_TQ = 256
SB_SUB = SB_TQ // SB_TILE
SB_STRIP = 128
SB_HEADS_PER_STEP = 4
LOG2E = 1.4426950408889634


def _sb_kernel(q_ref, k_ref, v_ref, w_ref, o_ref, r_scr, acc_scr):
    qi = pl.program_id(2)
    row = lax.broadcasted_iota(jnp.int32, (SB_STRIP, SB_TILE), 0)
    col = lax.broadcasted_iota(jnp.int32, (SB_STRIP, SB_TILE), 1)
    wmat = w_ref[...]
    r_scr[...] = jnp.zeros_like(r_scr)
    acc_scr[...] = jnp.zeros_like(acc_scr)
    strips = range(SB_TQ // SB_STRIP)
    heads = range(SB_HEADS_PER_STEP)
    qh = {}
    for s in strips:
        for h in heads:
            lanes = slice((h // 2) * LANES, (h // 2 + 1) * LANES)
            q = q_ref[s * SB_STRIP:(s + 1) * SB_STRIP, lanes].astype(F32) * (SB_HEAD_DIM ** -0.5)
            qh[s, h] = _bf(jnp.where((col >= SB_HEAD_DIM) == (h % 2 == 1), q, 0.0))

    def tiles(kb_lo, diagonal):
        start = pl.multiple_of(kb_lo * SB_TILE, SB_TILE)
        kblk = [k_ref[pl.ds(start, SB_SUB * SB_TILE), p * LANES:(p + 1) * LANES] for p in range(SB_HEADS_PER_STEP // 2)]
        vblk = [v_ref[pl.ds(start, SB_SUB * SB_TILE), p * LANES:(p + 1) * LANES] for p in range(SB_HEADS_PER_STEP // 2)]
        order = list(reversed(range(SB_SUB)))
        chains = [(t, s, h) for t in order for s in strips for h in heads]
        z = {(t, s, h): _mm_nt(qh[s, h], kblk[h // 2][t * SB_TILE:(t + 1) * SB_TILE]) * LOG2E for t, s, h in chains}
        zs, cs, tri = {}, {}, {}
        for c in chains:
            t, s, _ = c
            neg_abs = pltpu.bitcast(pltpu.bitcast(z[c], jnp.uint32) | jnp.uint32(0x80000000), F32)
            sp = jnp.maximum(z[c], 0.0) + jnp.log2(1.0 + jnp.exp2(neg_abs))
            zs[c] = z[c] - sp
            if diagonal:
                tri[c] = col + t * SB_TILE < row + s * SB_STRIP
                sp = jnp.where(tri[c], sp, 0.0)
            hi, lo = _split2(sp)
            cs[c] = jnp.dot(jnp.concatenate([hi, lo], axis=1), wmat, preferred_element_type=F32)
        for s in strips:
            rows = slice(s * SB_STRIP, (s + 1) * SB_STRIP)
            for h in heads:
                r_sum = r_scr[h, rows]
                wgts = {}
                for t in order:
                    c = (t, s, h)
                    wgt = jnp.exp2(zs[c] - cs[c][:, :SB_TILE] - r_sum)
                    wgts[t] = _bf(jnp.where(tri[c], wgt, 0.0) if diagonal else wgt)
                    r_sum = r_sum + cs[c][:, SB_TILE:]
                r_scr[h, rows] = r_sum
                acc_scr[h, rows] += jnp.dot(jnp.concatenate([wgts[t] for t in range(SB_SUB)], axis=1), vblk[h // 2],
                                            preferred_element_type=F32)

    tiles(qi * SB_SUB, True)

    def body(i, carry):
        tiles((qi - 1 - i) * SB_SUB, False)
        return carry

    lax.fori_loop(0, qi, body, 0)
    lane = lax.broadcasted_iota(jnp.int32, (SB_TQ, SB_TILE), 1)
    for p in range(SB_HEADS_PER_STEP // 2):
        o_ref[:, p * LANES:(p + 1) * LANES] = _bf(jnp.where(lane < SB_HEAD_DIM, acc_scr[2 * p], acc_scr[2 * p + 1]))


def _sb_mixer(big, wmat, batch, seq):
    nq = seq // SB_TQ
    width = SB_HEADS_PER_STEP * SB_HEAD_DIM
    groups = MIX_W // width
    qoff = COL_SB_QKV // width
    return pl.pallas_call(
        _sb_kernel,
        grid=(batch, groups, nq),
        in_specs=[
            pl.BlockSpec((SB_TQ, width), lambda b, p, i: (b * nq + i, qoff + p)),
            pl.BlockSpec((seq, width), lambda b, p, i: (b, qoff + groups + p)),
            pl.BlockSpec((seq, width), lambda b, p, i: (b, qoff + 2 * groups + p)),
            pl.BlockSpec((2 * SB_TILE, 2 * SB_TILE), lambda b, p, i: (0, 0)),
        ],
        out_specs=pl.BlockSpec((SB_TQ, width), lambda b, p, i: (b * nq + i, p)),
        out_shape=jax.ShapeDtypeStruct((batch * seq, MIX_W), BF16),
        scratch_shapes=[pltpu.VMEM((SB_HEADS_PER_STEP, SB_TQ, SB_TILE), F32),
                        pltpu.VMEM((SB_HEADS_PER_STEP, SB_TQ, SB_TILE), F32)],
        compiler_params=_cparams("parallel", "parallel", "arbitrary"),
        name="stickbreak",
    )(big, big, big, wmat)


GROUP_W = MIX_W // SSM_GROUPS
HEADS_PER_GROUP = SSM_HEADS // SSM_GROUPS


def _ssd_kernel(z_ref, xbc_ref, sm_ref, cw_ref, cb_ref, prow_ref, drow_ref, nw_ref, e_ref, o_ref,
                halo_scr, state_scr):
    @pl.when(pl.program_id(1) == 0)
    def _():
        halo_scr[...] = jnp.zeros_like(halo_scr)
        state_scr[...] = jnp.zeros_like(state_scr)

    row, col, m_incl, _ = _chunk_masks()
    tril = jnp.where(m_incl, 1.0, 0.0)
    first = row[:, 0:1] < CHUNK
    gcol = lax.broadcasted_iota(jnp.int32, (TBLK, GROUP_W), 1)

    cur = xbc_ref[...].astype(F32)
    xbc = _silu(_causal_conv(halo_scr[...], cur, cw_ref[...], cb_ref[...]))
    halo_scr[...] = cur[TBLK - HALO:TBLK]
    gn = SSM_GROUPS * SSM_STATE
    xs = xbc[:, :MIX_W]
    bm = xbc[:, MIX_W:MIX_W + gn]
    cm = xbc[:, MIX_W + gn:]

    dt_all = _softplus(sm_ref[...] + prow_ref[1:2, :])
    a_all = -jnp.exp(prow_ref[0:1, :]) * dt_all
    acum_all = _mm_sel_lhs(tril, a_all)
    acum_t = acum_all.T
    e = e_ref[...]
    dt_exp = _mm_sel_rhs(dt_all, e)
    acum_exp = _mm_sel_rhs(acum_all, e)
    xdt = xs * dt_exp

    for g in range(SSM_GROUPS):
        c_g = cm[:, g * SSM_STATE:(g + 1) * SSM_STATE]
        b_g = bm[:, g * SSM_STATE:(g + 1) * SSM_STATE]
        scores = _mm_nt(c_g, b_g)
        lo, hi = g * GROUP_W, (g + 1) * GROUP_W
        x_g = xdt[:, lo:hi]
        a_g = acum_exp[:, lo:hi]
        y_g = jnp.zeros((TBLK, GROUP_W), F32)
        for hh in range(HEADS_PER_GROUP):
            h = g * HEADS_PER_GROUP + hh
            a_col = acum_exp[:, h * SSM_HEAD_DIM:h * SSM_HEAD_DIM + 1]
            a_row = acum_t[LANE_SSM_DT + h:LANE_SSM_DT + h + 1, :]
            lmat = jnp.exp(jnp.where(m_incl, a_col - a_row, -jnp.inf))
            y_g = jnp.where(gcol // SSM_HEAD_DIM == hh, _mm(scores * lmat, x_g), y_g)

        b_t = b_g.T
        a_last0 = a_g[CHUNK - 1:CHUNK, :]
        a_last1 = a_g[TBLK - 1:TBLK, :]
        x_sc = x_g * jnp.exp(jnp.where(first, a_last0, a_last1) - a_g)
        st = state_scr[g]
        yoff = []
        for c, a_last in ((0, a_last0), (1, a_last1)):
            r0, r1 = c * CHUNK, (c + 1) * CHUNK
            yoff.append(_mm(c_g[r0:r1], st) * jnp.exp(a_g[r0:r1]))
            st = st * jnp.exp(a_last) + _mm(b_t, jnp.where(first == (c == 0), x_sc, 0.0))
        state_scr[g] = st
        y_g = y_g + jnp.concatenate(yoff, axis=0) + xs[:, lo:hi] * drow_ref[:, lo:hi]
        y_g = y_g * _silu(z_ref[:, lo:hi].astype(F32))
        o_ref[:, lo:hi] = _bf(_rms(y_g, nw_ref[:, lo:hi]))


def _ssd_mixer(big, small, conv_w, conv_b, prow, drow, norm_w, emat, batch, seq):
    nd = seq // TBLK
    conv_dim = MIX_W + 2 * SSM_GROUPS * SSM_STATE
    row_map = lambda b, d: (b * nd + d, 0)
    const = lambda b, d: (0, 0)
    return pl.pallas_call(
        _ssd_kernel,
        grid=(batch, nd),
        in_specs=[
            pl.BlockSpec((TBLK, MIX_W), lambda b, d: (b * nd + d, COL_SSM_Z // MIX_W)),
            pl.BlockSpec((TBLK, conv_dim), lambda b, d: (b * nd + d, COL_SSM_XBC // conv_dim)),
            pl.BlockSpec((TBLK, LANES), row_map),
            pl.BlockSpec((CONV_WIDTH, conv_dim), const),
            pl.BlockSpec((1, conv_dim), const),
            pl.BlockSpec((8, LANES), const),
            pl.BlockSpec((1, MIX_W), const),
            pl.BlockSpec((1, MIX_W), const),
            pl.BlockSpec((LANES, MIX_W), const),
        ],
        out_specs=pl.BlockSpec((TBLK, MIX_W), row_map),
        out_shape=jax.ShapeDtypeStruct((batch * seq, MIX_W), BF16),
        scratch_shapes=[pltpu.VMEM((HALO, conv_dim), F32), pltpu.VMEM((SSM_GROUPS, SSM_STATE, GROUP_W), F32)],
        compiler_params=_cparams("parallel", "arbitrary"),
        name="ssd",
    )(big, big, small, conv_w, conv_b, prow, drow, norm_w, emat)


def _merge_kernel(x_ref, odn_ref, osb_ref, ossm_ref, g0_ref, g1_ref, g2_ref, wb_ref, wo_ref, o_ref):
    m = _sigmoid(g0_ref[...].astype(F32)) * jnp.dot(odn_ref[...], wb_ref[0], preferred_element_type=F32)
    m = m + _sigmoid(g1_ref[...].astype(F32)) * jnp.dot(osb_ref[...], wb_ref[1], preferred_element_type=F32)
    m = m + _sigmoid(g2_ref[...].astype(F32)) * jnp.dot(ossm_ref[...], wb_ref[2], preferred_element_type=F32)
    o_ref[...] = x_ref[...] + jnp.dot(_bf(m), wo_ref[...], preferred_element_type=F32)


def _merge(x, o_dn, o_sb, o_ssm, big, w_branch, w_out, tm=512):
    t = x.shape[0]
    tm = min(tm, t)
    rows = lambda i: (i, 0)
    gcol = COL_GATES // D_MODEL
    return pl.pallas_call(
        _merge_kernel,
        grid=(t // tm,),
        in_specs=[
            pl.BlockSpec((tm, D_MODEL), rows),
            pl.BlockSpec((tm, MIX_W), rows),
            pl.BlockSpec((tm, MIX_W), rows),
            pl.BlockSpec((tm, MIX_W), rows),
            pl.BlockSpec((tm, D_MODEL), lambda i: (i, gcol)),
            pl.BlockSpec((tm, D_MODEL), lambda i: (i, gcol + 1)),
            pl.BlockSpec((tm, D_MODEL), lambda i: (i, gcol + 2)),
            pl.BlockSpec((3, MIX_W, D_MODEL), lambda i: (0, 0, 0)),
            pl.BlockSpec((D_MODEL, D_MODEL), lambda i: (0, 0)),
        ],
        out_specs=pl.BlockSpec((tm, D_MODEL), rows),
        out_shape=jax.ShapeDtypeStruct((t, D_MODEL), F32),
        compiler_params=_cparams("parallel"),
        name="merge",
    )(x, o_dn, o_sb, o_ssm, big, big, big, w_branch, w_out)


def _mlp_kernel(x_ref, g_ref, wu_ref, wd_ref, gf_ref, o_ref, xn_scr, acc_scr, *, final_norm):
    j = pl.program_id(1)

    @pl.when(j == 0)
    def _():
        xn_scr[...] = _bf(_rms(x_ref[...], g_ref[...]))
        acc_scr[...] = jnp.zeros_like(acc_scr)

    h = jnp.dot(xn_scr[...], wu_ref[...], preferred_element_type=F32)
    h = jnp.square(jnp.maximum(h, 0.0))
    acc_scr[...] += jnp.dot(_bf(h), wd_ref[...], preferred_element_type=F32)

    @pl.when(j == pl.num_programs(1) - 1)
    def _():
        y = x_ref[...] + acc_scr[...]
        o_ref[...] = _rms(y, gf_ref[...]) if final_norm else y


def _mlp(x, g, w_up, w_down, g_final, final_norm, tm=1024, tf=1024):
    t = x.shape[0]
    tm = min(tm, t)
    return pl.pallas_call(
        functools.partial(_mlp_kernel, final_norm=final_norm),
        grid=(t // tm, D_FF // tf),
        in_specs=[
            pl.BlockSpec((tm, D_MODEL), lambda i, j: (i, 0)),
            pl.BlockSpec((1, D_MODEL), lambda i, j: (0, 0)),
            pl.BlockSpec((D_MODEL, tf), lambda i, j: (0, j)),
            pl.BlockSpec((tf, D_MODEL), lambda i, j: (j, 0)),
            pl.BlockSpec((1, D_MODEL), lambda i, j: (0, 0)),
        ],
        out_specs=pl.BlockSpec((tm, D_MODEL), lambda i, j: (i, 0)),
        out_shape=jax.ShapeDtypeStruct((t, D_MODEL), F32),
        scratch_shapes=[pltpu.VMEM((tm, D_MODEL), BF16), pltpu.VMEM((tm, D_MODEL), F32)],
        compiler_params=_cparams("parallel", "arbitrary"),
        name="mlp",
    )(x, g, w_up, w_down, g_final)


def _lane_row(values, lane0):
    return jnp.zeros((LANES,), F32).at[lane0:lane0 + values.shape[0]].set(values.astype(F32))


def _param_rows(a_log, dt_bias, lane0):
    rows = jnp.zeros((8, LANES), F32)
    return rows.at[0].set(_lane_row(a_log, lane0)).at[1].set(_lane_row(dt_bias, lane0))


def _sb_cumsum_matrix():
    j = jnp.arange(SB_TILE)[:, None]
    s = jnp.arange(SB_TILE)[None, :]
    half = jnp.concatenate([(j > s).astype(BF16), jnp.ones((SB_TILE, SB_TILE), BF16)], axis=1)
    return jnp.concatenate([half, half], axis=0)


def _ssm_expand_matrix():
    lane = jnp.arange(LANES)[:, None]
    ch = jnp.arange(MIX_W)[None, :]
    return (lane == LANE_SSM_DT + ch // SSM_HEAD_DIM).astype(BF16)


def _split_w_in(w):
    big = jnp.concatenate([w[:, 0:4096], w[:, 4112:10256], w[:, 10272:13344]], axis=1)
    small = jnp.concatenate([w[:, 4096:4112], w[:, 10256:10272], jnp.zeros((D_MODEL, LANES - 32), w.dtype)], axis=1)
    return _bf(big), _bf(small)


def kernel(x, norm_mix, w_in, dn_conv_w, dn_a_log, dn_dt_bias, dn_norm_w, ssm_conv_w, ssm_conv_b, ssm_a_log,
           ssm_dt_bias, ssm_d, ssm_norm_w, w_branch, w_out, norm_mlp, w_up, w_down, norm_final):
    batch, seq, _ = x.shape
    depth = w_in.shape[0]
    h = x.reshape(batch * seq, D_MODEL)
    sb_w = _sb_cumsum_matrix()
    emat = _ssm_expand_matrix()
    g_final = norm_final.reshape(1, D_MODEL)
    for l in range(depth):
        w_big, w_small = _split_w_in(w_in[l])
        big, small = _inproj(h, norm_mix[l].reshape(1, D_MODEL), w_big, w_small)
        o_dn = _dn_mixer(big, small, dn_conv_w[l], _param_rows(dn_a_log[l], dn_dt_bias[l], LANE_DN_A),
                         dn_norm_w[l].reshape(1, DN_HEAD_DIM), batch, seq)
        o_sb = _sb_mixer(big, sb_w, batch, seq)
        o_ssm = _ssd_mixer(big, small, ssm_conv_w[l], ssm_conv_b[l].reshape(1, -1),
                           _param_rows(ssm_a_log[l], ssm_dt_bias[l], LANE_SSM_DT),
                           jnp.repeat(ssm_d[l], SSM_HEAD_DIM).reshape(1, MIX_W),
                           ssm_norm_w[l].reshape(1, MIX_W), emat, batch, seq)
        h = _merge(h, o_dn, o_sb, o_ssm, big, _bf(w_branch[l]), _bf(w_out[l]))
        h = _mlp(h, norm_mlp[l].reshape(1, D_MODEL), _bf(w_up[l]), _bf(w_down[l]), g_final,
                 final_norm=(l == depth - 1))
    return h.reshape(batch, seq, D_MODEL)
```

```python
import functools

import jax
import jax.numpy as jnp
from jax import lax
from jax.experimental import pallas as pl
from jax.experimental.pallas import tpu as pltpu

F32 = jnp.float32
BF16 = jnp.bfloat16

D_MODEL = 1024
MIX_W = D_MODEL
DN_HEADS = 8
DN_HEAD_DIM = 128
SB_HEADS = 16
SB_HEAD_DIM = 64
SSM_HEADS = 16
SSM_HEAD_DIM = 64
SSM_STATE = 128
SSM_GROUPS = 4
CHUNK = 64
D_FF = 4 * D_MODEL
EPS = 1e-6
CONV_WIDTH = 4

LANES = 128
TBLK = 2 * CHUNK
HALO = 8

COL_DN_QKV = 0
COL_DN_GATE = 3072
COL_SB_QKV = 4096
COL_SSM_Z = 7168
COL_SSM_XBC = 8192
COL_GATES = 10240
N_BIG = 13312
LANE_DN_A = 0
LANE_DN_B = 8
LANE_SSM_DT = 16

VMEM_LIMIT = 56 * 1024 * 1024


def _cparams(*sem):
    return pltpu.CompilerParams(dimension_semantics=sem, vmem_limit_bytes=VMEM_LIMIT)


def _bf(x):
    return x.astype(BF16)


def _mm(a, b):
    return jnp.dot(_bf(a), _bf(b), preferred_element_type=F32)


def _mm_nt(a, b):
    return lax.dot_general(_bf(a), _bf(b), (((1,), (1,)), ((), ())), preferred_element_type=F32)


def _split2(a):
    hi = _bf(a)
    lo = _bf(a - hi.astype(F32))
    return hi, lo


def _split3(a):
    hi = _bf(a)
    r = a - hi.astype(F32)
    mid = _bf(r)
    lo = _bf(r - mid.astype(F32))
    return hi, mid, lo


def _mm_x3s(a_split, b_split):
    ah, al = a_split
    bh, bl = b_split
    lhs = jnp.concatenate([ah, al, ah], axis=1)
    rhs = jnp.concatenate([bh, bh, bl], axis=0)
    return jnp.dot(lhs, rhs, preferred_element_type=F32)


def _mm_sel_lhs(sel, b):
    s = _bf(sel)
    bh, bm, bl = _split3(b)
    return jnp.dot(jnp.concatenate([s, s, s], axis=1), jnp.concatenate([bh, bm, bl], axis=0),
                   preferred_element_type=F32)


def _mm_sel_rhs(a, sel_bf):
    ah, am, al = _split3(a)
    return jnp.dot(jnp.concatenate([ah, am, al], axis=1), jnp.concatenate([sel_bf, sel_bf, sel_bf], axis=0),
                   preferred_element_type=F32)


def _softplus(x):
    return jnp.maximum(x, 0.0) + jnp.log(1.0 + jnp.exp(-jnp.abs(x)))


def _sigmoid(x):
    return 1.0 / (1.0 + jnp.exp(-x))


def _silu(x):
    return x * _sigmoid(x)


def _rms(x, w):
    return x * lax.rsqrt(jnp.mean(x * x, axis=-1, keepdims=True) + EPS) * w


def _causal_conv(halo, cur, w, bias):
    rows = cur.shape[0]
    xx = jnp.concatenate([halo, cur], axis=0)
    y = w[CONV_WIDTH - 1:CONV_WIDTH, :] * cur
    for k in range(CONV_WIDTH - 1):
        shifted = pltpu.roll(xx, CONV_WIDTH - 1 - k, 0)[HALO:HALO + rows]
        y = y + w[k:k + 1, :] * shifted
    if bias is not None:
        y = y + bias
    return y


def _chunk_masks():
    row = lax.broadcasted_iota(jnp.int32, (TBLK, TBLK), 0)
    col = lax.broadcasted_iota(jnp.int32, (TBLK, TBLK), 1)
    same = (row >= CHUNK) == (col >= CHUNK)
    return row, col, same & (col <= row), same & (col < row)


def _inproj_kernel(x_ref, g_ref, wb_ref, ws_ref, big_ref, small_ref, xn_scr):
    @pl.when(pl.program_id(1) == 0)
    def _():
        xn_scr[...] = _bf(_rms(x_ref[...], g_ref[...]))
        small_ref[...] = jnp.dot(xn_scr[...], ws_ref[...], preferred_element_type=F32)

    big_ref[...] = _bf(jnp.dot(xn_scr[...], wb_ref[...], preferred_element_type=F32))


def _inproj(x, g, w_big, w_small, tm=1024, tn=1024):
    t = x.shape[0]
    tm = min(tm, t)
    return pl.pallas_call(
        _inproj_kernel,
        grid=(t // tm, N_BIG // tn),
        in_specs=[
            pl.BlockSpec((tm, D_MODEL), lambda i, j: (i, 0)),
            pl.BlockSpec((1, D_MODEL), lambda i, j: (0, 0)),
            pl.BlockSpec((D_MODEL, tn), lambda i, j: (0, j)),
            pl.BlockSpec((D_MODEL, LANES), lambda i, j: (0, 0)),
        ],
        out_specs=[
            pl.BlockSpec((tm, tn), lambda i, j: (i, j)),
            pl.BlockSpec((tm, LANES), lambda i, j: (i, 0)),
        ],
        out_shape=[jax.ShapeDtypeStruct((t, N_BIG), BF16), jax.ShapeDtypeStruct((t, LANES), F32)],
        scratch_shapes=[pltpu.VMEM((tm, D_MODEL), BF16)],
        compiler_params=_cparams("parallel", "arbitrary"),
        name="inproj",
    )(x, g, w_big, w_small)


def _dn_kernel(qkv_ref, gate_ref, sm_ref, cw_ref, prow_ref, nw_ref, o_ref, halo_scr, state_scr):
    @pl.when(pl.program_id(1) == 0)
    def _():
        halo_scr[...] = jnp.zeros_like(halo_scr)
        state_scr[...] = jnp.zeros_like(state_scr)

    row, col, m_incl, m_strict = _chunk_masks()
    tril = jnp.where(m_incl, 1.0, 0.0)
    eye = jnp.where(row == col, 1.0, 0.0)
    first = row[:, 0:1] < CHUNK

    sm = sm_ref[...]
    g_all = -jnp.exp(prow_ref[0:1, :]) * _softplus(sm + prow_ref[1:2, :])
    beta_all = _sigmoid(sm)
    gc_all = _mm_sel_lhs(tril, g_all)
    gct_all = gc_all.T

    heads = range(DN_HEADS)

    def conv_silu(seg, h):
        lanes = slice(seg * MIX_W + h * DN_HEAD_DIM, seg * MIX_W + (h + 1) * DN_HEAD_DIM)
        cur = qkv_ref[:, lanes].astype(F32)
        return _silu(_causal_conv(halo_scr[:, lanes], cur, cw_ref[:, lanes], None))

    q, k, v, kb, vb, decay, gc_col, kk, qk = {}, {}, {}, {}, {}, {}, {}, {}, {}
    for h in heads:
        qh = conv_silu(0, h)
        kh = conv_silu(1, h)
        v[h] = conv_silu(2, h)
        q[h] = qh * lax.rsqrt(jnp.sum(qh * qh, axis=-1, keepdims=True) + EPS) * (DN_HEAD_DIM ** -0.5)
        k[h] = kh * lax.rsqrt(jnp.sum(kh * kh, axis=-1, keepdims=True) + EPS)
        gc_col[h] = gc_all[:, LANE_DN_A + h:LANE_DN_A + h + 1]
        beta = beta_all[:, LANE_DN_B + h:LANE_DN_B + h + 1]
        gc_row = gct_all[LANE_DN_A + h:LANE_DN_A + h + 1, :]
        decay[h] = jnp.exp(jnp.where(m_incl, gc_col[h] - gc_row, -jnp.inf))
        kb[h] = k[h] * beta
        vb[h] = v[h] * beta
    for h in heads:
        kk[h] = _mm_nt(kb[h], k[h])
        qk[h] = _mm_nt(q[h], k[h])

    xs, tinv = {}, {}
    for h in heads:
        x = -jnp.where(m_strict, kk[h] * decay[h], 0.0)
        xs[h] = _split2(x)
        tinv[h] = eye + x
    for _ in range(5):
        for h in heads:
            xs[h] = _split2(_mm_x3s(xs[h], xs[h]))
        for h in heads:
            tinv[h] = tinv[h] + _mm_x3s(_split2(tinv[h]), xs[h])

    u, w, attn, qg, kdt, gl = {}, {}, {}, {}, {}, {}
    for h in heads:
        egc = jnp.exp(gc_col[h])
        ts = _split2(tinv[h])
        u[h] = _mm_x3s(ts, _split2(vb[h]))
        w[h] = _mm_x3s(ts, _split2(kb[h] * egc))
        attn[h] = _bf(jnp.where(m_incl, qk[h] * decay[h], 0.0))
        qg[h] = _bf(q[h] * egc)
        gl[h] = (gc_col[h][CHUNK - 1:CHUNK, :], gc_col[h][TBLK - 1:TBLK, :])
        kdt[h] = _bf((k[h] * jnp.exp(jnp.where(first, gl[h][0], gl[h][1]) - gc_col[h])).T)

    s = {h: state_scr[h] for h in heads}
    outs = {h: [] for h in heads}
    pad = jnp.zeros((CHUNK, DN_HEAD_DIM), F32)
    for c in range(2):
        r0, r1 = c * CHUNK, (c + 1) * CHUNK
        v_blk = {}
        for h in heads:
            v_new = u[h][r0:r1] - _mm(w[h][r0:r1], s[h])
            v_blk[h] = _bf(jnp.concatenate([v_new, pad] if c == 0 else [pad, v_new], axis=0))
        for h in heads:
            outs[h].append(_mm(qg[h][r0:r1], s[h]) + _mm(attn[h][r0:r1, :], v_blk[h]))
            s[h] = s[h] * jnp.exp(gl[h][c]) + _mm(kdt[h], v_blk[h])
    for h in heads:
        state_scr[h] = s[h]
        lanes = slice(h * DN_HEAD_DIM, (h + 1) * DN_HEAD_DIM)
        o = jnp.concatenate(outs[h], axis=0)
        o_ref[:, lanes] = _bf(_rms(o, nw_ref[...]) * _silu(gate_ref[:, lanes].astype(F32)))
    halo_scr[...] = qkv_ref[TBLK - 16:TBLK, :].astype(F32)[16 - HALO:16]


def _dn_mixer(big, small, conv_w, prow, norm_w, batch, seq):
    nd = seq // TBLK
    row_map = lambda b, d: (b * nd + d, 0)
    return pl.pallas_call(
        _dn_kernel,
        grid=(batch, nd),
        in_specs=[
            pl.BlockSpec((TBLK, 3 * MIX_W), lambda b, d: (b * nd + d, COL_DN_QKV // (3 * MIX_W))),
            pl.BlockSpec((TBLK, MIX_W), lambda b, d: (b * nd + d, COL_DN_GATE // MIX_W)),
            pl.BlockSpec((TBLK, LANES), row_map),
            pl.BlockSpec((CONV_WIDTH, 3 * MIX_W), lambda b, d: (0, 0)),
            pl.BlockSpec((8, LANES), lambda b, d: (0, 0)),
            pl.BlockSpec((1, DN_HEAD_DIM), lambda b, d: (0, 0)),
        ],
        out_specs=pl.BlockSpec((TBLK, MIX_W), row_map),
        out_shape=jax.ShapeDtypeStruct((batch * seq, MIX_W), BF16),
        scratch_shapes=[pltpu.VMEM((HALO, 3 * MIX_W), F32), pltpu.VMEM((DN_HEADS, DN_HEAD_DIM, DN_HEAD_DIM), F32)],
        compiler_params=_cparams("parallel", "arbitrary"),
        name="deltanet",
    )(big, big, small, conv_w, prow, norm_w)


SB_TILE = 128
SB_TQ = 256
SB_SUB = SB_TQ // SB_TILE
SB_STRIP = 128
SB_HEADS_PER_STEP = 4
LOG2E = 1.4426950408889634


def _sb_kernel(q_ref, k_ref, v_ref, w_ref, o_ref, r_scr, acc_scr, za_scr, zb_scr, wgt_scr):
    qi = pl.program_id(2)
    row = lax.broadcasted_iota(jnp.int32, (SB_STRIP, SB_TILE), 0)
    col = lax.broadcasted_iota(jnp.int32, (SB_STRIP, SB_TILE), 1)
    wmat = w_ref[...]
    r_scr[...] = jnp.zeros_like(r_scr)
    acc_scr[...] = jnp.zeros_like(acc_scr)
    strips = range(SB_TQ // SB_STRIP)
    heads = range(SB_HEADS_PER_STEP)
    pairs = range(SB_HEADS_PER_STEP // 2)
    order = list(reversed(range(SB_SUB)))
    qh = {}
    for s in strips:
        for h in heads:
            lanes = slice((h // 2) * LANES, (h // 2 + 1) * LANES)
            q = q_ref[s * SB_STRIP:(s + 1) * SB_STRIP, lanes].astype(F32) * (SB_HEAD_DIM ** -0.5)
            qh[s, h] = _bf(jnp.where((col >= SB_HEAD_DIM) == (h % 2 == 1), q, 0.0))

    def chain_id(t, s, h):
        return (t * len(strips) + s) * SB_HEADS_PER_STEP + h

    def masked_out(t, s, diagonal):
        return diagonal and t * SB_TILE >= (s + 1) * SB_STRIP

    def block_rows(kb_lo):
        return pl.ds(pl.multiple_of(kb_lo * SB_TILE, SB_TILE), SB_SUB * SB_TILE)

    def score_stage(kb_lo, z_dst, diagonal):
        kblk = [k_ref[block_rows(kb_lo), p * LANES:(p + 1) * LANES] for p in pairs]
        for t in order:
            for s in strips:
                for h in heads:
                    if not masked_out(t, s, diagonal):
                        z_dst[chain_id(t, s, h)] = _mm_nt(qh[s, h], kblk[h // 2][t * SB_TILE:(t + 1) * SB_TILE]) * LOG2E

    def cumsum_stage(z_src, diagonal):
        chains = [(t, s, h) for t in order for s in strips for h in heads if not masked_out(t, s, diagonal)]
        z, cs, tri = {}, {}, {}
        for c in chains:
            t, s, _ = c
            z[c] = z_src[chain_id(*c)]
            neg_abs = pltpu.bitcast(pltpu.bitcast(z[c], jnp.uint32) | jnp.uint32(0x80000000), F32)
            sp = jnp.maximum(z[c], 0.0) + jnp.log2(1.0 + jnp.exp2(neg_abs))
            if diagonal:
                tri[c] = col + t * SB_TILE < row + s * SB_STRIP
                sp = jnp.where(tri[c], sp, 0.0)
            hi, lo = _split2(sp)
            cs[c] = jnp.dot(jnp.concatenate([hi, lo], axis=1), wmat, preferred_element_type=F32)
        return z, cs, tri

    def weight_stage(z, cs, tri, diagonal):
        for s in strips:
            rows = slice(s * SB_STRIP, (s + 1) * SB_STRIP)
            for h in heads:
                r_sum = r_scr[h, rows]
                for t in order:
                    c = (t, s, h)
                    lanes = slice(t * SB_TILE, (t + 1) * SB_TILE)
                    if c not in cs:
                        wgt_scr[h, rows, lanes] = jnp.zeros((SB_STRIP, SB_TILE), BF16)
                        continue
                    wgt = jnp.exp2(z[c] - cs[c][:, :SB_TILE] - r_sum)
                    wgt_scr[h, rows, lanes] = _bf(jnp.where(tri[c], wgt, 0.0) if diagonal else wgt)
                    r_sum = r_sum + cs[c][:, SB_TILE:]
                r_scr[h, rows] = r_sum

    def value_stage(kb_lo):
        vblk = [v_ref[block_rows(kb_lo), p * LANES:(p + 1) * LANES] for p in pairs]
        for h in heads:
            acc_scr[h] += jnp.dot(wgt_scr[h], vblk[h // 2], preferred_element_type=F32)

    def step(cur, z_src, z_dst, issue_next=True):
        value_stage((cur + 1) * SB_SUB)
        if issue_next:
            score_stage(jnp.maximum(cur - 1, 0) * SB_SUB, z_dst, False)
        weight_stage(*cumsum_stage(z_src, False), False)

    score_stage(qi * SB_SUB, za_scr, True)
    score_stage(jnp.maximum(qi - 1, 0) * SB_SUB, zb_scr, False)
    weight_stage(*cumsum_stage(za_scr, True), True)

    def body(j, carry):
        step(qi - 1 - 2 * j, zb_scr, za_scr)
        step(qi - 2 - 2 * j, za_scr, zb_scr)
        return carry

    lax.fori_loop(0, qi // 2, body, 0)

    @pl.when(qi % 2 == 1)
    def _():
        step(0, zb_scr, za_scr, issue_next=False)

    value_stage(0)
    lane = lax.broadcasted_iota(jnp.int32, (SB_TQ, SB_TILE), 1)
    for p in range(SB_HEADS_PER_STEP // 2):
        o_ref[:, p * LANES:(p + 1) * LANES] = _bf(jnp.where(lane < SB_HEAD_DIM, acc_scr[2 * p], acc_scr[2 * p + 1]))


def _sb_mixer(big, wmat, batch, seq):
    nq = seq // SB_TQ
    width = SB_HEADS_PER_STEP * SB_HEAD_DIM
    groups = MIX_W // width
    qoff = COL_SB_QKV // width
    return pl.pallas_call(
        _sb_kernel,
        grid=(batch, groups, nq),
        in_specs=[
            pl.BlockSpec((SB_TQ, width), lambda b, p, i: (b * nq + i, qoff + p)),
            pl.BlockSpec((seq, width), lambda b, p, i: (b, qoff + groups + p)),
            pl.BlockSpec((seq, width), lambda b, p, i: (b, qoff + 2 * groups + p)),
            pl.BlockSpec((2 * SB_TILE, 2 * SB_TILE), lambda b, p, i: (0, 0)),
        ],
        out_specs=pl.BlockSpec((SB_TQ, width), lambda b, p, i: (b * nq + i, p)),
        out_shape=jax.ShapeDtypeStruct((batch * seq, MIX_W), BF16),
        scratch_shapes=[pltpu.VMEM((SB_HEADS_PER_STEP, SB_TQ, SB_TILE), F32),
                        pltpu.VMEM((SB_HEADS_PER_STEP, SB_TQ, SB_TILE), F32),
                        pltpu.VMEM((SB_SUB * (SB_TQ // SB_STRIP) * SB_HEADS_PER_STEP, SB_STRIP, SB_TILE), F32),
                        pltpu.VMEM((SB_SUB * (SB_TQ // SB_STRIP) * SB_HEADS_PER_STEP, SB_STRIP, SB_TILE), F32),
                        pltpu.VMEM((SB_HEADS_PER_STEP, SB_TQ, SB_SUB * SB_TILE), BF16)],
        compiler_params=_cparams("parallel", "parallel", "arbitrary"),
        name="stickbreak",
    )(big, big, big, wmat)


GROUP_W = MIX_W // SSM_GROUPS
HEADS_PER_GROUP = SSM_HEADS // SSM_GROUPS


def _ssd_kernel(z_ref, xbc_ref, sm_ref, cw_ref, cb_ref, prow_ref, drow_ref, nw_ref, e_ref, o_ref,
                halo_scr, state_scr):
    @pl.when(pl.program_id(1) == 0)
    def _():
        halo_scr[...] = jnp.zeros_like(halo_scr)
        state_scr[...] = jnp.zeros_like(state_scr)

    row, col, m_incl, _ = _chunk_masks()
    tril = jnp.where(m_incl, 1.0, 0.0)
    first = row[:, 0:1] < CHUNK
    gcol = lax.broadcasted_iota(jnp.int32, (TBLK, GROUP_W), 1)

    cur = xbc_ref[...].astype(F32)
    xbc = _silu(_causal_conv(halo_scr[...], cur, cw_ref[...], cb_ref[...]))
    halo_scr[...] = cur[TBLK - HALO:TBLK]
    gn = SSM_GROUPS * SSM_STATE
    xs = xbc[:, :MIX_W]
    bm = xbc[:, MIX_W:MIX_W + gn]
    cm = xbc[:, MIX_W + gn:]

    dt_all = _softplus(sm_ref[...] + prow_ref[1:2, :])
    a_all = -jnp.exp(prow_ref[0:1, :]) * dt_all
    acum_all = _mm_sel_lhs(tril, a_all)
    acum_t = acum_all.T
    e = e_ref[...]
    dt_exp = _mm_sel_rhs(dt_all, e)
    acum_exp = _mm_sel_rhs(acum_all, e)
    xdt = xs * dt_exp

    for g in range(SSM_GROUPS):
        c_g = cm[:, g * SSM_STATE:(g + 1) * SSM_STATE]
        b_g = bm[:, g * SSM_STATE:(g + 1) * SSM_STATE]
        scores = _mm_nt(c_g, b_g)
        lo, hi = g * GROUP_W, (g + 1) * GROUP_W
        x_g = xdt[:, lo:hi]
        a_g = acum_exp[:, lo:hi]
        y_g = jnp.zeros((TBLK, GROUP_W), F32)
        for hh in range(HEADS_PER_GROUP):
            h = g * HEADS_PER_GROUP + hh
            a_col = acum_exp[:, h * SSM_HEAD_DIM:h * SSM_HEAD_DIM + 1]
            a_row = acum_t[LANE_SSM_DT + h:LANE_SSM_DT + h + 1, :]
            lmat = jnp.exp(jnp.where(m_incl, a_col - a_row, -jnp.inf))
            y_g = jnp.where(gcol // SSM_HEAD_DIM == hh, _mm(scores * lmat, x_g), y_g)

        b_t = b_g.T
        a_last0 = a_g[CHUNK - 1:CHUNK, :]
        a_last1 = a_g[TBLK - 1:TBLK, :]
        x_sc = x_g * jnp.exp(jnp.where(first, a_last0, a_last1) - a_g)
        st = state_scr[g]
        yoff = []
        for c, a_last in ((0, a_last0), (1, a_last1)):
            r0, r1 = c * CHUNK, (c + 1) * CHUNK
            yoff.append(_mm(c_g[r0:r1], st) * jnp.exp(a_g[r0:r1]))
            st = st * jnp.exp(a_last) + _mm(b_t, jnp.where(first == (c == 0), x_sc, 0.0))
        state_scr[g] = st
        y_g = y_g + jnp.concatenate(yoff, axis=0) + xs[:, lo:hi] * drow_ref[:, lo:hi]
        y_g = y_g * _silu(z_ref[:, lo:hi].astype(F32))
        o_ref[:, lo:hi] = _bf(_rms(y_g, nw_ref[:, lo:hi]))


def _ssd_mixer(big, small, conv_w, conv_b, prow, drow, norm_w, emat, batch, seq):
    nd = seq // TBLK
    conv_dim = MIX_W + 2 * SSM_GROUPS * SSM_STATE
    row_map = lambda b, d: (b * nd + d, 0)
    const = lambda b, d: (0, 0)
    return pl.pallas_call(
        _ssd_kernel,
        grid=(batch, nd),
        in_specs=[
            pl.BlockSpec((TBLK, MIX_W), lambda b, d: (b * nd + d, COL_SSM_Z // MIX_W)),
            pl.BlockSpec((TBLK, conv_dim), lambda b, d: (b * nd + d, COL_SSM_XBC // conv_dim)),
            pl.BlockSpec((TBLK, LANES), row_map),
            pl.BlockSpec((CONV_WIDTH, conv_dim), const),
            pl.BlockSpec((1, conv_dim), const),
            pl.BlockSpec((8, LANES), const),
            pl.BlockSpec((1, MIX_W), const),
            pl.BlockSpec((1, MIX_W), const),
            pl.BlockSpec((LANES, MIX_W), const),
        ],
        out_specs=pl.BlockSpec((TBLK, MIX_W), row_map),
        out_shape=jax.ShapeDtypeStruct((batch * seq, MIX_W), BF16),
        scratch_shapes=[pltpu.VMEM((HALO, conv_dim), F32), pltpu.VMEM((SSM_GROUPS, SSM_STATE, GROUP_W), F32)],
        compiler_params=_cparams("parallel", "arbitrary"),
        name="ssd",
    )(big, big, small, conv_w, conv_b, prow, drow, norm_w, emat)


def _merge_kernel(x_ref, odn_ref, osb_ref, ossm_ref, g0_ref, g1_ref, g2_ref, wb_ref, wo_ref, o_ref):
    m = _sigmoid(g0_ref[...].astype(F32)) * jnp.dot(odn_ref[...], wb_ref[0], preferred_element_type=F32)
    m = m + _sigmoid(g1_ref[...].astype(F32)) * jnp.dot(osb_ref[...], wb_ref[1], preferred_element_type=F32)
    m = m + _sigmoid(g2_ref[...].astype(F32)) * jnp.dot(ossm_ref[...], wb_ref[2], preferred_element_type=F32)
    o_ref[...] = x_ref[...] + jnp.dot(_bf(m), wo_ref[...], preferred_element_type=F32)


def _merge(x, o_dn, o_sb, o_ssm, big, w_branch, w_out, tm=512):
    t = x.shape[0]
    tm = min(tm, t)
    rows = lambda i: (i, 0)
    gcol = COL_GATES // D_MODEL
    return pl.pallas_call(
        _merge_kernel,
        grid=(t // tm,),
        in_specs=[
            pl.BlockSpec((tm, D_MODEL), rows),
            pl.BlockSpec((tm, MIX_W), rows),
            pl.BlockSpec((tm, MIX_W), rows),
            pl.BlockSpec((tm, MIX_W), rows),
            pl.BlockSpec((tm, D_MODEL), lambda i: (i, gcol)),
            pl.BlockSpec((tm, D_MODEL), lambda i: (i, gcol + 1)),
            pl.BlockSpec((tm, D_MODEL), lambda i: (i, gcol + 2)),
            pl.BlockSpec((3, MIX_W, D_MODEL), lambda i: (0, 0, 0)),
            pl.BlockSpec((D_MODEL, D_MODEL), lambda i: (0, 0)),
        ],
        out_specs=pl.BlockSpec((tm, D_MODEL), rows),
        out_shape=jax.ShapeDtypeStruct((t, D_MODEL), F32),
        compiler_params=_cparams("parallel"),
        name="merge",
    )(x, o_dn, o_sb, o_ssm, big, big, big, w_branch, w_out)


def _mlp_kernel(x_ref, g_ref, wu_ref, wd_ref, gf_ref, o_ref, xn_scr, acc_scr, *, final_norm):
    j = pl.program_id(1)

    @pl.when(j == 0)
    def _():
        xn_scr[...] = _bf(_rms(x_ref[...], g_ref[...]))
        acc_scr[...] = jnp.zeros_like(acc_scr)

    h = jnp.dot(xn_scr[...], wu_ref[...], preferred_element_type=F32)
    h = jnp.square(jnp.maximum(h, 0.0))
    acc_scr[...] += jnp.dot(_bf(h), wd_ref[...], preferred_element_type=F32)

    @pl.when(j == pl.num_programs(1) - 1)
    def _():
        y = x_ref[...] + acc_scr[...]
        o_ref[...] = _rms(y, gf_ref[...]) if final_norm else y


def _mlp(x, g, w_up, w_down, g_final, final_norm, tm=1024, tf=1024):
    t = x.shape[0]
    tm = min(tm, t)
    return pl.pallas_call(
        functools.partial(_mlp_kernel, final_norm=final_norm),
        grid=(t // tm, D_FF // tf),
        in_specs=[
            pl.BlockSpec((tm, D_MODEL), lambda i, j: (i, 0)),
            pl.BlockSpec((1, D_MODEL), lambda i, j: (0, 0)),
            pl.BlockSpec((D_MODEL, tf), lambda i, j: (0, j)),
            pl.BlockSpec((tf, D_MODEL), lambda i, j: (j, 0)),
            pl.BlockSpec((1, D_MODEL), lambda i, j: (0, 0)),
        ],
        out_specs=pl.BlockSpec((tm, D_MODEL), lambda i, j: (i, 0)),
        out_shape=jax.ShapeDtypeStruct((t, D_MODEL), F32),
        scratch_shapes=[pltpu.VMEM((tm, D_MODEL), BF16), pltpu.VMEM((tm, D_MODEL), F32)],
        compiler_params=_cparams("parallel", "arbitrary"),
        name="mlp",
    )(x, g, w_up, w_down, g_final)


def _lane_row(values, lane0):
    return jnp.zeros((LANES,), F32).at[lane0:lane0 + values.shape[0]].set(values.astype(F32))


def _param_rows(a_log, dt_bias, lane0):
    rows = jnp.zeros((8, LANES), F32)
    return rows.at[0].set(_lane_row(a_log, lane0)).at[1].set(_lane_row(dt_bias, lane0))


def _sb_cumsum_matrix():
    j = jnp.arange(SB_TILE)[:, None]
    s = jnp.arange(SB_TILE)[None, :]
    half = jnp.concatenate([(j >= s).astype(BF16), jnp.ones((SB_TILE, SB_TILE), BF16)], axis=1)
    return jnp.concatenate([half, half], axis=0)


def _ssm_expand_matrix():
    lane = jnp.arange(LANES)[:, None]
    ch = jnp.arange(MIX_W)[None, :]
    return (lane == LANE_SSM_DT + ch // SSM_HEAD_DIM).astype(BF16)


def _split_w_in(w):
    big = jnp.concatenate([w[:, 0:4096], w[:, 4112:10256], w[:, 10272:13344]], axis=1)
    small = jnp.concatenate([w[:, 4096:4112], w[:, 10256:10272], jnp.zeros((D_MODEL, LANES - 32), w.dtype)], axis=1)
    return _bf(big), _bf(small)


def kernel(x, norm_mix, w_in, dn_conv_w, dn_a_log, dn_dt_bias, dn_norm_w, ssm_conv_w, ssm_conv_b, ssm_a_log,
           ssm_dt_bias, ssm_d, ssm_norm_w, w_branch, w_out, norm_mlp, w_up, w_down, norm_final):
    batch, seq, _ = x.shape
    depth = w_in.shape[0]
    h = x.reshape(batch * seq, D_MODEL)
    sb_w = _sb_cumsum_matrix()
    emat = _ssm_expand_matrix()
    g_final = norm_final.reshape(1, D_MODEL)
    for l in range(depth):
        w_big, w_small = _split_w_in(w_in[l])
        big, small = _inproj(h, norm_mix[l].reshape(1, D_MODEL), w_big, w_small)
        o_dn = _dn_mixer(big, small, dn_conv_w[l], _param_rows(dn_a_log[l], dn_dt_bias[l], LANE_DN_A),
                         dn_norm_w[l].reshape(1, DN_HEAD_DIM), batch, seq)
        o_sb = _sb_mixer(big, sb_w, batch, seq)
        o_ssm = _ssd_mixer(big, small, ssm_conv_w[l], ssm_conv_b[l].reshape(1, -1),
                           _param_rows(ssm_a_log[l], ssm_dt_bias[l], LANE_SSM_DT),
                           jnp.repeat(ssm_d[l], SSM_HEAD_DIM).reshape(1, MIX_W),
                           ssm_norm_w[l].reshape(1, MIX_W), emat, batch, seq)
        h = _merge(h, o_dn, o_sb, o_ssm, big, _bf(w_branch[l]), _bf(w_out[l]))
        h = _mlp(h, norm_mlp[l].reshape(1, D_MODEL), _bf(w_up[l]), _bf(w_down[l]), g_final,
                 final_norm=(l == depth - 1))
    return h.reshape(batch, seq, D_MODEL)
```

```python
import functools

import jax
import jax.numpy as jnp
from jax import lax
from jax.experimental import pallas as pl
from jax.experimental.pallas import tpu as pltpu

F32 = jnp.float32
BF16 = jnp.bfloat16

D_MODEL = 1024
MIX_W = D_MODEL
DN_HEADS = 8
DN_HEAD_DIM = 128
SB_HEADS = 16
SB_HEAD_DIM = 64
SSM_HEADS = 16
SSM_HEAD_DIM = 64
SSM_STATE = 128
SSM_GROUPS = 4
CHUNK = 64
D_FF = 4 * D_MODEL
EPS = 1e-6
CONV_WIDTH = 4

LANES = 128
TBLK = 2 * CHUNK
HALO = 8
DN_BASE_BLOCK = 8

COL_DN_QKV = 0
COL_DN_GATE = 3072
COL_SB_QKV = 4096
COL_SSM_Z = 7168
COL_SSM_XBC = 8192
COL_GATES = 10240
N_BIG = 13312
LANE_DN_A = 0
LANE_DN_B = 8
LANE_SSM_DT = 16

VMEM_LIMIT = 56 * 1024 * 1024


def _cparams(*sem):
    return pltpu.CompilerParams(dimension_semantics=sem, vmem_limit_bytes=VMEM_LIMIT)


def _bf(x):
    return x.astype(BF16)


def _mm(a, b):
    return jnp.dot(_bf(a), _bf(b), preferred_element_type=F32)


def _mm_nt(a, b):
    return lax.dot_general(_bf(a), _bf(b), (((1,), (1,)), ((), ())), preferred_element_type=F32)


def _split2(a):
    hi = _bf(a)
    lo = _bf(a - hi.astype(F32))
    return hi, lo


def _split3(a):
    hi = _bf(a)
    r = a - hi.astype(F32)
    mid = _bf(r)
    lo = _bf(r - mid.astype(F32))
    return hi, mid, lo


def _mm_x3s(a_split, b_split):
    ah, al = a_split
    bh, bl = b_split
    lhs = jnp.concatenate([ah, al, ah], axis=1)
    rhs = jnp.concatenate([bh, bh, bl], axis=0)
    return jnp.dot(lhs, rhs, preferred_element_type=F32)


def _mm_sel_lhs(sel, b):
    s = _bf(sel)
    bh, bm, bl = _split3(b)
    return jnp.dot(jnp.concatenate([s, s, s], axis=1), jnp.concatenate([bh, bm, bl], axis=0),
                   preferred_element_type=F32)


def _mm_sel_rhs(a, sel_bf):
    ah, am, al = _split3(a)
    return jnp.dot(jnp.concatenate([ah, am, al], axis=1), jnp.concatenate([sel_bf, sel_bf, sel_bf], axis=0),
                   preferred_element_type=F32)


def _softplus(x):
    return jnp.maximum(x, 0.0) + jnp.log(1.0 + jnp.exp(-jnp.abs(x)))


def _sigmoid(x):
    return 1.0 / (1.0 + jnp.exp(-x))


def _silu(x):
    return x * _sigmoid(x)


def _rms(x, w):
    return x * lax.rsqrt(jnp.mean(x * x, axis=-1, keepdims=True) + EPS) * w


def _causal_conv(halo, cur, w, bias):
    rows = cur.shape[0]
    xx = jnp.concatenate([halo, cur], axis=0)
    y = w[CONV_WIDTH - 1:CONV_WIDTH, :] * cur
    for k in range(CONV_WIDTH - 1):
        shifted = pltpu.roll(xx, CONV_WIDTH - 1 - k, 0)[HALO:HALO + rows]
        y = y + w[k:k + 1, :] * shifted
    if bias is not None:
        y = y + bias
    return y


def _chunk_masks():
    row = lax.broadcasted_iota(jnp.int32, (TBLK, TBLK), 0)
    col = lax.broadcasted_iota(jnp.int32, (TBLK, TBLK), 1)
    same = (row >= CHUNK) == (col >= CHUNK)
    return row, col, same & (col <= row), same & (col < row)


def _inproj_kernel(x_ref, g_ref, wb_ref, ws_ref, big_ref, small_ref, xn_scr):
    @pl.when(pl.program_id(1) == 0)
    def _():
        xn_scr[...] = _bf(_rms(x_ref[...], g_ref[...]))
        small_ref[...] = jnp.dot(xn_scr[...], ws_ref[...], preferred_element_type=F32)

    big_ref[...] = _bf(jnp.dot(xn_scr[...], wb_ref[...], preferred_element_type=F32))


def _inproj(x, g, w_big, w_small, layer, tm=2048, tn=1024):
    t = x.shape[0]
    tm = min(tm, t)
    return pl.pallas_call(
        _inproj_kernel,
        grid=(t // tm, N_BIG // tn),
        in_specs=[
            pl.BlockSpec((tm, D_MODEL), lambda i, j: (i, 0)),
            pl.BlockSpec((1, D_MODEL), lambda i, j: (0, 0)),
            pl.BlockSpec((None, D_MODEL, tn), lambda i, j: (layer, 0, j)),
            pl.BlockSpec((None, D_MODEL, LANES), lambda i, j: (layer, 0, 0)),
        ],
        out_specs=[
            pl.BlockSpec((tm, tn), lambda i, j: (i, j)),
            pl.BlockSpec((tm, LANES), lambda i, j: (i, 0)),
        ],
        out_shape=[jax.ShapeDtypeStruct((t, N_BIG), BF16), jax.ShapeDtypeStruct((t, LANES), F32)],
        scratch_shapes=[pltpu.VMEM((tm, D_MODEL), BF16)],
        compiler_params=_cparams("parallel", "arbitrary"),
        name="inproj",
    )(x, g, w_big, w_small)


def _dn_kernel(qkv_ref, gate_ref, sm_ref, cw_ref, prow_ref, nw_ref, o_ref, halo_scr, state_scr):
    @pl.when(pl.program_id(1) == 0)
    def _():
        halo_scr[...] = jnp.zeros_like(halo_scr)
        state_scr[...] = jnp.zeros_like(state_scr)

    row, col, m_incl, m_strict = _chunk_masks()
    tril = jnp.where(m_incl, 1.0, 0.0)
    eye = jnp.where(row == col, 1.0, 0.0)
    first = row[:, 0:1] < CHUNK

    sm = sm_ref[...]
    g_all = -jnp.exp(prow_ref[0:1, :]) * _softplus(sm + prow_ref[1:2, :])
    beta_all = _sigmoid(sm)
    gc_all = _mm_sel_lhs(tril, g_all)
    gct_all = gc_all.T

    heads = range(DN_HEADS)

    def conv_silu(seg, h):
        lanes = slice(seg * MIX_W + h * DN_HEAD_DIM, seg * MIX_W + (h + 1) * DN_HEAD_DIM)
        cur = qkv_ref[:, lanes].astype(F32)
        return _silu(_causal_conv(halo_scr[:, lanes], cur, cw_ref[:, lanes], None))

    q, k, v, kb, vb, decay, gc_col, kk, qk = {}, {}, {}, {}, {}, {}, {}, {}, {}
    for h in heads:
        qh = conv_silu(0, h)
        kh = conv_silu(1, h)
        v[h] = conv_silu(2, h)
        q[h] = qh * lax.rsqrt(jnp.sum(qh * qh, axis=-1, keepdims=True) + EPS) * (DN_HEAD_DIM ** -0.5)
        k[h] = kh * lax.rsqrt(jnp.sum(kh * kh, axis=-1, keepdims=True) + EPS)
        gc_col[h] = gc_all[:, LANE_DN_A + h:LANE_DN_A + h + 1]
        beta = beta_all[:, LANE_DN_B + h:LANE_DN_B + h + 1]
        gc_row = gct_all[LANE_DN_A + h:LANE_DN_A + h + 1, :]
        decay[h] = jnp.exp(jnp.where(m_incl, gc_col[h] - gc_row, -jnp.inf))
        kb[h] = k[h] * beta
        vb[h] = v[h] * beta
    for h in heads:
        kk[h] = _mm_nt(kb[h], k[h])
        qk[h] = _mm_nt(q[h], k[h])

    def same_block(size):
        shift = size.bit_length() - 1
        return (row >> shift) == (col >> shift)

    low, xs, tinv = {}, {}, {}
    for h in heads:
        low[h] = jnp.where(m_strict, kk[h] * decay[h], 0.0)
        x = -jnp.where(same_block(DN_BASE_BLOCK), low[h], 0.0)
        xs[h] = _split2(x)
        tinv[h] = eye + x
    for _ in range(2):
        for h in heads:
            xs[h] = _split2(_mm_x3s(xs[h], xs[h]))
        for h in heads:
            tinv[h] = tinv[h] + _mm_x3s(_split2(tinv[h]), xs[h])
    size = DN_BASE_BLOCK
    while size < CHUNK:
        off_diag = same_block(2 * size) & jnp.logical_not(same_block(size))
        ts, cd = {}, {}
        for h in heads:
            ts[h] = _split2(tinv[h])
            cd[h] = _mm_x3s(_split2(jnp.where(off_diag, low[h], 0.0)), ts[h])
        for h in heads:
            tinv[h] = tinv[h] - _mm_x3s(ts[h], _split2(cd[h]))
        size *= 2

    u, w, attn, qg, kdt, gl = {}, {}, {}, {}, {}, {}
    for h in heads:
        egc = jnp.exp(gc_col[h])
        ts = _split2(tinv[h])
        u[h] = _mm_x3s(ts, _split2(vb[h]))
        w[h] = _mm_x3s(ts, _split2(kb[h] * egc))
        attn[h] = _bf(jnp.where(m_incl, qk[h] * decay[h], 0.0))
        qg[h] = _bf(q[h] * egc)
        gl[h] = (gc_col[h][CHUNK - 1:CHUNK, :], gc_col[h][TBLK - 1:TBLK, :])
        kdt[h] = _bf((k[h] * jnp.exp(jnp.where(first, gl[h][0], gl[h][1]) - gc_col[h])).T)

    s = {h: state_scr[h] for h in heads}
    outs = {h: [] for h in heads}
    pad = jnp.zeros((CHUNK, DN_HEAD_DIM), F32)
    for c in range(2):
        r0, r1 = c * CHUNK, (c + 1) * CHUNK
        v_blk = {}
        for h in heads:
            v_new = u[h][r0:r1] - _mm(w[h][r0:r1], s[h])
            v_blk[h] = _bf(jnp.concatenate([v_new, pad] if c == 0 else [pad, v_new], axis=0))
        for h in heads:
            outs[h].append(_mm(qg[h][r0:r1], s[h]) + _mm(attn[h][r0:r1, :], v_blk[h]))
            s[h] = s[h] * jnp.exp(gl[h][c]) + _mm(kdt[h], v_blk[h])
    for h in heads:
        state_scr[h] = s[h]
        lanes = slice(h * DN_HEAD_DIM, (h + 1) * DN_HEAD_DIM)
        o = jnp.concatenate(outs[h], axis=0)
        o_ref[:, lanes] = _bf(_rms(o, nw_ref[...]) * _silu(gate_ref[:, lanes].astype(F32)))
    halo_scr[...] = qkv_ref[TBLK - 16:TBLK, :].astype(F32)[16 - HALO:16]


def _dn_mixer(big, small, conv_w, prow, norm_w, batch, seq):
    nd = seq // TBLK
    row_map = lambda b, d: (b * nd + d, 0)
    return pl.pallas_call(
        _dn_kernel,
        grid=(batch, nd),
        in_specs=[
            pl.BlockSpec((TBLK, 3 * MIX_W), lambda b, d: (b * nd + d, COL_DN_QKV // (3 * MIX_W))),
            pl.BlockSpec((TBLK, MIX_W), lambda b, d: (b * nd + d, COL_DN_GATE // MIX_W)),
            pl.BlockSpec((TBLK, LANES), row_map),
            pl.BlockSpec((CONV_WIDTH, 3 * MIX_W), lambda b, d: (0, 0)),
            pl.BlockSpec((8, LANES), lambda b, d: (0, 0)),
            pl.BlockSpec((1, DN_HEAD_DIM), lambda b, d: (0, 0)),
        ],
        out_specs=pl.BlockSpec((TBLK, MIX_W), row_map),
        out_shape=jax.ShapeDtypeStruct((batch * seq, MIX_W), BF16),
        scratch_shapes=[pltpu.VMEM((HALO, 3 * MIX_W), F32), pltpu.VMEM((DN_HEADS, DN_HEAD_DIM, DN_HEAD_DIM), F32)],
        compiler_params=_cparams("parallel", "arbitrary"),
        name="deltanet",
    )(big, big, small, conv_w, prow, norm_w)


SB_TILE = 128
SB_TQ = 256
SB_SUB = SB_TQ // SB_TILE
SB_STRIP = 128
SB_HEADS_PER_STEP = 4
LOG2E = 1.4426950408889634


def _sb_kernel(q_ref, k_ref, v_ref, w_ref, o_ref, r_scr, acc_scr, za_scr, zb_scr, wgt_scr):
    qi = pl.program_id(2)
    row = lax.broadcasted_iota(jnp.int32, (SB_STRIP, SB_TILE), 0)
    col = lax.broadcasted_iota(jnp.int32, (SB_STRIP, SB_TILE), 1)
    wmat = w_ref[...]
    r_scr[...] = jnp.zeros_like(r_scr)
    acc_scr[...] = jnp.zeros_like(acc_scr)
    strips = range(SB_TQ // SB_STRIP)
    heads = range(SB_HEADS_PER_STEP)
    pairs = range(SB_HEADS_PER_STEP // 2)
    order = list(reversed(range(SB_SUB)))
    qh = {}
    for s in strips:
        for h in heads:
            lanes = slice((h // 2) * LANES, (h // 2 + 1) * LANES)
            q = q_ref[s * SB_STRIP:(s + 1) * SB_STRIP, lanes].astype(F32) * (SB_HEAD_DIM ** -0.5)
            qh[s, h] = _bf(jnp.where((col >= SB_HEAD_DIM) == (h % 2 == 1), q, 0.0))

    def chain_id(t, s, h):
        return (t * len(strips) + s) * SB_HEADS_PER_STEP + h

    def masked_out(t, s, diagonal):
        return diagonal and t * SB_TILE >= (s + 1) * SB_STRIP

    def block_rows(kb_lo):
        return pl.ds(pl.multiple_of(kb_lo * SB_TILE, SB_TILE), SB_SUB * SB_TILE)

    def score_stage(kb_lo, z_dst, diagonal):
        kblk = [k_ref[block_rows(kb_lo), p * LANES:(p + 1) * LANES] for p in pairs]
        for t in order:
            for s in strips:
                for h in heads:
                    if not masked_out(t, s, diagonal):
                        z_dst[chain_id(t, s, h)] = _mm_nt(qh[s, h], kblk[h // 2][t * SB_TILE:(t + 1) * SB_TILE]) * LOG2E

    def cumsum_stage(z_src, diagonal):
        chains = [(t, s, h) for t in order for s in strips for h in heads if not masked_out(t, s, diagonal)]
        z, cs, tri = {}, {}, {}
        for c in chains:
            t, s, _ = c
            z[c] = z_src[chain_id(*c)]
            neg_abs = pltpu.bitcast(pltpu.bitcast(z[c], jnp.uint32) | jnp.uint32(0x80000000), F32)
            sp = jnp.maximum(z[c], 0.0) + jnp.log2(1.0 + jnp.exp2(neg_abs))
            if diagonal:
                tri[c] = col + t * SB_TILE < row + s * SB_STRIP
                sp = jnp.where(tri[c], sp, 0.0)
            hi, lo = _split2(sp)
            cs[c] = jnp.dot(jnp.concatenate([hi, lo], axis=1), wmat, preferred_element_type=F32)
        return z, cs, tri

    def weight_stage(z, cs, tri, diagonal):
        for s in strips:
            rows = slice(s * SB_STRIP, (s + 1) * SB_STRIP)
            for h in heads:
                r_sum = r_scr[h, rows]
                for t in order:
                    c = (t, s, h)
                    lanes = slice(t * SB_TILE, (t + 1) * SB_TILE)
                    if c not in cs:
                        wgt_scr[h, rows, lanes] = jnp.zeros((SB_STRIP, SB_TILE), BF16)
                        continue
                    wgt = jnp.exp2(z[c] - cs[c][:, :SB_TILE] - r_sum)
                    wgt_scr[h, rows, lanes] = _bf(jnp.where(tri[c], wgt, 0.0) if diagonal else wgt)
                    r_sum = r_sum + cs[c][:, SB_TILE:]
                r_scr[h, rows] = r_sum

    def value_stage(kb_lo):
        vblk = [v_ref[block_rows(kb_lo), p * LANES:(p + 1) * LANES] for p in pairs]
        for h in heads:
            acc_scr[h] += jnp.dot(wgt_scr[h], vblk[h // 2], preferred_element_type=F32)

    def step(cur, z_src, z_dst, issue_next=True):
        value_stage((cur + 1) * SB_SUB)
        if issue_next:
            score_stage(jnp.maximum(cur - 1, 0) * SB_SUB, z_dst, False)
        weight_stage(*cumsum_stage(z_src, False), False)

    score_stage(qi * SB_SUB, za_scr, True)
    score_stage(jnp.maximum(qi - 1, 0) * SB_SUB, zb_scr, False)
    weight_stage(*cumsum_stage(za_scr, True), True)

    def body(j, carry):
        step(qi - 1 - 2 * j, zb_scr, za_scr)
        step(qi - 2 - 2 * j, za_scr, zb_scr)
        return carry

    lax.fori_loop(0, qi // 2, body, 0)

    @pl.when(qi % 2 == 1)
    def _():
        step(0, zb_scr, za_scr, issue_next=False)

    value_stage(0)
    lane = lax.broadcasted_iota(jnp.int32, (SB_TQ, SB_TILE), 1)
    for p in range(SB_HEADS_PER_STEP // 2):
        o_ref[:, p * LANES:(p + 1) * LANES] = _bf(jnp.where(lane < SB_HEAD_DIM, acc_scr[2 * p], acc_scr[2 * p + 1]))


def _sb_mixer(big, wmat, batch, seq):
    nq = seq // SB_TQ
    width = SB_HEADS_PER_STEP * SB_HEAD_DIM
    groups = MIX_W // width
    qoff = COL_SB_QKV // width
    return pl.pallas_call(
        _sb_kernel,
        grid=(batch, groups, nq),
        in_specs=[
            pl.BlockSpec((SB_TQ, width), lambda b, p, i: (b * nq + i, qoff + p)),
            pl.BlockSpec((seq, width), lambda b, p, i: (b, qoff + groups + p)),
            pl.BlockSpec((seq, width), lambda b, p, i: (b, qoff + 2 * groups + p)),
            pl.BlockSpec((2 * SB_TILE, 2 * SB_TILE), lambda b, p, i: (0, 0)),
        ],
        out_specs=pl.BlockSpec((SB_TQ, width), lambda b, p, i: (b * nq + i, p)),
        out_shape=jax.ShapeDtypeStruct((batch * seq, MIX_W), BF16),
        scratch_shapes=[pltpu.VMEM((SB_HEADS_PER_STEP, SB_TQ, SB_TILE), F32),
                        pltpu.VMEM((SB_HEADS_PER_STEP, SB_TQ, SB_TILE), F32),
                        pltpu.VMEM((SB_SUB * (SB_TQ // SB_STRIP) * SB_HEADS_PER_STEP, SB_STRIP, SB_TILE), F32),
                        pltpu.VMEM((SB_SUB * (SB_TQ // SB_STRIP) * SB_HEADS_PER_STEP, SB_STRIP, SB_TILE), F32),
                        pltpu.VMEM((SB_HEADS_PER_STEP, SB_TQ, SB_SUB * SB_TILE), BF16)],
        compiler_params=_cparams("parallel", "parallel", "arbitrary"),
        name="stickbreak",
    )(big, big, big, wmat)


GROUP_W = MIX_W // SSM_GROUPS
HEADS_PER_GROUP = SSM_HEADS // SSM_GROUPS


def _ssd_kernel(z_ref, xbc_ref, sm_ref, cw_ref, cb_ref, prow_ref, drow_ref, nw_ref, e_ref, o_ref,
                halo_scr, state_scr):
    @pl.when(pl.program_id(1) == 0)
    def _():
        halo_scr[...] = jnp.zeros_like(halo_scr)
        state_scr[...] = jnp.zeros_like(state_scr)

    row, col, m_incl, _ = _chunk_masks()
    tril = jnp.where(m_incl, 1.0, 0.0)
    first = row[:, 0:1] < CHUNK
    gcol = lax.broadcasted_iota(jnp.int32, (TBLK, GROUP_W), 1)

    cur = xbc_ref[...].astype(F32)
    xbc = _silu(_causal_conv(halo_scr[...], cur, cw_ref[...], cb_ref[...]))
    halo_scr[...] = cur[TBLK - HALO:TBLK]
    gn = SSM_GROUPS * SSM_STATE
    xs = xbc[:, :MIX_W]
    bm = xbc[:, MIX_W:MIX_W + gn]
    cm = xbc[:, MIX_W + gn:]

    dt_all = _softplus(sm_ref[...] + prow_ref[1:2, :])
    a_all = -jnp.exp(prow_ref[0:1, :]) * dt_all
    acum_all = _mm_sel_lhs(tril, a_all)
    acum_t = acum_all.T
    e = e_ref[...]
    dt_exp = _mm_sel_rhs(dt_all, e)
    acum_exp = _mm_sel_rhs(acum_all, e)
    xdt = xs * dt_exp

    for g in range(SSM_GROUPS):
        c_g = cm[:, g * SSM_STATE:(g + 1) * SSM_STATE]
        b_g = bm[:, g * SSM_STATE:(g + 1) * SSM_STATE]
        scores = _mm_nt(c_g, b_g)
        lo, hi = g * GROUP_W, (g + 1) * GROUP_W
        x_g = xdt[:, lo:hi]
        a_g = acum_exp[:, lo:hi]
        y_g = jnp.zeros((TBLK, GROUP_W), F32)
        for hh in range(HEADS_PER_GROUP):
            h = g * HEADS_PER_GROUP + hh
            a_col = acum_exp[:, h * SSM_HEAD_DIM:h * SSM_HEAD_DIM + 1]
            a_row = acum_t[LANE_SSM_DT + h:LANE_SSM_DT + h + 1, :]
            lmat = jnp.exp(jnp.where(m_incl, a_col - a_row, -jnp.inf))
            y_g = jnp.where(gcol // SSM_HEAD_DIM == hh, _mm(scores * lmat, x_g), y_g)

        b_t = b_g.T
        a_last0 = a_g[CHUNK - 1:CHUNK, :]
        a_last1 = a_g[TBLK - 1:TBLK, :]
        x_sc = x_g * jnp.exp(jnp.where(first, a_last0, a_last1) - a_g)
        st = state_scr[g]
        yoff = []
        for c, a_last in ((0, a_last0), (1, a_last1)):
            r0, r1 = c * CHUNK, (c + 1) * CHUNK
            yoff.append(_mm(c_g[r0:r1], st) * jnp.exp(a_g[r0:r1]))
            st = st * jnp.exp(a_last) + _mm(b_t, jnp.where(first == (c == 0), x_sc, 0.0))
        state_scr[g] = st
        y_g = y_g + jnp.concatenate(yoff, axis=0) + xs[:, lo:hi] * drow_ref[:, lo:hi]
        y_g = y_g * _silu(z_ref[:, lo:hi].astype(F32))
        o_ref[:, lo:hi] = _bf(_rms(y_g, nw_ref[:, lo:hi]))


def _ssd_mixer(big, small, conv_w, conv_b, prow, drow, norm_w, emat, batch, seq):
    nd = seq // TBLK
    conv_dim = MIX_W + 2 * SSM_GROUPS * SSM_STATE
    row_map = lambda b, d: (b * nd + d, 0)
    const = lambda b, d: (0, 0)
    return pl.pallas_call(
        _ssd_kernel,
        grid=(batch, nd),
        in_specs=[
            pl.BlockSpec((TBLK, MIX_W), lambda b, d: (b * nd + d, COL_SSM_Z // MIX_W)),
            pl.BlockSpec((TBLK, conv_dim), lambda b, d: (b * nd + d, COL_SSM_XBC // conv_dim)),
            pl.BlockSpec((TBLK, LANES), row_map),
            pl.BlockSpec((CONV_WIDTH, conv_dim), const),
            pl.BlockSpec((1, conv_dim), const),
            pl.BlockSpec((8, LANES), const),
            pl.BlockSpec((1, MIX_W), const),
            pl.BlockSpec((1, MIX_W), const),
            pl.BlockSpec((LANES, MIX_W), const),
        ],
        out_specs=pl.BlockSpec((TBLK, MIX_W), row_map),
        out_shape=jax.ShapeDtypeStruct((batch * seq, MIX_W), BF16),
        scratch_shapes=[pltpu.VMEM((HALO, conv_dim), F32), pltpu.VMEM((SSM_GROUPS, SSM_STATE, GROUP_W), F32)],
        compiler_params=_cparams("parallel", "arbitrary"),
        name="ssd",
    )(big, big, small, conv_w, conv_b, prow, drow, norm_w, emat)


def _merge_kernel(x_ref, odn_ref, osb_ref, ossm_ref, g0_ref, g1_ref, g2_ref, wb_ref, wo_ref, o_ref):
    m = _sigmoid(g0_ref[...].astype(F32)) * jnp.dot(odn_ref[...], wb_ref[0], preferred_element_type=F32)
    m = m + _sigmoid(g1_ref[...].astype(F32)) * jnp.dot(osb_ref[...], wb_ref[1], preferred_element_type=F32)
    m = m + _sigmoid(g2_ref[...].astype(F32)) * jnp.dot(ossm_ref[...], wb_ref[2], preferred_element_type=F32)
    o_ref[...] = x_ref[...] + jnp.dot(_bf(m), wo_ref[...], preferred_element_type=F32)


def _merge(x, o_dn, o_sb, o_ssm, big, w_branch, w_out, layer, tm=512):
    t = x.shape[0]
    tm = min(tm, t)
    rows = lambda i: (i, 0)
    gcol = COL_GATES // D_MODEL
    return pl.pallas_call(
        _merge_kernel,
        grid=(t // tm,),
        in_specs=[
            pl.BlockSpec((tm, D_MODEL), rows),
            pl.BlockSpec((tm, MIX_W), rows),
            pl.BlockSpec((tm, MIX_W), rows),
            pl.BlockSpec((tm, MIX_W), rows),
            pl.BlockSpec((tm, D_MODEL), lambda i: (i, gcol)),
            pl.BlockSpec((tm, D_MODEL), lambda i: (i, gcol + 1)),
            pl.BlockSpec((tm, D_MODEL), lambda i: (i, gcol + 2)),
            pl.BlockSpec((None, 3, MIX_W, D_MODEL), lambda i: (layer, 0, 0, 0)),
            pl.BlockSpec((None, D_MODEL, D_MODEL), lambda i: (layer, 0, 0)),
        ],
        out_specs=pl.BlockSpec((tm, D_MODEL), rows),
        out_shape=jax.ShapeDtypeStruct((t, D_MODEL), F32),
        compiler_params=_cparams("parallel"),
        name="merge",
    )(x, o_dn, o_sb, o_ssm, big, big, big, w_branch, w_out)


def _mlp_kernel(x_ref, g_ref, wu_ref, wd_ref, gf_ref, o_ref, xn_scr, acc_scr, *, final_norm):
    j = pl.program_id(1)

    @pl.when(j == 0)
    def _():
        xn_scr[...] = _bf(_rms(x_ref[...], g_ref[...]))
        acc_scr[...] = jnp.zeros_like(acc_scr)

    h = jnp.dot(xn_scr[...], wu_ref[...], preferred_element_type=F32)
    h = jnp.square(jnp.maximum(h, 0.0))
    acc_scr[...] += jnp.dot(_bf(h), wd_ref[...], preferred_element_type=F32)

    @pl.when(j == pl.num_programs(1) - 1)
    def _():
        y = x_ref[...] + acc_scr[...]
        o_ref[...] = _rms(y, gf_ref[...]) if final_norm else y


def _mlp(x, g, w_up, w_down, g_final, layer, final_norm, tm=1024, tf=1024):
    t = x.shape[0]
    tm = min(tm, t)
    return pl.pallas_call(
        functools.partial(_mlp_kernel, final_norm=final_norm),
        grid=(t // tm, D_FF // tf),
        in_specs=[
            pl.BlockSpec((tm, D_MODEL), lambda i, j: (i, 0)),
            pl.BlockSpec((1, D_MODEL), lambda i, j: (0, 0)),
            pl.BlockSpec((None, D_MODEL, tf), lambda i, j: (layer, 0, j)),
            pl.BlockSpec((None, tf, D_MODEL), lambda i, j: (layer, j, 0)),
            pl.BlockSpec((1, D_MODEL), lambda i, j: (0, 0)),
        ],
        out_specs=pl.BlockSpec((tm, D_MODEL), lambda i, j: (i, 0)),
        out_shape=jax.ShapeDtypeStruct((t, D_MODEL), F32),
        scratch_shapes=[pltpu.VMEM((tm, D_MODEL), BF16), pltpu.VMEM((tm, D_MODEL), F32)],
        compiler_params=_cparams("parallel", "arbitrary"),
        name="mlp",
    )(x, g, w_up, w_down, g_final)


def _lane_row(values, lane0):
    return jnp.zeros((LANES,), F32).at[lane0:lane0 + values.shape[0]].set(values.astype(F32))


def _param_rows(a_log, dt_bias, lane0):
    rows = jnp.zeros((8, LANES), F32)
    return rows.at[0].set(_lane_row(a_log, lane0)).at[1].set(_lane_row(dt_bias, lane0))


def _sb_cumsum_matrix():
    j = jnp.arange(SB_TILE)[:, None]
    s = jnp.arange(SB_TILE)[None, :]
    half = jnp.concatenate([(j >= s).astype(BF16), jnp.ones((SB_TILE, SB_TILE), BF16)], axis=1)
    return jnp.concatenate([half, half], axis=0)


def _ssm_expand_matrix():
    lane = jnp.arange(LANES)[:, None]
    ch = jnp.arange(MIX_W)[None, :]
    return (lane == LANE_SSM_DT + ch // SSM_HEAD_DIM).astype(BF16)


def _split_w_in(w):
    big = jnp.concatenate([w[..., 0:4096], w[..., 4112:10256], w[..., 10272:13344]], axis=-1)
    pad = jnp.zeros(w.shape[:-1] + (LANES - 32,), w.dtype)
    small = jnp.concatenate([w[..., 4096:4112], w[..., 10256:10272], pad], axis=-1)
    return _bf(big), _bf(small)


def kernel(x, norm_mix, w_in, dn_conv_w, dn_a_log, dn_dt_bias, dn_norm_w, ssm_conv_w, ssm_conv_b, ssm_a_log,
           ssm_dt_bias, ssm_d, ssm_norm_w, w_branch, w_out, norm_mlp, w_up, w_down, norm_final):
    batch, seq, _ = x.shape
    depth = w_in.shape[0]
    h = x.reshape(batch * seq, D_MODEL)
    sb_w = _sb_cumsum_matrix()
    emat = _ssm_expand_matrix()
    g_final = norm_final.reshape(1, D_MODEL)
    w_big, w_small = _split_w_in(w_in)
    w_branch, w_out, w_up, w_down = _bf(w_branch), _bf(w_out), _bf(w_up), _bf(w_down)
    for l in range(depth):
        big, small = _inproj(h, norm_mix[l].reshape(1, D_MODEL), w_big, w_small, l)
        o_dn = _dn_mixer(big, small, dn_conv_w[l], _param_rows(dn_a_log[l], dn_dt_bias[l], LANE_DN_A),
                         dn_norm_w[l].reshape(1, DN_HEAD_DIM), batch, seq)
        o_sb = _sb_mixer(big, sb_w, batch, seq)
        o_ssm = _ssd_mixer(big, small, ssm_conv_w[l], ssm_conv_b[l].reshape(1, -1),
                           _param_rows(ssm_a_log[l], ssm_dt_bias[l], LANE_SSM_DT),
                           jnp.repeat(ssm_d[l], SSM_HEAD_DIM).reshape(1, MIX_W),
                           ssm_norm_w[l].reshape(1, MIX_W), emat, batch, seq)
        h = _merge(h, o_dn, o_sb, o_ssm, big, w_branch, w_out, l)
        h = _mlp(h, norm_mlp[l].reshape(1, D_MODEL), w_up, w_down, g_final, l, final_norm=(l == depth - 1))
    return h.reshape(batch, seq, D_MODEL)
```

```python
import functools

import jax
import jax.numpy as jnp
from jax import lax
from jax.experimental import pallas as pl
from jax.experimental.pallas import tpu as pltpu

F32 = jnp.float32
BF16 = jnp.bfloat16

D_MODEL = 1024
MIX_W = D_MODEL
DN_HEADS = 8
DN_HEAD_DIM = 128
SB_HEADS = 16
SB_HEAD_DIM = 64
SSM_HEADS = 16
SSM_HEAD_DIM = 64
SSM_STATE = 128
SSM_GROUPS = 4
CHUNK = 64
D_FF = 4 * D_MODEL
EPS = 1e-6
CONV_WIDTH = 4

LANES = 128
TBLK = 2 * CHUNK
HALO = 8
DN_BASE_BLOCK = 8

COL_DN_QKV = 0
COL_DN_GATE = 3072
COL_SB_QKV = 4096
COL_SSM_Z = 7168
COL_SSM_XBC = 8192
COL_GATES = 10240
N_BIG = 13312
LANE_DN_A = 0
LANE_DN_B = 8
LANE_SSM_DT = 16

VMEM_LIMIT = 56 * 1024 * 1024


def _cparams(*sem):
    return pltpu.CompilerParams(dimension_semantics=sem, vmem_limit_bytes=VMEM_LIMIT)


def _bf(x):
    return x.astype(BF16)


def _mm(a, b):
    return jnp.dot(_bf(a), _bf(b), preferred_element_type=F32)


def _mm_nt(a, b):
    return lax.dot_general(_bf(a), _bf(b), (((1,), (1,)), ((), ())), preferred_element_type=F32)


def _split2(a):
    hi = _bf(a)
    lo = _bf(a - hi.astype(F32))
    return hi, lo


def _split3(a):
    hi = _bf(a)
    r = a - hi.astype(F32)
    mid = _bf(r)
    lo = _bf(r - mid.astype(F32))
    return hi, mid, lo


def _mm_x3s(a_split, b_split):
    ah, al = a_split
    bh, bl = b_split
    lhs = jnp.concatenate([ah, al, ah], axis=1)
    rhs = jnp.concatenate([bh, bh, bl], axis=0)
    return jnp.dot(lhs, rhs, preferred_element_type=F32)


def _mm_sel_lhs(sel, b):
    s = _bf(sel)
    bh, bm, bl = _split3(b)
    return jnp.dot(jnp.concatenate([s, s, s], axis=1), jnp.concatenate([bh, bm, bl], axis=0),
                   preferred_element_type=F32)


def _mm_sel_rhs(a, sel_bf):
    ah, am, al = _split3(a)
    return jnp.dot(jnp.concatenate([ah, am, al], axis=1), jnp.concatenate([sel_bf, sel_bf, sel_bf], axis=0),
                   preferred_element_type=F32)


def _softplus(x):
    return jnp.maximum(x, 0.0) + jnp.log(1.0 + jnp.exp(-jnp.abs(x)))


def _sigmoid(x):
    return 1.0 / (1.0 + jnp.exp(-x))


def _silu(x):
    return x * _sigmoid(x)


def _rms(x, w):
    return x * lax.rsqrt(jnp.mean(x * x, axis=-1, keepdims=True) + EPS) * w


def _causal_conv(halo, cur, w, bias):
    rows = cur.shape[0]
    xx = jnp.concatenate([halo, cur], axis=0)
    y = w[CONV_WIDTH - 1:CONV_WIDTH, :] * cur
    for k in range(CONV_WIDTH - 1):
        shifted = pltpu.roll(xx, CONV_WIDTH - 1 - k, 0)[HALO:HALO + rows]
        y = y + w[k:k + 1, :] * shifted
    if bias is not None:
        y = y + bias
    return y


def _chunk_masks():
    row = lax.broadcasted_iota(jnp.int32, (TBLK, TBLK), 0)
    col = lax.broadcasted_iota(jnp.int32, (TBLK, TBLK), 1)
    same = (row >= CHUNK) == (col >= CHUNK)
    return row, col, same & (col <= row), same & (col < row)


def _inproj_kernel(x_ref, g_ref, wb_ref, ws_ref, big_ref, small_ref, xn_scr):
    @pl.when(pl.program_id(1) == 0)
    def _():
        xn_scr[...] = _bf(_rms(x_ref[...], g_ref[...]))
        small_ref[...] = jnp.dot(xn_scr[...], ws_ref[...], preferred_element_type=F32)

    big_ref[...] = _bf(jnp.dot(xn_scr[...], wb_ref[...], preferred_element_type=F32))


def _inproj(x, g, w_big, w_small, layer, tm=2048, tn=1024):
    t = x.shape[0]
    tm = min(tm, t)
    return pl.pallas_call(
        _inproj_kernel,
        grid=(t // tm, N_BIG // tn),
        in_specs=[
            pl.BlockSpec((tm, D_MODEL), lambda i, j: (i, 0)),
            pl.BlockSpec((1, D_MODEL), lambda i, j: (0, 0)),
            pl.BlockSpec((None, D_MODEL, tn), lambda i, j: (layer, 0, j)),
            pl.BlockSpec((None, D_MODEL, LANES), lambda i, j: (layer, 0, 0)),
        ],
        out_specs=[
            pl.BlockSpec((tm, tn), lambda i, j: (i, j)),
            pl.BlockSpec((tm, LANES), lambda i, j: (i, 0)),
        ],
        out_shape=[jax.ShapeDtypeStruct((t, N_BIG), BF16), jax.ShapeDtypeStruct((t, LANES), F32)],
        scratch_shapes=[pltpu.VMEM((tm, D_MODEL), BF16)],
        compiler_params=_cparams("parallel", "arbitrary"),
        name="inproj",
    )(x, g, w_big, w_small)


def _dn_kernel(qkv_ref, gate_ref, sm_ref, cw_ref, prow_ref, nw_ref, o_ref, halo_scr, state_scr):
    @pl.when(pl.program_id(1) == 0)
    def _():
        halo_scr[...] = jnp.zeros_like(halo_scr)
        state_scr[...] = jnp.zeros_like(state_scr)

    row, col, m_incl, m_strict = _chunk_masks()
    tril = jnp.where(m_incl, 1.0, 0.0)
    eye = jnp.where(row == col, 1.0, 0.0)
    first = row[:, 0:1] < CHUNK

    sm = sm_ref[...]
    g_all = -jnp.exp(prow_ref[0:1, :]) * _softplus(sm + prow_ref[1:2, :])
    beta_all = _sigmoid(sm)
    gc_all = _mm_sel_lhs(tril, g_all)
    gct_all = gc_all.T

    heads = range(DN_HEADS)

    def conv_silu(seg, h):
        lanes = slice(seg * MIX_W + h * DN_HEAD_DIM, seg * MIX_W + (h + 1) * DN_HEAD_DIM)
        cur = qkv_ref[:, lanes].astype(F32)
        return _silu(_causal_conv(halo_scr[:, lanes], cur, cw_ref[:, lanes], None))

    q, k, v, kb, vb, decay, gc_col, kk, qk = {}, {}, {}, {}, {}, {}, {}, {}, {}
    for h in heads:
        qh = conv_silu(0, h)
        kh = conv_silu(1, h)
        v[h] = conv_silu(2, h)
        q[h] = qh * lax.rsqrt(jnp.sum(qh * qh, axis=-1, keepdims=True) + EPS) * (DN_HEAD_DIM ** -0.5)
        k[h] = kh * lax.rsqrt(jnp.sum(kh * kh, axis=-1, keepdims=True) + EPS)
        gc_col[h] = gc_all[:, LANE_DN_A + h:LANE_DN_A + h + 1]
        beta = beta_all[:, LANE_DN_B + h:LANE_DN_B + h + 1]
        gc_row = gct_all[LANE_DN_A + h:LANE_DN_A + h + 1, :]
        decay[h] = jnp.exp(jnp.where(m_incl, gc_col[h] - gc_row, -jnp.inf))
        kb[h] = k[h] * beta
        vb[h] = v[h] * beta
    for h in heads:
        kk[h] = _mm_nt(kb[h], k[h])
        qk[h] = _mm_nt(q[h], k[h])

    def same_block(size):
        shift = size.bit_length() - 1
        return (row >> shift) == (col >> shift)

    low, xs, tinv = {}, {}, {}
    for h in heads:
        low[h] = jnp.where(m_strict, kk[h] * decay[h], 0.0)
        x = -jnp.where(same_block(DN_BASE_BLOCK), low[h], 0.0)
        xs[h] = _split2(x)
        tinv[h] = eye + x
    for _ in range(2):
        for h in heads:
            xs[h] = _split2(_mm_x3s(xs[h], xs[h]))
        for h in heads:
            tinv[h] = tinv[h] + _mm_x3s(_split2(tinv[h]), xs[h])
    size = DN_BASE_BLOCK
    while size < CHUNK:
        off_diag = same_block(2 * size) & jnp.logical_not(same_block(size))
        ts, cd = {}, {}
        for h in heads:
            ts[h] = _split2(tinv[h])
            cd[h] = _mm_x3s(_split2(jnp.where(off_diag, low[h], 0.0)), ts[h])
        for h in heads:
            tinv[h] = tinv[h] - _mm_x3s(ts[h], _split2(cd[h]))
        size *= 2

    u, w, attn, qg, kdt, gl = {}, {}, {}, {}, {}, {}
    for h in heads:
        egc = jnp.exp(gc_col[h])
        ts = _split2(tinv[h])
        u[h] = _mm_x3s(ts, _split2(vb[h]))
        w[h] = _mm_x3s(ts, _split2(kb[h] * egc))
        attn[h] = _bf(jnp.where(m_incl, qk[h] * decay[h], 0.0))
        qg[h] = _bf(q[h] * egc)
        gl[h] = (gc_col[h][CHUNK - 1:CHUNK, :], gc_col[h][TBLK - 1:TBLK, :])
        kdt[h] = _bf((k[h] * jnp.exp(jnp.where(first, gl[h][0], gl[h][1]) - gc_col[h])).T)

    s = {h: state_scr[h] for h in heads}
    outs = {h: [] for h in heads}
    pad = jnp.zeros((CHUNK, DN_HEAD_DIM), F32)
    for c in range(2):
        r0, r1 = c * CHUNK, (c + 1) * CHUNK
        v_blk = {}
        for h in heads:
            v_new = u[h][r0:r1] - _mm(w[h][r0:r1], s[h])
            v_blk[h] = _bf(jnp.concatenate([v_new, pad] if c == 0 else [pad, v_new], axis=0))
        for h in heads:
            outs[h].append(_mm(qg[h][r0:r1], s[h]) + _mm(attn[h][r0:r1, :], v_blk[h]))
            s[h] = s[h] * jnp.exp(gl[h][c]) + _mm(kdt[h], v_blk[h])
    for h in heads:
        state_scr[h] = s[h]
        lanes = slice(h * DN_HEAD_DIM, (h + 1) * DN_HEAD_DIM)
        o = jnp.concatenate(outs[h], axis=0)
        o_ref[:, lanes] = _bf(_rms(o, nw_ref[...]) * _silu(gate_ref[:, lanes].astype(F32)))
    halo_scr[...] = qkv_ref[TBLK - 16:TBLK, :].astype(F32)[16 - HALO:16]


def _dn_mixer(big, small, conv_w, prow, norm_w, batch, seq):
    nd = seq // TBLK
    row_map = lambda b, d: (b * nd + d, 0)
    return pl.pallas_call(
        _dn_kernel,
        grid=(batch, nd),
        in_specs=[
            pl.BlockSpec((TBLK, 3 * MIX_W), lambda b, d: (b * nd + d, COL_DN_QKV // (3 * MIX_W))),
            pl.BlockSpec((TBLK, MIX_W), lambda b, d: (b * nd + d, COL_DN_GATE // MIX_W)),
            pl.BlockSpec((TBLK, LANES), row_map),
            pl.BlockSpec((CONV_WIDTH, 3 * MIX_W), lambda b, d: (0, 0)),
            pl.BlockSpec((8, LANES), lambda b, d: (0, 0)),
            pl.BlockSpec((1, DN_HEAD_DIM), lambda b, d: (0, 0)),
        ],
        out_specs=pl.BlockSpec((TBLK, MIX_W), row_map),
        out_shape=jax.ShapeDtypeStruct((batch * seq, MIX_W), BF16),
        scratch_shapes=[pltpu.VMEM((HALO, 3 * MIX_W), F32), pltpu.VMEM((DN_HEADS, DN_HEAD_DIM, DN_HEAD_DIM), F32)],
        compiler_params=_cparams("parallel", "arbitrary"),
        name="deltanet",
    )(big, big, small, conv_w, prow, norm_w)


SB_TILE = 128
SB_TQ = 256
SB_SUB = SB_TQ // SB_TILE
SB_STRIP = 128
SB_HEADS_PER_STEP = 4
LOG2E = 1.4426950408889634
SB_LAG = 4


def _sb_kernel(q_ref, k_ref, v_ref, w_ref, o_ref, r_scr, acc_scr, za_scr, zb_scr, wgt_scr):
    qi = pl.program_id(2)
    row = lax.broadcasted_iota(jnp.int32, (SB_STRIP, SB_TILE), 0)
    col = lax.broadcasted_iota(jnp.int32, (SB_STRIP, SB_TILE), 1)
    wmat = w_ref[...]
    r_scr[...] = jnp.zeros_like(r_scr)
    acc_scr[...] = jnp.zeros_like(acc_scr)
    strips = range(SB_TQ // SB_STRIP)
    heads = range(SB_HEADS_PER_STEP)
    pairs = range(SB_HEADS_PER_STEP // 2)
    order = list(reversed(range(SB_SUB)))
    qh = {}
    for s in strips:
        for h in heads:
            lanes = slice((h // 2) * LANES, (h // 2 + 1) * LANES)
            q = q_ref[s * SB_STRIP:(s + 1) * SB_STRIP, lanes].astype(F32) * (SB_HEAD_DIM ** -0.5)
            qh[s, h] = _bf(jnp.where((col >= SB_HEAD_DIM) == (h % 2 == 1), q, 0.0))

    def chain_id(t, s, h):
        return (t * len(strips) + s) * SB_HEADS_PER_STEP + h

    def masked_out(t, s, diagonal):
        return diagonal and t * SB_TILE >= (s + 1) * SB_STRIP

    def block_rows(kb_lo):
        return pl.ds(pl.multiple_of(kb_lo * SB_TILE, SB_TILE), SB_SUB * SB_TILE)

    def score_stage(kb_lo, z_dst, diagonal):
        kblk = [k_ref[block_rows(kb_lo), p * LANES:(p + 1) * LANES] for p in pairs]
        for t in order:
            for s in strips:
                for h in heads:
                    if not masked_out(t, s, diagonal):
                        z_dst[chain_id(t, s, h)] = _mm_nt(qh[s, h], kblk[h // 2][t * SB_TILE:(t + 1) * SB_TILE]) * LOG2E

    def weight_stage(z_src, diagonal):
        chains = [(t, s, h) for t in order for s in strips for h in heads]
        cs, tri = {}, {}

        def cumsum(c):
            t, s, _ = c
            z = z_src[chain_id(*c)]
            neg_abs = pltpu.bitcast(pltpu.bitcast(z, jnp.uint32) | jnp.uint32(0x80000000), F32)
            sp = jnp.maximum(z, 0.0) + jnp.log2(1.0 + jnp.exp2(neg_abs))
            if diagonal:
                tri[c] = col + t * SB_TILE < row + s * SB_STRIP
                sp = jnp.where(tri[c], sp, 0.0)
            hi, lo = _split2(sp)
            cs[c] = jnp.dot(jnp.concatenate([hi, lo], axis=1), wmat, preferred_element_type=F32)

        def weigh(c):
            t, s, h = c
            rows = slice(s * SB_STRIP, (s + 1) * SB_STRIP)
            lanes = slice(t * SB_TILE, (t + 1) * SB_TILE)
            wgt = jnp.exp2(z_src[chain_id(*c)] - cs[c][:, :SB_TILE] - r_scr[h, rows])
            wgt_scr[h, rows, lanes] = _bf(jnp.where(tri[c], wgt, 0.0) if diagonal else wgt)
            r_scr[h, rows] += cs[c][:, SB_TILE:]

        live = [c for c in chains if not masked_out(c[0], c[1], diagonal)]
        for t, s, h in chains:
            if masked_out(t, s, diagonal):
                wgt_scr[h, s * SB_STRIP:(s + 1) * SB_STRIP, t * SB_TILE:(t + 1) * SB_TILE] = (
                    jnp.zeros((SB_STRIP, SB_TILE), BF16))
        for i, c in enumerate(live):
            cumsum(c)
            if i >= SB_LAG:
                weigh(live[i - SB_LAG])
        for c in live[-SB_LAG:]:
            weigh(c)

    def value_stage(kb_lo):
        vblk = [v_ref[block_rows(kb_lo), p * LANES:(p + 1) * LANES] for p in pairs]
        for h in heads:
            acc_scr[h] += jnp.dot(wgt_scr[h], vblk[h // 2], preferred_element_type=F32)

    def step(cur, z_src, z_dst, issue_next=True):
        value_stage((cur + 1) * SB_SUB)
        if issue_next:
            score_stage(jnp.maximum(cur - 1, 0) * SB_SUB, z_dst, False)
        weight_stage(z_src, False)

    score_stage(qi * SB_SUB, za_scr, True)
    score_stage(jnp.maximum(qi - 1, 0) * SB_SUB, zb_scr, False)
    weight_stage(za_scr, True)

    def body(j, carry):
        step(qi - 1 - 2 * j, zb_scr, za_scr)
        step(qi - 2 - 2 * j, za_scr, zb_scr)
        return carry

    lax.fori_loop(0, qi // 2, body, 0)

    @pl.when(qi % 2 == 1)
    def _():
        step(0, zb_scr, za_scr, issue_next=False)

    value_stage(0)
    lane = lax.broadcasted_iota(jnp.int32, (SB_TQ, SB_TILE), 1)
    for p in range(SB_HEADS_PER_STEP // 2):
        o_ref[:, p * LANES:(p + 1) * LANES] = _bf(jnp.where(lane < SB_HEAD_DIM, acc_scr[2 * p], acc_scr[2 * p + 1]))


def _sb_mixer(big, wmat, batch, seq):
    nq = seq // SB_TQ
    width = SB_HEADS_PER_STEP * SB_HEAD_DIM
    groups = MIX_W // width
    qoff = COL_SB_QKV // width
    return pl.pallas_call(
        _sb_kernel,
        grid=(batch, groups, nq),
        in_specs=[
            pl.BlockSpec((SB_TQ, width), lambda b, p, i: (b * nq + i, qoff + p)),
            pl.BlockSpec((seq, width), lambda b, p, i: (b, qoff + groups + p)),
            pl.BlockSpec((seq, width), lambda b, p, i: (b, qoff + 2 * groups + p)),
            pl.BlockSpec((2 * SB_TILE, 2 * SB_TILE), lambda b, p, i: (0, 0)),
        ],
        out_specs=pl.BlockSpec((SB_TQ, width), lambda b, p, i: (b * nq + i, p)),
        out_shape=jax.ShapeDtypeStruct((batch * seq, MIX_W), BF16),
        scratch_shapes=[pltpu.VMEM((SB_HEADS_PER_STEP, SB_TQ, SB_TILE), F32),
                        pltpu.VMEM((SB_HEADS_PER_STEP, SB_TQ, SB_TILE), F32),
                        pltpu.VMEM((SB_SUB * (SB_TQ // SB_STRIP) * SB_HEADS_PER_STEP, SB_STRIP, SB_TILE), F32),
                        pltpu.VMEM((SB_SUB * (SB_TQ // SB_STRIP) * SB_HEADS_PER_STEP, SB_STRIP, SB_TILE), F32),
                        pltpu.VMEM((SB_HEADS_PER_STEP, SB_TQ, SB_SUB * SB_TILE), BF16)],
        compiler_params=_cparams("parallel", "parallel", "arbitrary"),
        name="stickbreak",
    )(big, big, big, wmat)


GROUP_W = MIX_W // SSM_GROUPS
HEADS_PER_GROUP = SSM_HEADS // SSM_GROUPS


def _ssd_kernel(z_ref, xbc_ref, sm_ref, cw_ref, cb_ref, prow_ref, drow_ref, nw_ref, e_ref, o_ref,
                halo_scr, state_scr):
    @pl.when(pl.program_id(1) == 0)
    def _():
        halo_scr[...] = jnp.zeros_like(halo_scr)
        state_scr[...] = jnp.zeros_like(state_scr)

    row, col, m_incl, _ = _chunk_masks()
    tril = jnp.where(m_incl, 1.0, 0.0)
    first = row[:, 0:1] < CHUNK
    gcol = lax.broadcasted_iota(jnp.int32, (TBLK, GROUP_W), 1)

    cur = xbc_ref[...].astype(F32)
    xbc = _silu(_causal_conv(halo_scr[...], cur, cw_ref[...], cb_ref[...]))
    halo_scr[...] = cur[TBLK - HALO:TBLK]
    gn = SSM_GROUPS * SSM_STATE
    xs = xbc[:, :MIX_W]
    bm = xbc[:, MIX_W:MIX_W + gn]
    cm = xbc[:, MIX_W + gn:]

    dt_all = _softplus(sm_ref[...] + prow_ref[1:2, :])
    a_all = -jnp.exp(prow_ref[0:1, :]) * dt_all
    acum_all = _mm_sel_lhs(tril, a_all)
    acum_t = acum_all.T
    e = e_ref[...]
    dt_exp = _mm_sel_rhs(dt_all, e)
    acum_exp = _mm_sel_rhs(acum_all, e)
    xdt = xs * dt_exp

    for g in range(SSM_GROUPS):
        c_g = cm[:, g * SSM_STATE:(g + 1) * SSM_STATE]
        b_g = bm[:, g * SSM_STATE:(g + 1) * SSM_STATE]
        scores = _mm_nt(c_g, b_g)
        lo, hi = g * GROUP_W, (g + 1) * GROUP_W
        x_g = xdt[:, lo:hi]
        a_g = acum_exp[:, lo:hi]
        y_g = jnp.zeros((TBLK, GROUP_W), F32)
        for hh in range(HEADS_PER_GROUP):
            h = g * HEADS_PER_GROUP + hh
            a_col = acum_exp[:, h * SSM_HEAD_DIM:h * SSM_HEAD_DIM + 1]
            a_row = acum_t[LANE_SSM_DT + h:LANE_SSM_DT + h + 1, :]
            lmat = jnp.exp(jnp.where(m_incl, a_col - a_row, -jnp.inf))
            y_g = jnp.where(gcol // SSM_HEAD_DIM == hh, _mm(scores * lmat, x_g), y_g)

        b_t = b_g.T
        a_last0 = a_g[CHUNK - 1:CHUNK, :]
        a_last1 = a_g[TBLK - 1:TBLK, :]
        x_sc = x_g * jnp.exp(jnp.where(first, a_last0, a_last1) - a_g)
        st = state_scr[g]
        yoff = []
        for c, a_last in ((0, a_last0), (1, a_last1)):
            r0, r1 = c * CHUNK, (c + 1) * CHUNK
            yoff.append(_mm(c_g[r0:r1], st) * jnp.exp(a_g[r0:r1]))
            st = st * jnp.exp(a_last) + _mm(b_t, jnp.where(first == (c == 0), x_sc, 0.0))
        state_scr[g] = st
        y_g = y_g + jnp.concatenate(yoff, axis=0) + xs[:, lo:hi] * drow_ref[:, lo:hi]
        y_g = y_g * _silu(z_ref[:, lo:hi].astype(F32))
        o_ref[:, lo:hi] = _bf(_rms(y_g, nw_ref[:, lo:hi]))


def _ssd_mixer(big, small, conv_w, conv_b, prow, drow, norm_w, emat, batch, seq):
    nd = seq // TBLK
    conv_dim = MIX_W + 2 * SSM_GROUPS * SSM_STATE
    row_map = lambda b, d: (b * nd + d, 0)
    const = lambda b, d: (0, 0)
    return pl.pallas_call(
        _ssd_kernel,
        grid=(batch, nd),
        in_specs=[
            pl.BlockSpec((TBLK, MIX_W), lambda b, d: (b * nd + d, COL_SSM_Z // MIX_W)),
            pl.BlockSpec((TBLK, conv_dim), lambda b, d: (b * nd + d, COL_SSM_XBC // conv_dim)),
            pl.BlockSpec((TBLK, LANES), row_map),
            pl.BlockSpec((CONV_WIDTH, conv_dim), const),
            pl.BlockSpec((1, conv_dim), const),
            pl.BlockSpec((8, LANES), const),
            pl.BlockSpec((1, MIX_W), const),
            pl.BlockSpec((1, MIX_W), const),
            pl.BlockSpec((LANES, MIX_W), const),
        ],
        out_specs=pl.BlockSpec((TBLK, MIX_W), row_map),
        out_shape=jax.ShapeDtypeStruct((batch * seq, MIX_W), BF16),
        scratch_shapes=[pltpu.VMEM((HALO, conv_dim), F32), pltpu.VMEM((SSM_GROUPS, SSM_STATE, GROUP_W), F32)],
        compiler_params=_cparams("parallel", "arbitrary"),
        name="ssd",
    )(big, big, small, conv_w, conv_b, prow, drow, norm_w, emat)


def _merge_kernel(x_ref, odn_ref, osb_ref, ossm_ref, g0_ref, g1_ref, g2_ref, wb_ref, wo_ref, o_ref):
    m = _sigmoid(g0_ref[...].astype(F32)) * jnp.dot(odn_ref[...], wb_ref[0], preferred_element_type=F32)
    m = m + _sigmoid(g1_ref[...].astype(F32)) * jnp.dot(osb_ref[...], wb_ref[1], preferred_element_type=F32)
    m = m + _sigmoid(g2_ref[...].astype(F32)) * jnp.dot(ossm_ref[...], wb_ref[2], preferred_element_type=F32)
    o_ref[...] = x_ref[...] + jnp.dot(_bf(m), wo_ref[...], preferred_element_type=F32)


def _merge(x, o_dn, o_sb, o_ssm, big, w_branch, w_out, layer, tm=512):
    t = x.shape[0]
    tm = min(tm, t)
    rows = lambda i: (i, 0)
    gcol = COL_GATES // D_MODEL
    return pl.pallas_call(
        _merge_kernel,
        grid=(t // tm,),
        in_specs=[
            pl.BlockSpec((tm, D_MODEL), rows),
            pl.BlockSpec((tm, MIX_W), rows),
            pl.BlockSpec((tm, MIX_W), rows),
            pl.BlockSpec((tm, MIX_W), rows),
            pl.BlockSpec((tm, D_MODEL), lambda i: (i, gcol)),
            pl.BlockSpec((tm, D_MODEL), lambda i: (i, gcol + 1)),
            pl.BlockSpec((tm, D_MODEL), lambda i: (i, gcol + 2)),
            pl.BlockSpec((None, 3, MIX_W, D_MODEL), lambda i: (layer, 0, 0, 0)),
            pl.BlockSpec((None, D_MODEL, D_MODEL), lambda i: (layer, 0, 0)),
        ],
        out_specs=pl.BlockSpec((tm, D_MODEL), rows),
        out_shape=jax.ShapeDtypeStruct((t, D_MODEL), F32),
        compiler_params=_cparams("parallel"),
        name="merge",
    )(x, o_dn, o_sb, o_ssm, big, big, big, w_branch, w_out)


def _mlp_kernel(x_ref, g_ref, wu_ref, wd_ref, gf_ref, o_ref, xn_scr, acc_scr, *, final_norm):
    j = pl.program_id(1)

    @pl.when(j == 0)
    def _():
        xn_scr[...] = _bf(_rms(x_ref[...], g_ref[...]))
        acc_scr[...] = jnp.zeros_like(acc_scr)

    h = jnp.dot(xn_scr[...], wu_ref[...], preferred_element_type=F32)
    h = jnp.square(jnp.maximum(h, 0.0))
    acc_scr[...] += jnp.dot(_bf(h), wd_ref[...], preferred_element_type=F32)

    @pl.when(j == pl.num_programs(1) - 1)
    def _():
        y = x_ref[...] + acc_scr[...]
        o_ref[...] = _rms(y, gf_ref[...]) if final_norm else y


def _mlp(x, g, w_up, w_down, g_final, layer, final_norm, tm=1024, tf=1024):
    t = x.shape[0]
    tm = min(tm, t)
    return pl.pallas_call(
        functools.partial(_mlp_kernel, final_norm=final_norm),
        grid=(t // tm, D_FF // tf),
        in_specs=[
            pl.BlockSpec((tm, D_MODEL), lambda i, j: (i, 0)),
            pl.BlockSpec((1, D_MODEL), lambda i, j: (0, 0)),
            pl.BlockSpec((None, D_MODEL, tf), lambda i, j: (layer, 0, j)),
            pl.BlockSpec((None, tf, D_MODEL), lambda i, j: (layer, j, 0)),
            pl.BlockSpec((1, D_MODEL), lambda i, j: (0, 0)),
        ],
        out_specs=pl.BlockSpec((tm, D_MODEL), lambda i, j: (i, 0)),
        out_shape=jax.ShapeDtypeStruct((t, D_MODEL), F32),
        scratch_shapes=[pltpu.VMEM((tm, D_MODEL), BF16), pltpu.VMEM((tm, D_MODEL), F32)],
        compiler_params=_cparams("parallel", "arbitrary"),
        name="mlp",
    )(x, g, w_up, w_down, g_final)


def _lane_row(values, lane0):
    return jnp.zeros((LANES,), F32).at[lane0:lane0 + values.shape[0]].set(values.astype(F32))


def _param_rows(a_log, dt_bias, lane0):
    rows = jnp.zeros((8, LANES), F32)
    return rows.at[0].set(_lane_row(a_log, lane0)).at[1].set(_lane_row(dt_bias, lane0))


def _sb_cumsum_matrix():
    j = jnp.arange(SB_TILE)[:, None]
    s = jnp.arange(SB_TILE)[None, :]
    half = jnp.concatenate([(j >= s).astype(BF16), jnp.ones((SB_TILE, SB_TILE), BF16)], axis=1)
    return jnp.concatenate([half, half], axis=0)


def _ssm_expand_matrix():
    lane = jnp.arange(LANES)[:, None]
    ch = jnp.arange(MIX_W)[None, :]
    return (lane == LANE_SSM_DT + ch // SSM_HEAD_DIM).astype(BF16)


def _split_w_in(w):
    big = jnp.concatenate([w[..., 0:4096], w[..., 4112:10256], w[..., 10272:13344]], axis=-1)
    pad = jnp.zeros(w.shape[:-1] + (LANES - 32,), w.dtype)
    small = jnp.concatenate([w[..., 4096:4112], w[..., 10256:10272], pad], axis=-1)
    return _bf(big), _bf(small)


def kernel(x, norm_mix, w_in, dn_conv_w, dn_a_log, dn_dt_bias, dn_norm_w, ssm_conv_w, ssm_conv_b, ssm_a_log,
           ssm_dt_bias, ssm_d, ssm_norm_w, w_branch, w_out, norm_mlp, w_up, w_down, norm_final):
    batch, seq, _ = x.shape
    depth = w_in.shape[0]
    h = x.reshape(batch * seq, D_MODEL)
    sb_w = _sb_cumsum_matrix()
    emat = _ssm_expand_matrix()
    g_final = norm_final.reshape(1, D_MODEL)
    w_big, w_small = _split_w_in(w_in)
    w_branch, w_out, w_up, w_down = _bf(w_branch), _bf(w_out), _bf(w_up), _bf(w_down)
    for l in range(depth):
        big, small = _inproj(h, norm_mix[l].reshape(1, D_MODEL), w_big, w_small, l)
        o_dn = _dn_mixer(big, small, dn_conv_w[l], _param_rows(dn_a_log[l], dn_dt_bias[l], LANE_DN_A),
                         dn_norm_w[l].reshape(1, DN_HEAD_DIM), batch, seq)
        o_sb = _sb_mixer(big, sb_w, batch, seq)
        o_ssm = _ssd_mixer(big, small, ssm_conv_w[l], ssm_conv_b[l].reshape(1, -1),
                           _param_rows(ssm_a_log[l], ssm_dt_bias[l], LANE_SSM_DT),
                           jnp.repeat(ssm_d[l], SSM_HEAD_DIM).reshape(1, MIX_W),
                           ssm_norm_w[l].reshape(1, MIX_W), emat, batch, seq)
        h = _merge(h, o_dn, o_sb, o_ssm, big, w_branch, w_out, l)
        h = _mlp(h, norm_mlp[l].reshape(1, D_MODEL), w_up, w_down, g_final, l, final_norm=(l == depth - 1))
    return h.reshape(batch, seq, D_MODEL)
```

```python
import functools

import jax
import jax.numpy as jnp
from jax import lax
from jax.experimental import pallas as pl
from jax.experimental.pallas import tpu as pltpu

F32 = jnp.float32
BF16 = jnp.bfloat16

D_MODEL = 1024
MIX_W = D_MODEL
DN_HEADS = 8
DN_HEAD_DIM = 128
SB_HEADS = 16
SB_HEAD_DIM = 64
SSM_HEADS = 16
SSM_HEAD_DIM = 64
SSM_STATE = 128
SSM_GROUPS = 4
CHUNK = 64
D_FF = 4 * D_MODEL
EPS = 1e-6
CONV_WIDTH = 4

LANES = 128
TBLK = 2 * CHUNK
HALO = 8
DN_BASE_BLOCK = 8

COL_DN_QKV = 0
COL_DN_GATE = 3072
COL_SB_QKV = 4096
COL_SSM_Z = 7168
COL_SSM_XBC = 8192
COL_GATES = 10240
N_BIG = 13312
LANE_DN_A = 0
LANE_DN_B = 8
LANE_SSM_DT = 16

VMEM_LIMIT = 56 * 1024 * 1024


def _cparams(*sem):
    return pltpu.CompilerParams(dimension_semantics=sem, vmem_limit_bytes=VMEM_LIMIT)


def _bf(x):
    return x.astype(BF16)


def _mm(a, b):
    return jnp.dot(_bf(a), _bf(b), preferred_element_type=F32)


def _mm_nt(a, b):
    return lax.dot_general(_bf(a), _bf(b), (((1,), (1,)), ((), ())), preferred_element_type=F32)


def _split2(a):
    hi = _bf(a)
    lo = _bf(a - hi.astype(F32))
    return hi, lo


def _split3(a):
    hi = _bf(a)
    r = a - hi.astype(F32)
    mid = _bf(r)
    lo = _bf(r - mid.astype(F32))
    return hi, mid, lo


def _mm_x3s(a_split, b_split):
    ah, al = a_split
    bh, bl = b_split
    lhs = jnp.concatenate([ah, al, ah], axis=1)
    rhs = jnp.concatenate([bh, bh, bl], axis=0)
    return jnp.dot(lhs, rhs, preferred_element_type=F32)


def _mm_sel_lhs(sel, b):
    s = _bf(sel)
    bh, bm, bl = _split3(b)
    return jnp.dot(jnp.concatenate([s, s, s], axis=1), jnp.concatenate([bh, bm, bl], axis=0),
                   preferred_element_type=F32)


def _mm_sel_rhs(a, sel_bf):
    ah, am, al = _split3(a)
    return jnp.dot(jnp.concatenate([ah, am, al], axis=1), jnp.concatenate([sel_bf, sel_bf, sel_bf], axis=0),
                   preferred_element_type=F32)


def _softplus(x):
    return jnp.maximum(x, 0.0) + jnp.log(1.0 + jnp.exp(-jnp.abs(x)))


def _sigmoid(x):
    return 1.0 / (1.0 + jnp.exp(-x))


def _silu(x):
    return x * _sigmoid(x)


def _rms(x, w):
    return x * lax.rsqrt(jnp.mean(x * x, axis=-1, keepdims=True) + EPS) * w


def _causal_conv(halo, cur, w, bias):
    rows = cur.shape[0]
    xx = jnp.concatenate([halo, cur], axis=0)
    y = w[CONV_WIDTH - 1:CONV_WIDTH, :] * cur
    for k in range(CONV_WIDTH - 1):
        shifted = pltpu.roll(xx, CONV_WIDTH - 1 - k, 0)[HALO:HALO + rows]
        y = y + w[k:k + 1, :] * shifted
    if bias is not None:
        y = y + bias
    return y


def _chunk_masks():
    row = lax.broadcasted_iota(jnp.int32, (TBLK, TBLK), 0)
    col = lax.broadcasted_iota(jnp.int32, (TBLK, TBLK), 1)
    same = (row >= CHUNK) == (col >= CHUNK)
    return row, col, same & (col <= row), same & (col < row)


def _inproj_kernel(x_ref, g_ref, wb_ref, ws_ref, big_ref, small_ref, xn_scr):
    @pl.when(pl.program_id(1) == 0)
    def _():
        xn_scr[...] = _bf(_rms(x_ref[...], g_ref[...]))
        small_ref[...] = jnp.dot(xn_scr[...], ws_ref[...], preferred_element_type=F32)

    big_ref[...] = _bf(jnp.dot(xn_scr[...], wb_ref[...], preferred_element_type=F32))


def _inproj(x, g, w_big, w_small, layer, tm=2048, tn=1024):
    t = x.shape[0]
    tm = min(tm, t)
    return pl.pallas_call(
        _inproj_kernel,
        grid=(t // tm, N_BIG // tn),
        in_specs=[
            pl.BlockSpec((tm, D_MODEL), lambda i, j: (i, 0)),
            pl.BlockSpec((1, D_MODEL), lambda i, j: (0, 0)),
            pl.BlockSpec((None, D_MODEL, tn), lambda i, j: (layer, 0, j)),
            pl.BlockSpec((None, D_MODEL, LANES), lambda i, j: (layer, 0, 0)),
        ],
        out_specs=[
            pl.BlockSpec((tm, tn), lambda i, j: (i, j)),
            pl.BlockSpec((tm, LANES), lambda i, j: (i, 0)),
        ],
        out_shape=[jax.ShapeDtypeStruct((t, N_BIG), BF16), jax.ShapeDtypeStruct((t, LANES), F32)],
        scratch_shapes=[pltpu.VMEM((tm, D_MODEL), BF16)],
        compiler_params=_cparams("parallel", "arbitrary"),
        name="inproj",
    )(x, g, w_big, w_small)


def _dn_kernel(qkv_ref, gate_ref, sm_ref, cw_ref, prow_ref, nw_ref, o_ref, halo_scr, state_scr):
    @pl.when(pl.program_id(1) == 0)
    def _():
        halo_scr[...] = jnp.zeros_like(halo_scr)
        state_scr[...] = jnp.zeros_like(state_scr)

    row, col, m_incl, m_strict = _chunk_masks()
    tril = jnp.where(m_incl, 1.0, 0.0)
    eye = jnp.where(row == col, 1.0, 0.0)
    first = row[:, 0:1] < CHUNK

    sm = sm_ref[...]
    g_all = -jnp.exp(prow_ref[0:1, :]) * _softplus(sm + prow_ref[1:2, :])
    beta_all = _sigmoid(sm)
    gc_all = _mm_sel_lhs(tril, g_all)
    gct_all = gc_all.T

    heads = range(DN_HEADS)

    def conv_silu(seg, h):
        lanes = slice(seg * MIX_W + h * DN_HEAD_DIM, seg * MIX_W + (h + 1) * DN_HEAD_DIM)
        cur = qkv_ref[:, lanes].astype(F32)
        return _silu(_causal_conv(halo_scr[:, lanes], cur, cw_ref[:, lanes], None))

    q, k, v, kb, vb, decay, gc_col, kk, qk = {}, {}, {}, {}, {}, {}, {}, {}, {}
    for h in heads:
        qh = conv_silu(0, h)
        kh = conv_silu(1, h)
        v[h] = conv_silu(2, h)
        q[h] = qh * lax.rsqrt(jnp.sum(qh * qh, axis=-1, keepdims=True) + EPS) * (DN_HEAD_DIM ** -0.5)
        k[h] = kh * lax.rsqrt(jnp.sum(kh * kh, axis=-1, keepdims=True) + EPS)
        gc_col[h] = gc_all[:, LANE_DN_A + h:LANE_DN_A + h + 1]
        beta = beta_all[:, LANE_DN_B + h:LANE_DN_B + h + 1]
        gc_row = gct_all[LANE_DN_A + h:LANE_DN_A + h + 1, :]
        decay[h] = jnp.exp(jnp.where(m_incl, gc_col[h] - gc_row, -jnp.inf))
        kb[h] = k[h] * beta
        vb[h] = v[h] * beta
    for h in heads:
        kk[h] = _mm_nt(kb[h], k[h])
        qk[h] = _mm_nt(q[h], k[h])

    def same_block(size):
        shift = size.bit_length() - 1
        return (row >> shift) == (col >> shift)

    low, xs, tinv = {}, {}, {}
    for h in heads:
        low[h] = jnp.where(m_strict, kk[h] * decay[h], 0.0)
        x = -jnp.where(same_block(DN_BASE_BLOCK), low[h], 0.0)
        xs[h] = _split2(x)
        tinv[h] = eye + x
    for _ in range(2):
        for h in heads:
            xs[h] = _split2(_mm_x3s(xs[h], xs[h]))
        for h in heads:
            tinv[h] = tinv[h] + _mm_x3s(_split2(tinv[h]), xs[h])
    size = DN_BASE_BLOCK
    while size < CHUNK:
        off_diag = same_block(2 * size) & jnp.logical_not(same_block(size))
        cd = {}
        for h in heads:
            cd[h] = _mm(jnp.where(off_diag, low[h], 0.0), tinv[h])
        for h in heads:
            tinv[h] = tinv[h] - _mm(tinv[h], cd[h])
        size *= 2

    u, w, attn, qg, kdt, gl = {}, {}, {}, {}, {}, {}
    for h in heads:
        egc = jnp.exp(gc_col[h])
        u[h] = _mm(tinv[h], vb[h])
        w[h] = _mm(tinv[h], kb[h] * egc)
        attn[h] = _bf(jnp.where(m_incl, qk[h] * decay[h], 0.0))
        qg[h] = _bf(q[h] * egc)
        gl[h] = (gc_col[h][CHUNK - 1:CHUNK, :], gc_col[h][TBLK - 1:TBLK, :])
        kdt[h] = _bf((k[h] * jnp.exp(jnp.where(first, gl[h][0], gl[h][1]) - gc_col[h])).T)

    s = {h: state_scr[h] for h in heads}
    outs = {h: [] for h in heads}
    pad = jnp.zeros((CHUNK, DN_HEAD_DIM), F32)
    for c in range(2):
        r0, r1 = c * CHUNK, (c + 1) * CHUNK
        v_blk = {}
        for h in heads:
            v_new = u[h][r0:r1] - _mm(w[h][r0:r1], s[h])
            v_blk[h] = _bf(jnp.concatenate([v_new, pad] if c == 0 else [pad, v_new], axis=0))
        for h in heads:
            outs[h].append(_mm(qg[h][r0:r1], s[h]) + _mm(attn[h][r0:r1, :], v_blk[h]))
            s[h] = s[h] * jnp.exp(gl[h][c]) + _mm(kdt[h], v_blk[h])
    for h in heads:
        state_scr[h] = s[h]
        lanes = slice(h * DN_HEAD_DIM, (h + 1) * DN_HEAD_DIM)
        o = jnp.concatenate(outs[h], axis=0)
        o_ref[:, lanes] = _bf(_rms(o, nw_ref[...]) * _silu(gate_ref[:, lanes].astype(F32)))
    halo_scr[...] = qkv_ref[TBLK - 16:TBLK, :].astype(F32)[16 - HALO:16]


def _dn_mixer(big, small, conv_w, prow, norm_w, batch, seq):
    nd = seq // TBLK
    row_map = lambda b, d: (b * nd + d, 0)
    return pl.pallas_call(
        _dn_kernel,
        grid=(batch, nd),
        in_specs=[
            pl.BlockSpec((TBLK, 3 * MIX_W), lambda b, d: (b * nd + d, COL_DN_QKV // (3 * MIX_W))),
            pl.BlockSpec((TBLK, MIX_W), lambda b, d: (b * nd + d, COL_DN_GATE // MIX_W)),
            pl.BlockSpec((TBLK, LANES), row_map),
            pl.BlockSpec((CONV_WIDTH, 3 * MIX_W), lambda b, d: (0, 0)),
            pl.BlockSpec((8, LANES), lambda b, d: (0, 0)),
            pl.BlockSpec((1, DN_HEAD_DIM), lambda b, d: (0, 0)),
        ],
        out_specs=pl.BlockSpec((TBLK, MIX_W), row_map),
        out_shape=jax.ShapeDtypeStruct((batch * seq, MIX_W), BF16),
        scratch_shapes=[pltpu.VMEM((HALO, 3 * MIX_W), F32), pltpu.VMEM((DN_HEADS, DN_HEAD_DIM, DN_HEAD_DIM), F32)],
        compiler_params=_cparams("parallel", "arbitrary"),
        name="deltanet",
    )(big, big, small, conv_w, prow, norm_w)


SB_TILE = 128
SB_TQ = 256
SB_SUB = SB_TQ // SB_TILE
SB_STRIP = 128
SB_HEADS_PER_STEP = 4
LOG2E = 1.4426950408889634
SB_LAG = 4


def _sb_kernel(q_ref, k_ref, v_ref, w_ref, o_ref, r_scr, acc_scr, za_scr, zb_scr, wgt_scr):
    qi = pl.program_id(2)
    row = lax.broadcasted_iota(jnp.int32, (SB_STRIP, SB_TILE), 0)
    col = lax.broadcasted_iota(jnp.int32, (SB_STRIP, SB_TILE), 1)
    wmat = w_ref[...]
    r_scr[...] = jnp.zeros_like(r_scr)
    acc_scr[...] = jnp.zeros_like(acc_scr)
    strips = range(SB_TQ // SB_STRIP)
    heads = range(SB_HEADS_PER_STEP)
    pairs = range(SB_HEADS_PER_STEP // 2)
    order = list(reversed(range(SB_SUB)))
    qh = {}
    for s in strips:
        for h in heads:
            lanes = slice((h // 2) * LANES, (h // 2 + 1) * LANES)
            q = q_ref[s * SB_STRIP:(s + 1) * SB_STRIP, lanes].astype(F32) * (SB_HEAD_DIM ** -0.5)
            qh[s, h] = _bf(jnp.where((col >= SB_HEAD_DIM) == (h % 2 == 1), q, 0.0))

    def chain_id(t, s, h):
        return (t * len(strips) + s) * SB_HEADS_PER_STEP + h

    def masked_out(t, s, diagonal):
        return diagonal and t * SB_TILE >= (s + 1) * SB_STRIP

    def block_rows(kb_lo):
        return pl.ds(pl.multiple_of(kb_lo * SB_TILE, SB_TILE), SB_SUB * SB_TILE)

    def score_ops(kb_lo, z_dst, diagonal):
        def one(t, s, h):
            def emit():
                rows = pl.ds(pl.multiple_of((kb_lo + t) * SB_TILE, SB_TILE), SB_TILE)
                kblk = k_ref[rows, (h // 2) * LANES:(h // 2 + 1) * LANES]
                z_dst[chain_id(t, s, h)] = _mm_nt(qh[s, h], kblk) * LOG2E
            return emit
        return [one(t, s, h) for t in order for s in strips for h in heads if not masked_out(t, s, diagonal)]

    def value_ops(kb_lo):
        def one(h):
            def emit():
                vblk = v_ref[block_rows(kb_lo), (h // 2) * LANES:(h // 2 + 1) * LANES]
                acc_scr[h] += jnp.dot(wgt_scr[h], vblk, preferred_element_type=F32)
            return emit
        return [one(h) for h in heads]

    def weight_stage(z_src, diagonal, extras):
        chains = [(t, s, h) for t in order for s in strips for h in heads]
        cs, tri = {}, {}

        def cumsum(c):
            t, s, _ = c
            z = z_src[chain_id(*c)]
            neg_abs = pltpu.bitcast(pltpu.bitcast(z, jnp.uint32) | jnp.uint32(0x80000000), F32)
            sp = jnp.maximum(z, 0.0) + jnp.log2(1.0 + jnp.exp2(neg_abs))
            if diagonal:
                tri[c] = col + t * SB_TILE < row + s * SB_STRIP
                sp = jnp.where(tri[c], sp, 0.0)
            hi, lo = _split2(sp)
            cs[c] = jnp.dot(jnp.concatenate([hi, lo], axis=1), wmat, preferred_element_type=F32)

        def weigh(c):
            t, s, h = c
            rows = slice(s * SB_STRIP, (s + 1) * SB_STRIP)
            lanes = slice(t * SB_TILE, (t + 1) * SB_TILE)
            wgt = jnp.exp2(z_src[chain_id(*c)] - cs[c][:, :SB_TILE] - r_scr[h, rows])
            wgt_scr[h, rows, lanes] = _bf(jnp.where(tri[c], wgt, 0.0) if diagonal else wgt)
            r_scr[h, rows] += cs[c][:, SB_TILE:]

        live = [c for c in chains if not masked_out(c[0], c[1], diagonal)]
        for t, s, h in chains:
            if masked_out(t, s, diagonal):
                wgt_scr[h, s * SB_STRIP:(s + 1) * SB_STRIP, t * SB_TILE:(t + 1) * SB_TILE] = (
                    jnp.zeros((SB_STRIP, SB_TILE), BF16))
        per_slot = -(-len(extras) // len(live))
        for i, c in enumerate(live):
            cumsum(c)
            for emit in extras[i * per_slot:(i + 1) * per_slot]:
                emit()
            if i >= SB_LAG:
                weigh(live[i - SB_LAG])
        for c in live[-SB_LAG:]:
            weigh(c)

    def step(cur, z_src, z_dst, issue_next=True):
        extras = value_ops((cur + 1) * SB_SUB)
        if issue_next:
            extras = extras + score_ops(jnp.maximum(cur - 1, 0) * SB_SUB, z_dst, False)
        weight_stage(z_src, False, extras)

    for emit in score_ops(qi * SB_SUB, za_scr, True):
        emit()
    weight_stage(za_scr, True, score_ops(jnp.maximum(qi - 1, 0) * SB_SUB, zb_scr, False))

    def body(j, carry):
        step(qi - 1 - 2 * j, zb_scr, za_scr)
        step(qi - 2 - 2 * j, za_scr, zb_scr)
        return carry

    lax.fori_loop(0, qi // 2, body, 0)

    @pl.when(qi % 2 == 1)
    def _():
        step(0, zb_scr, za_scr, issue_next=False)

    for emit in value_ops(0):
        emit()
    lane = lax.broadcasted_iota(jnp.int32, (SB_TQ, SB_TILE), 1)
    for p in range(SB_HEADS_PER_STEP // 2):
        o_ref[:, p * LANES:(p + 1) * LANES] = _bf(jnp.where(lane < SB_HEAD_DIM, acc_scr[2 * p], acc_scr[2 * p + 1]))


def _sb_mixer(big, wmat, batch, seq):
    nq = seq // SB_TQ
    width = SB_HEADS_PER_STEP * SB_HEAD_DIM
    groups = MIX_W // width
    qoff = COL_SB_QKV // width
    return pl.pallas_call(
        _sb_kernel,
        grid=(batch, groups, nq),
        in_specs=[
            pl.BlockSpec((SB_TQ, width), lambda b, p, i: (b * nq + i, qoff + p)),
            pl.BlockSpec((seq, width), lambda b, p, i: (b, qoff + groups + p)),
            pl.BlockSpec((seq, width), lambda b, p, i: (b, qoff + 2 * groups + p)),
            pl.BlockSpec((2 * SB_TILE, 2 * SB_TILE), lambda b, p, i: (0, 0)),
        ],
        out_specs=pl.BlockSpec((SB_TQ, width), lambda b, p, i: (b * nq + i, p)),
        out_shape=jax.ShapeDtypeStruct((batch * seq, MIX_W), BF16),
        scratch_shapes=[pltpu.VMEM((SB_HEADS_PER_STEP, SB_TQ, SB_TILE), F32),
                        pltpu.VMEM((SB_HEADS_PER_STEP, SB_TQ, SB_TILE), F32),
                        pltpu.VMEM((SB_SUB * (SB_TQ // SB_STRIP) * SB_HEADS_PER_STEP, SB_STRIP, SB_TILE), F32),
                        pltpu.VMEM((SB_SUB * (SB_TQ // SB_STRIP) * SB_HEADS_PER_STEP, SB_STRIP, SB_TILE), F32),
                        pltpu.VMEM((SB_HEADS_PER_STEP, SB_TQ, SB_SUB * SB_TILE), BF16)],
        compiler_params=_cparams("parallel", "parallel", "arbitrary"),
        name="stickbreak",
    )(big, big, big, wmat)


GROUP_W = MIX_W // SSM_GROUPS
HEADS_PER_GROUP = SSM_HEADS // SSM_GROUPS


def _ssd_kernel(z_ref, xbc_ref, sm_ref, cw_ref, cb_ref, prow_ref, drow_ref, nw_ref, e_ref, o_ref,
                halo_scr, state_scr):
    @pl.when(pl.program_id(1) == 0)
    def _():
        halo_scr[...] = jnp.zeros_like(halo_scr)
        state_scr[...] = jnp.zeros_like(state_scr)

    row, col, m_incl, _ = _chunk_masks()
    tril = jnp.where(m_incl, 1.0, 0.0)
    first = row[:, 0:1] < CHUNK
    gcol = lax.broadcasted_iota(jnp.int32, (TBLK, GROUP_W), 1)

    cur = xbc_ref[...].astype(F32)
    xbc = _silu(_causal_conv(halo_scr[...], cur, cw_ref[...], cb_ref[...]))
    halo_scr[...] = cur[TBLK - HALO:TBLK]
    gn = SSM_GROUPS * SSM_STATE
    xs = xbc[:, :MIX_W]
    bm = xbc[:, MIX_W:MIX_W + gn]
    cm = xbc[:, MIX_W + gn:]

    dt_all = _softplus(sm_ref[...] + prow_ref[1:2, :])
    a_all = -jnp.exp(prow_ref[0:1, :]) * dt_all
    acum_all = _mm_sel_lhs(tril, a_all)
    acum_t = acum_all.T
    e = e_ref[...]
    dt_exp = _mm_sel_rhs(dt_all, e)
    acum_exp = _mm_sel_rhs(acum_all, e)
    xdt = xs * dt_exp

    for g in range(SSM_GROUPS):
        c_g = cm[:, g * SSM_STATE:(g + 1) * SSM_STATE]
        b_g = bm[:, g * SSM_STATE:(g + 1) * SSM_STATE]
        scores = _mm_nt(c_g, b_g)
        lo, hi = g * GROUP_W, (g + 1) * GROUP_W
        x_g = xdt[:, lo:hi]
        a_g = acum_exp[:, lo:hi]
        y_g = jnp.zeros((TBLK, GROUP_W), F32)
        for hh in range(HEADS_PER_GROUP):
            h = g * HEADS_PER_GROUP + hh
            a_col = acum_exp[:, h * SSM_HEAD_DIM:h * SSM_HEAD_DIM + 1]
            a_row = acum_t[LANE_SSM_DT + h:LANE_SSM_DT + h + 1, :]
            lmat = jnp.exp(jnp.where(m_incl, a_col - a_row, -jnp.inf))
            y_g = jnp.where(gcol // SSM_HEAD_DIM == hh, _mm(scores * lmat, x_g), y_g)

        b_t = b_g.T
        a_last0 = a_g[CHUNK - 1:CHUNK, :]
        a_last1 = a_g[TBLK - 1:TBLK, :]
        x_sc = x_g * jnp.exp(jnp.where(first, a_last0, a_last1) - a_g)
        st = state_scr[g]
        yoff = []
        for c, a_last in ((0, a_last0), (1, a_last1)):
            r0, r1 = c * CHUNK, (c + 1) * CHUNK
            yoff.append(_mm(c_g[r0:r1], st) * jnp.exp(a_g[r0:r1]))
            st = st * jnp.exp(a_last) + _mm(b_t, jnp.where(first == (c == 0), x_sc, 0.0))
        state_scr[g] = st
        y_g = y_g + jnp.concatenate(yoff, axis=0) + xs[:, lo:hi] * drow_ref[:, lo:hi]
        y_g = y_g * _silu(z_ref[:, lo:hi].astype(F32))
        o_ref[:, lo:hi] = _bf(_rms(y_g, nw_ref[:, lo:hi]))


def _ssd_mixer(big, small, conv_w, conv_b, prow, drow, norm_w, emat, batch, seq):
    nd = seq // TBLK
    conv_dim = MIX_W + 2 * SSM_GROUPS * SSM_STATE
    row_map = lambda b, d: (b * nd + d, 0)
    const = lambda b, d: (0, 0)
    return pl.pallas_call(
        _ssd_kernel,
        grid=(batch, nd),
        in_specs=[
            pl.BlockSpec((TBLK, MIX_W), lambda b, d: (b * nd + d, COL_SSM_Z // MIX_W)),
            pl.BlockSpec((TBLK, conv_dim), lambda b, d: (b * nd + d, COL_SSM_XBC // conv_dim)),
            pl.BlockSpec((TBLK, LANES), row_map),
            pl.BlockSpec((CONV_WIDTH, conv_dim), const),
            pl.BlockSpec((1, conv_dim), const),
            pl.BlockSpec((8, LANES), const),
            pl.BlockSpec((1, MIX_W), const),
            pl.BlockSpec((1, MIX_W), const),
            pl.BlockSpec((LANES, MIX_W), const),
        ],
        out_specs=pl.BlockSpec((TBLK, MIX_W), row_map),
        out_shape=jax.ShapeDtypeStruct((batch * seq, MIX_W), BF16),
        scratch_shapes=[pltpu.VMEM((HALO, conv_dim), F32), pltpu.VMEM((SSM_GROUPS, SSM_STATE, GROUP_W), F32)],
        compiler_params=_cparams("parallel", "arbitrary"),
        name="ssd",
    )(big, big, small, conv_w, conv_b, prow, drow, norm_w, emat)


def _merge_kernel(x_ref, odn_ref, osb_ref, ossm_ref, g0_ref, g1_ref, g2_ref, wb_ref, wo_ref, o_ref):
    m = _sigmoid(g0_ref[...].astype(F32)) * jnp.dot(odn_ref[...], wb_ref[0], preferred_element_type=F32)
    m = m + _sigmoid(g1_ref[...].astype(F32)) * jnp.dot(osb_ref[...], wb_ref[1], preferred_element_type=F32)
    m = m + _sigmoid(g2_ref[...].astype(F32)) * jnp.dot(ossm_ref[...], wb_ref[2], preferred_element_type=F32)
    o_ref[...] = x_ref[...] + jnp.dot(_bf(m), wo_ref[...], preferred_element_type=F32)


def _merge(x, o_dn, o_sb, o_ssm, big, w_branch, w_out, layer, tm=512):
    t = x.shape[0]
    tm = min(tm, t)
    rows = lambda i: (i, 0)
    gcol = COL_GATES // D_MODEL
    return pl.pallas_call(
        _merge_kernel,
        grid=(t // tm,),
        in_specs=[
            pl.BlockSpec((tm, D_MODEL), rows),
            pl.BlockSpec((tm, MIX_W), rows),
            pl.BlockSpec((tm, MIX_W), rows),
            pl.BlockSpec((tm, MIX_W), rows),
            pl.BlockSpec((tm, D_MODEL), lambda i: (i, gcol)),
            pl.BlockSpec((tm, D_MODEL), lambda i: (i, gcol + 1)),
            pl.BlockSpec((tm, D_MODEL), lambda i: (i, gcol + 2)),
            pl.BlockSpec((None, 3, MIX_W, D_MODEL), lambda i: (layer, 0, 0, 0)),
            pl.BlockSpec((None, D_MODEL, D_MODEL), lambda i: (layer, 0, 0)),
        ],
        out_specs=pl.BlockSpec((tm, D_MODEL), rows),
        out_shape=jax.ShapeDtypeStruct((t, D_MODEL), F32),
        compiler_params=_cparams("parallel"),
        name="merge",
    )(x, o_dn, o_sb, o_ssm, big, big, big, w_branch, w_out)


def _mlp_kernel(x_ref, g_ref, wu_ref, wd_ref, gf_ref, o_ref, xn_scr, acc_scr, *, final_norm):
    j = pl.program_id(1)

    @pl.when(j == 0)
    def _():
        xn_scr[...] = _bf(_rms(x_ref[...], g_ref[...]))
        acc_scr[...] = jnp.zeros_like(acc_scr)

    h = jnp.dot(xn_scr[...], wu_ref[...], preferred_element_type=F32)
    h = jnp.square(jnp.maximum(h, 0.0))
    acc_scr[...] += jnp.dot(_bf(h), wd_ref[...], preferred_element_type=F32)

    @pl.when(j == pl.num_programs(1) - 1)
    def _():
        y = x_ref[...] + acc_scr[...]
        o_ref[...] = _rms(y, gf_ref[...]) if final_norm else y


def _mlp(x, g, w_up, w_down, g_final, layer, final_norm, tm=1024, tf=1024):
    t = x.shape[0]
    tm = min(tm, t)
    return pl.pallas_call(
        functools.partial(_mlp_kernel, final_norm=final_norm),
        grid=(t // tm, D_FF // tf),
        in_specs=[
            pl.BlockSpec((tm, D_MODEL), lambda i, j: (i, 0)),
            pl.BlockSpec((1, D_MODEL), lambda i, j: (0, 0)),
            pl.BlockSpec((None, D_MODEL, tf), lambda i, j: (layer, 0, j)),
            pl.BlockSpec((None, tf, D_MODEL), lambda i, j: (layer, j, 0)),
            pl.BlockSpec((1, D_MODEL), lambda i, j: (0, 0)),
        ],
        out_specs=pl.BlockSpec((tm, D_MODEL), lambda i, j: (i, 0)),
        out_shape=jax.ShapeDtypeStruct((t, D_MODEL), F32),
        scratch_shapes=[pltpu.VMEM((tm, D_MODEL), BF16), pltpu.VMEM((tm, D_MODEL), F32)],
        compiler_params=_cparams("parallel", "arbitrary"),
        name="mlp",
    )(x, g, w_up, w_down, g_final)


def _lane_row(values, lane0):
    return jnp.zeros((LANES,), F32).at[lane0:lane0 + values.shape[0]].set(values.astype(F32))


def _param_rows(a_log, dt_bias, lane0):
    rows = jnp.zeros((8, LANES), F32)
    return rows.at[0].set(_lane_row(a_log, lane0)).at[1].set(_lane_row(dt_bias, lane0))


def _sb_cumsum_matrix():
    j = jnp.arange(SB_TILE)[:, None]
    s = jnp.arange(SB_TILE)[None, :]
    half = jnp.concatenate([(j >= s).astype(BF16), jnp.ones((SB_TILE, SB_TILE), BF16)], axis=1)
    return jnp.concatenate([half, half], axis=0)


def _ssm_expand_matrix():
    lane = jnp.arange(LANES)[:, None]
    ch = jnp.arange(MIX_W)[None, :]
    return (lane == LANE_SSM_DT + ch // SSM_HEAD_DIM).astype(BF16)


def _split_w_in(w):
    big = jnp.concatenate([w[..., 0:4096], w[..., 4112:10256], w[..., 10272:13344]], axis=-1)
    pad = jnp.zeros(w.shape[:-1] + (LANES - 32,), w.dtype)
    small = jnp.concatenate([w[..., 4096:4112], w[..., 10256:10272], pad], axis=-1)
    return _bf(big), _bf(small)


def kernel(x, norm_mix, w_in, dn_conv_w, dn_a_log, dn_dt_bias, dn_norm_w, ssm_conv_w, ssm_conv_b, ssm_a_log,
           ssm_dt_bias, ssm_d, ssm_norm_w, w_branch, w_out, norm_mlp, w_up, w_down, norm_final):
    batch, seq, _ = x.shape
    depth = w_in.shape[0]
    h = x.reshape(batch * seq, D_MODEL)
    sb_w = _sb_cumsum_matrix()
    emat = _ssm_expand_matrix()
    g_final = norm_final.reshape(1, D_MODEL)
    w_big, w_small = _split_w_in(w_in)
    w_branch, w_out, w_up, w_down = _bf(w_branch), _bf(w_out), _bf(w_up), _bf(w_down)
    for l in range(depth):
        big, small = _inproj(h, norm_mix[l].reshape(1, D_MODEL), w_big, w_small, l)
        o_dn = _dn_mixer(big, small, dn_conv_w[l], _param_rows(dn_a_log[l], dn_dt_bias[l], LANE_DN_A),
                         dn_norm_w[l].reshape(1, DN_HEAD_DIM), batch, seq)
        o_sb = _sb_mixer(big, sb_w, batch, seq)
        o_ssm = _ssd_mixer(big, small, ssm_conv_w[l], ssm_conv_b[l].reshape(1, -1),
                           _param_rows(ssm_a_log[l], ssm_dt_bias[l], LANE_SSM_DT),
                           jnp.repeat(ssm_d[l], SSM_HEAD_DIM).reshape(1, MIX_W),
                           ssm_norm_w[l].reshape(1, MIX_W), emat, batch, seq)
        h = _merge(h, o_dn, o_sb, o_ssm, big, w_branch, w_out, l)
        h = _mlp(h, norm_mlp[l].reshape(1, D_MODEL), w_up, w_down, g_final, l, final_norm=(l == depth - 1))
    return h.reshape(batch, seq, D_MODEL)
```

```python
import functools

import jax
import jax.numpy as jnp
from jax import lax
from jax.experimental import pallas as pl
from jax.experimental.pallas import tpu as pltpu

F32 = jnp.float32
BF16 = jnp.bfloat16

D_MODEL = 1024
MIX_W = D_MODEL
DN_HEADS = 8
DN_HEAD_DIM = 128
SB_HEADS = 16
SB_HEAD_DIM = 64
SSM_HEADS = 16
SSM_HEAD_DIM = 64
SSM_STATE = 128
SSM_GROUPS = 4
CHUNK = 64
D_FF = 4 * D_MODEL
EPS = 1e-6
CONV_WIDTH = 4

LANES = 128
TBLK = 2 * CHUNK
HALO = 8
DN_BASE_BLOCK = 8
DN_TILES = 2

COL_DN_QKV = 0
COL_DN_GATE = 3072
COL_SB_QKV = 4096
COL_SSM_Z = 7168
COL_SSM_XBC = 8192
COL_GATES = 10240
N_BIG = 13312
LANE_DN_A = 0
LANE_DN_B = 8
LANE_SSM_DT = 16

VMEM_LIMIT = 56 * 1024 * 1024


def _cparams(*sem):
    return pltpu.CompilerParams(dimension_semantics=sem, vmem_limit_bytes=VMEM_LIMIT)


def _bf(x):
    return x.astype(BF16)


def _mm(a, b):
    return jnp.dot(_bf(a), _bf(b), preferred_element_type=F32)


def _mm_nt(a, b):
    return lax.dot_general(_bf(a), _bf(b), (((1,), (1,)), ((), ())), preferred_element_type=F32)


def _split2(a):
    hi = _bf(a)
    lo = _bf(a - hi.astype(F32))
    return hi, lo


def _split3(a):
    hi = _bf(a)
    r = a - hi.astype(F32)
    mid = _bf(r)
    lo = _bf(r - mid.astype(F32))
    return hi, mid, lo


def _mm_x3s(a_split, b_split):
    ah, al = a_split
    bh, bl = b_split
    lhs = jnp.concatenate([ah, al, ah], axis=1)
    rhs = jnp.concatenate([bh, bh, bl], axis=0)
    return jnp.dot(lhs, rhs, preferred_element_type=F32)


def _mm_sel_lhs(sel, b):
    s = _bf(sel)
    bh, bm, bl = _split3(b)
    return jnp.dot(jnp.concatenate([s, s, s], axis=1), jnp.concatenate([bh, bm, bl], axis=0),
                   preferred_element_type=F32)


def _mm_sel_rhs(a, sel_bf):
    ah, am, al = _split3(a)
    return jnp.dot(jnp.concatenate([ah, am, al], axis=1), jnp.concatenate([sel_bf, sel_bf, sel_bf], axis=0),
                   preferred_element_type=F32)


def _softplus(x):
    return jnp.maximum(x, 0.0) + jnp.log(1.0 + jnp.exp(-jnp.abs(x)))


def _sigmoid(x):
    return 1.0 / (1.0 + jnp.exp(-x))


def _silu(x):
    return x * _sigmoid(x)


def _rms(x, w):
    return x * lax.rsqrt(jnp.mean(x * x, axis=-1, keepdims=True) + EPS) * w


def _causal_conv(halo, cur, w, bias):
    rows = cur.shape[0]
    xx = jnp.concatenate([halo, cur], axis=0)
    y = w[CONV_WIDTH - 1:CONV_WIDTH, :] * cur
    for k in range(CONV_WIDTH - 1):
        shifted = pltpu.roll(xx, CONV_WIDTH - 1 - k, 0)[HALO:HALO + rows]
        y = y + w[k:k + 1, :] * shifted
    if bias is not None:
        y = y + bias
    return y


def _chunk_masks():
    row = lax.broadcasted_iota(jnp.int32, (TBLK, TBLK), 0)
    col = lax.broadcasted_iota(jnp.int32, (TBLK, TBLK), 1)
    same = (row >= CHUNK) == (col >= CHUNK)
    return row, col, same & (col <= row), same & (col < row)


def _inproj_kernel(x_ref, g_ref, wb_ref, ws_ref, big_ref, small_ref, xn_scr):
    @pl.when(pl.program_id(1) == 0)
    def _():
        xn_scr[...] = _bf(_rms(x_ref[...], g_ref[...]))
        small_ref[...] = jnp.dot(xn_scr[...], ws_ref[...], preferred_element_type=F32)

    big_ref[...] = _bf(jnp.dot(xn_scr[...], wb_ref[...], preferred_element_type=F32))


def _inproj(x, g, w_big, w_small, layer, tm=2048, tn=1024):
    t = x.shape[0]
    tm = min(tm, t)
    return pl.pallas_call(
        _inproj_kernel,
        grid=(t // tm, N_BIG // tn),
        in_specs=[
            pl.BlockSpec((tm, D_MODEL), lambda i, j: (i, 0)),
            pl.BlockSpec((1, D_MODEL), lambda i, j: (0, 0)),
            pl.BlockSpec((None, D_MODEL, tn), lambda i, j: (layer, 0, j)),
            pl.BlockSpec((None, D_MODEL, LANES), lambda i, j: (layer, 0, 0)),
        ],
        out_specs=[
            pl.BlockSpec((tm, tn), lambda i, j: (i, j)),
            pl.BlockSpec((tm, LANES), lambda i, j: (i, 0)),
        ],
        out_shape=[jax.ShapeDtypeStruct((t, N_BIG), BF16), jax.ShapeDtypeStruct((t, LANES), F32)],
        scratch_shapes=[pltpu.VMEM((tm, D_MODEL), BF16)],
        compiler_params=_cparams("parallel", "arbitrary"),
        name="inproj",
    )(x, g, w_big, w_small)


def _dn_kernel(qkv_ref, gate_ref, sm_ref, cw_ref, prow_ref, nw_ref, o_ref, halo_scr, state_scr):
    @pl.when(pl.program_id(1) == 0)
    def _():
        halo_scr[...] = jnp.zeros_like(halo_scr)
        state_scr[...] = jnp.zeros_like(state_scr)

    row, col, m_incl, m_strict = _chunk_masks()
    tril = jnp.where(m_incl, 1.0, 0.0)
    eye = jnp.where(row == col, 1.0, 0.0)
    first = row[:, 0:1] < CHUNK

    units = [(j, h) for j in range(DN_TILES) for h in range(DN_HEADS)]
    heads = range(DN_HEADS)
    q, k, v, kb, vb, decay, gc_col, kk, qk = {}, {}, {}, {}, {}, {}, {}, {}, {}
    low, xs, tinv, cd = {}, {}, {}, {}
    u, w, attn, qg, kdt, gl = {}, {}, {}, {}, {}, {}
    pad = jnp.zeros((CHUNK, DN_HEAD_DIM), F32)

    def same_block(size):
        shift = size.bit_length() - 1
        return (row >> shift) == (col >> shift)

    def prepare(j):
        rows = slice(j * TBLK, (j + 1) * TBLK)
        sm = sm_ref[rows, :]
        g_all = -jnp.exp(prow_ref[0:1, :]) * _softplus(sm + prow_ref[1:2, :])
        beta_all = _sigmoid(sm)
        gc_all = _mm_sel_lhs(tril, g_all)
        gct_all = gc_all.T
        for h in heads:
            def conv_silu(seg):
                lanes = slice(seg * MIX_W + h * DN_HEAD_DIM, seg * MIX_W + (h + 1) * DN_HEAD_DIM)
                if j == 0:
                    halo = halo_scr[:, lanes]
                else:
                    halo = qkv_ref[j * TBLK - 16:j * TBLK, lanes].astype(F32)[16 - HALO:16]
                return _silu(_causal_conv(halo, qkv_ref[rows, lanes].astype(F32), cw_ref[:, lanes], None))

            un = (j, h)
            qh = conv_silu(0)
            kh = conv_silu(1)
            v[un] = conv_silu(2)
            q[un] = qh * lax.rsqrt(jnp.sum(qh * qh, axis=-1, keepdims=True) + EPS) * (DN_HEAD_DIM ** -0.5)
            k[un] = kh * lax.rsqrt(jnp.sum(kh * kh, axis=-1, keepdims=True) + EPS)
            gc_col[un] = gc_all[:, LANE_DN_A + h:LANE_DN_A + h + 1]
            beta = beta_all[:, LANE_DN_B + h:LANE_DN_B + h + 1]
            gc_row = gct_all[LANE_DN_A + h:LANE_DN_A + h + 1, :]
            decay[un] = jnp.exp(jnp.where(m_incl, gc_col[un] - gc_row, -jnp.inf))
            kb[un] = k[un] * beta
            vb[un] = v[un] * beta
    def gram(units):
        for un in units:
            kk[un] = _mm_nt(kb[un], k[un])
            qk[un] = _mm_nt(q[un], k[un])

    def solve(units):
        for un in units:
            low[un] = jnp.where(m_strict, kk[un] * decay[un], 0.0)
            x = -jnp.where(same_block(DN_BASE_BLOCK), low[un], 0.0)
            xs[un] = _split2(x)
            tinv[un] = eye + x
        for _ in range(2):
            for un in units:
                xs[un] = _split2(_mm_x3s(xs[un], xs[un]))
            for un in units:
                tinv[un] = tinv[un] + _mm_x3s(_split2(tinv[un]), xs[un])
        size = DN_BASE_BLOCK
        while size < CHUNK:
            off_diag = same_block(2 * size) & jnp.logical_not(same_block(size))
            for un in units:
                cd[un] = _mm(jnp.where(off_diag, low[un], 0.0), tinv[un])
            for un in units:
                tinv[un] = tinv[un] - _mm(tinv[un], cd[un])
            size *= 2
        for un in units:
            egc = jnp.exp(gc_col[un])
            u[un] = _mm(tinv[un], vb[un])
            w[un] = _mm(tinv[un], kb[un] * egc)
            attn[un] = _bf(jnp.where(m_incl, qk[un] * decay[un], 0.0))
            qg[un] = _bf(q[un] * egc)
            gl[un] = (gc_col[un][CHUNK - 1:CHUNK, :], gc_col[un][TBLK - 1:TBLK, :])
            kdt[un] = _bf((k[un] * jnp.exp(jnp.where(first, gl[un][0], gl[un][1]) - gc_col[un])).T)

    s = {h: state_scr[h] for h in heads}

    def scan(j):
        outs = {h: [] for h in heads}
        for c in range(2):
            r0, r1 = c * CHUNK, (c + 1) * CHUNK
            v_blk = {}
            for h in heads:
                un = (j, h)
                v_new = u[un][r0:r1] - _mm(w[un][r0:r1], s[h])
                v_blk[h] = _bf(jnp.concatenate([v_new, pad] if c == 0 else [pad, v_new], axis=0))
            for h in heads:
                un = (j, h)
                outs[h].append(_mm(qg[un][r0:r1], s[h]) + _mm(attn[un][r0:r1, :], v_blk[h]))
                s[h] = s[h] * jnp.exp(gl[un][c]) + _mm(kdt[un], v_blk[h])
        for h in heads:
            lanes = slice(h * DN_HEAD_DIM, (h + 1) * DN_HEAD_DIM)
            rows = slice(j * TBLK, (j + 1) * TBLK)
            o = jnp.concatenate(outs[h], axis=0)
            o_ref[rows, lanes] = _bf(_rms(o, nw_ref[...]) * _silu(gate_ref[rows, lanes].astype(F32)))

    for j in range(DN_TILES):
        prepare(j)
    gram(units)
    solve(units)
    for j in range(DN_TILES):
        scan(j)
    for h in heads:
        state_scr[h] = s[h]
    last = DN_TILES * TBLK
    halo_scr[...] = qkv_ref[last - 16:last, :].astype(F32)[16 - HALO:16]


def _dn_mixer(big, small, conv_w, prow, norm_w, batch, seq):
    rows = DN_TILES * TBLK
    nd = seq // rows
    row_map = lambda b, d: (b * nd + d, 0)
    return pl.pallas_call(
        _dn_kernel,
        grid=(batch, nd),
        in_specs=[
            pl.BlockSpec((rows, 3 * MIX_W), lambda b, d: (b * nd + d, COL_DN_QKV // (3 * MIX_W))),
            pl.BlockSpec((rows, MIX_W), lambda b, d: (b * nd + d, COL_DN_GATE // MIX_W)),
            pl.BlockSpec((rows, LANES), row_map),
            pl.BlockSpec((CONV_WIDTH, 3 * MIX_W), lambda b, d: (0, 0)),
            pl.BlockSpec((8, LANES), lambda b, d: (0, 0)),
            pl.BlockSpec((1, DN_HEAD_DIM), lambda b, d: (0, 0)),
        ],
        out_specs=pl.BlockSpec((rows, MIX_W), row_map),
        out_shape=jax.ShapeDtypeStruct((batch * seq, MIX_W), BF16),
        scratch_shapes=[pltpu.VMEM((HALO, 3 * MIX_W), F32), pltpu.VMEM((DN_HEADS, DN_HEAD_DIM, DN_HEAD_DIM), F32)],
        compiler_params=_cparams("parallel", "arbitrary"),
        name="deltanet",
    )(big, big, small, conv_w, prow, norm_w)


SB_TILE = 128
SB_TQ = 256
SB_SUB = SB_TQ // SB_TILE
SB_STRIP = 128
SB_HEADS_PER_STEP = 4
LOG2E = 1.4426950408889634
SB_LAG = 4


def _sb_kernel(q_ref, k_ref, v_ref, w_ref, o_ref, r_scr, acc_scr, za_scr, zb_scr, wgt_scr):
    qi = pl.program_id(2)
    row = lax.broadcasted_iota(jnp.int32, (SB_STRIP, SB_TILE), 0)
    col = lax.broadcasted_iota(jnp.int32, (SB_STRIP, SB_TILE), 1)
    wmat = w_ref[...]
    r_scr[...] = jnp.zeros_like(r_scr)
    acc_scr[...] = jnp.zeros_like(acc_scr)
    strips = range(SB_TQ // SB_STRIP)
    heads = range(SB_HEADS_PER_STEP)
    pairs = range(SB_HEADS_PER_STEP // 2)
    order = list(reversed(range(SB_SUB)))
    qh = {}
    for s in strips:
        for h in heads:
            lanes = slice((h // 2) * LANES, (h // 2 + 1) * LANES)
            q = q_ref[s * SB_STRIP:(s + 1) * SB_STRIP, lanes].astype(F32) * (SB_HEAD_DIM ** -0.5)
            qh[s, h] = _bf(jnp.where((col >= SB_HEAD_DIM) == (h % 2 == 1), q, 0.0))

    def chain_id(t, s, h):
        return (t * len(strips) + s) * SB_HEADS_PER_STEP + h

    def masked_out(t, s, diagonal):
        return diagonal and t * SB_TILE >= (s + 1) * SB_STRIP

    def block_rows(kb_lo):
        return pl.ds(pl.multiple_of(kb_lo * SB_TILE, SB_TILE), SB_SUB * SB_TILE)

    def score_ops(kb_lo, z_dst, diagonal):
        def one(t, s, h):
            def emit():
                rows = pl.ds(pl.multiple_of((kb_lo + t) * SB_TILE, SB_TILE), SB_TILE)
                kblk = k_ref[rows, (h // 2) * LANES:(h // 2 + 1) * LANES]
                z_dst[chain_id(t, s, h)] = _mm_nt(qh[s, h], kblk) * LOG2E
            return emit
        return [one(t, s, h) for t in order for s in strips for h in heads if not masked_out(t, s, diagonal)]

    def value_ops(kb_lo):
        def one(h):
            def emit():
                vblk = v_ref[block_rows(kb_lo), (h // 2) * LANES:(h // 2 + 1) * LANES]
                acc_scr[h] += jnp.dot(wgt_scr[h], vblk, preferred_element_type=F32)
            return emit
        return [one(h) for h in heads]

    def weight_stage(z_src, diagonal, extras):
        chains = [(t, s, h) for t in order for s in strips for h in heads]
        cs, tri = {}, {}

        def cumsum(c):
            t, s, _ = c
            z = z_src[chain_id(*c)]
            neg_abs = pltpu.bitcast(pltpu.bitcast(z, jnp.uint32) | jnp.uint32(0x80000000), F32)
            sp = jnp.maximum(z, 0.0) + jnp.log2(1.0 + jnp.exp2(neg_abs))
            if diagonal:
                tri[c] = col + t * SB_TILE < row + s * SB_STRIP
                sp = jnp.where(tri[c], sp, 0.0)
            hi, lo = _split2(sp)
            cs[c] = jnp.dot(jnp.concatenate([hi, lo], axis=1), wmat, preferred_element_type=F32)

        def weigh(c):
            t, s, h = c
            rows = slice(s * SB_STRIP, (s + 1) * SB_STRIP)
            lanes = slice(t * SB_TILE, (t + 1) * SB_TILE)
            wgt = jnp.exp2(z_src[chain_id(*c)] - cs[c][:, :SB_TILE] - r_scr[h, rows])
            wgt_scr[h, rows, lanes] = _bf(jnp.where(tri[c], wgt, 0.0) if diagonal else wgt)
            r_scr[h, rows] += cs[c][:, SB_TILE:]

        live = [c for c in chains if not masked_out(c[0], c[1], diagonal)]
        for t, s, h in chains:
            if masked_out(t, s, diagonal):
                wgt_scr[h, s * SB_STRIP:(s + 1) * SB_STRIP, t * SB_TILE:(t + 1) * SB_TILE] = (
                    jnp.zeros((SB_STRIP, SB_TILE), BF16))
        per_slot = -(-len(extras) // len(live))
        for i, c in enumerate(live):
            cumsum(c)
            for emit in extras[i * per_slot:(i + 1) * per_slot]:
                emit()
            if i >= SB_LAG:
                weigh(live[i - SB_LAG])
        for c in live[-SB_LAG:]:
            weigh(c)

    def step(cur, z_src, z_dst, issue_next=True):
        extras = value_ops((cur + 1) * SB_SUB)
        if issue_next:
            extras = extras + score_ops(jnp.maximum(cur - 1, 0) * SB_SUB, z_dst, False)
        weight_stage(z_src, False, extras)

    for emit in score_ops(qi * SB_SUB, za_scr, True):
        emit()
    weight_stage(za_scr, True, score_ops(jnp.maximum(qi - 1, 0) * SB_SUB, zb_scr, False))

    def body(j, carry):
        step(qi - 1 - 2 * j, zb_scr, za_scr)
        step(qi - 2 - 2 * j, za_scr, zb_scr)
        return carry

    lax.fori_loop(0, qi // 2, body, 0)

    @pl.when(qi % 2 == 1)
    def _():
        step(0, zb_scr, za_scr, issue_next=False)

    for emit in value_ops(0):
        emit()
    lane = lax.broadcasted_iota(jnp.int32, (SB_TQ, SB_TILE), 1)
    for p in range(SB_HEADS_PER_STEP // 2):
        o_ref[:, p * LANES:(p + 1) * LANES] = _bf(jnp.where(lane < SB_HEAD_DIM, acc_scr[2 * p], acc_scr[2 * p + 1]))


def _sb_mixer(big, wmat, batch, seq):
    nq = seq // SB_TQ
    width = SB_HEADS_PER_STEP * SB_HEAD_DIM
    groups = MIX_W // width
    qoff = COL_SB_QKV // width
    return pl.pallas_call(
        _sb_kernel,
        grid=(batch, groups, nq),
        in_specs=[
            pl.BlockSpec((SB_TQ, width), lambda b, p, i: (b * nq + i, qoff + p)),
            pl.BlockSpec((seq, width), lambda b, p, i: (b, qoff + groups + p)),
            pl.BlockSpec((seq, width), lambda b, p, i: (b, qoff + 2 * groups + p)),
            pl.BlockSpec((2 * SB_TILE, 2 * SB_TILE), lambda b, p, i: (0, 0)),
        ],
        out_specs=pl.BlockSpec((SB_TQ, width), lambda b, p, i: (b * nq + i, p)),
        out_shape=jax.ShapeDtypeStruct((batch * seq, MIX_W), BF16),
        scratch_shapes=[pltpu.VMEM((SB_HEADS_PER_STEP, SB_TQ, SB_TILE), F32),
                        pltpu.VMEM((SB_HEADS_PER_STEP, SB_TQ, SB_TILE), F32),
                        pltpu.VMEM((SB_SUB * (SB_TQ // SB_STRIP) * SB_HEADS_PER_STEP, SB_STRIP, SB_TILE), F32),
                        pltpu.VMEM((SB_SUB * (SB_TQ // SB_STRIP) * SB_HEADS_PER_STEP, SB_STRIP, SB_TILE), F32),
                        pltpu.VMEM((SB_HEADS_PER_STEP, SB_TQ, SB_SUB * SB_TILE), BF16)],
        compiler_params=_cparams("parallel", "parallel", "arbitrary"),
        name="stickbreak",
    )(big, big, big, wmat)


GROUP_W = MIX_W // SSM_GROUPS
HEADS_PER_GROUP = SSM_HEADS // SSM_GROUPS


def _ssd_kernel(z_ref, xbc_ref, sm_ref, cw_ref, cb_ref, prow_ref, drow_ref, nw_ref, e_ref, o_ref,
                halo_scr, state_scr):
    @pl.when(pl.program_id(1) == 0)
    def _():
        halo_scr[...] = jnp.zeros_like(halo_scr)
        state_scr[...] = jnp.zeros_like(state_scr)

    row, col, m_incl, _ = _chunk_masks()
    tril = jnp.where(m_incl, 1.0, 0.0)
    first = row[:, 0:1] < CHUNK
    gcol = lax.broadcasted_iota(jnp.int32, (TBLK, GROUP_W), 1)

    cur = xbc_ref[...].astype(F32)
    xbc = _silu(_causal_conv(halo_scr[...], cur, cw_ref[...], cb_ref[...]))
    halo_scr[...] = cur[TBLK - HALO:TBLK]
    gn = SSM_GROUPS * SSM_STATE
    xs = xbc[:, :MIX_W]
    bm = xbc[:, MIX_W:MIX_W + gn]
    cm = xbc[:, MIX_W + gn:]

    dt_all = _softplus(sm_ref[...] + prow_ref[1:2, :])
    a_all = -jnp.exp(prow_ref[0:1, :]) * dt_all
    acum_all = _mm_sel_lhs(tril, a_all)
    acum_t = acum_all.T
    e = e_ref[...]
    dt_exp = _mm_sel_rhs(dt_all, e)
    acum_exp = _mm_sel_rhs(acum_all, e)
    xdt = xs * dt_exp

    for g in range(SSM_GROUPS):
        c_g = cm[:, g * SSM_STATE:(g + 1) * SSM_STATE]
        b_g = bm[:, g * SSM_STATE:(g + 1) * SSM_STATE]
        scores = _mm_nt(c_g, b_g)
        lo, hi = g * GROUP_W, (g + 1) * GROUP_W
        x_g = xdt[:, lo:hi]
        a_g = acum_exp[:, lo:hi]
        y_g = jnp.zeros((TBLK, GROUP_W), F32)
        for hh in range(HEADS_PER_GROUP):
            h = g * HEADS_PER_GROUP + hh
            a_col = acum_exp[:, h * SSM_HEAD_DIM:h * SSM_HEAD_DIM + 1]
            a_row = acum_t[LANE_SSM_DT + h:LANE_SSM_DT + h + 1, :]
            lmat = jnp.exp(jnp.where(m_incl, a_col - a_row, -jnp.inf))
            y_g = jnp.where(gcol // SSM_HEAD_DIM == hh, _mm(scores * lmat, x_g), y_g)

        b_t = b_g.T
        a_last0 = a_g[CHUNK - 1:CHUNK, :]
        a_last1 = a_g[TBLK - 1:TBLK, :]
        x_sc = x_g * jnp.exp(jnp.where(first, a_last0, a_last1) - a_g)
        st = state_scr[g]
        yoff = []
        for c, a_last in ((0, a_last0), (1, a_last1)):
            r0, r1 = c * CHUNK, (c + 1) * CHUNK
            yoff.append(_mm(c_g[r0:r1], st) * jnp.exp(a_g[r0:r1]))
            st = st * jnp.exp(a_last) + _mm(b_t, jnp.where(first == (c == 0), x_sc, 0.0))
        state_scr[g] = st
        y_g = y_g + jnp.concatenate(yoff, axis=0) + xs[:, lo:hi] * drow_ref[:, lo:hi]
        y_g = y_g * _silu(z_ref[:, lo:hi].astype(F32))
        o_ref[:, lo:hi] = _bf(_rms(y_g, nw_ref[:, lo:hi]))


def _ssd_mixer(big, small, conv_w, conv_b, prow, drow, norm_w, emat, batch, seq):
    nd = seq // TBLK
    conv_dim = MIX_W + 2 * SSM_GROUPS * SSM_STATE
    row_map = lambda b, d: (b * nd + d, 0)
    const = lambda b, d: (0, 0)
    return pl.pallas_call(
        _ssd_kernel,
        grid=(batch, nd),
        in_specs=[
            pl.BlockSpec((TBLK, MIX_W), lambda b, d: (b * nd + d, COL_SSM_Z // MIX_W)),
            pl.BlockSpec((TBLK, conv_dim), lambda b, d: (b * nd + d, COL_SSM_XBC // conv_dim)),
            pl.BlockSpec((TBLK, LANES), row_map),
            pl.BlockSpec((CONV_WIDTH, conv_dim), const),
            pl.BlockSpec((1, conv_dim), const),
            pl.BlockSpec((8, LANES), const),
            pl.BlockSpec((1, MIX_W), const),
            pl.BlockSpec((1, MIX_W), const),
            pl.BlockSpec((LANES, MIX_W), const),
        ],
        out_specs=pl.BlockSpec((TBLK, MIX_W), row_map),
        out_shape=jax.ShapeDtypeStruct((batch * seq, MIX_W), BF16),
        scratch_shapes=[pltpu.VMEM((HALO, conv_dim), F32), pltpu.VMEM((SSM_GROUPS, SSM_STATE, GROUP_W), F32)],
        compiler_params=_cparams("parallel", "arbitrary"),
        name="ssd",
    )(big, big, small, conv_w, conv_b, prow, drow, norm_w, emat)


def _merge_kernel(x_ref, odn_ref, osb_ref, ossm_ref, g0_ref, g1_ref, g2_ref, wb_ref, wo_ref, o_ref):
    m = _sigmoid(g0_ref[...].astype(F32)) * jnp.dot(odn_ref[...], wb_ref[0], preferred_element_type=F32)
    m = m + _sigmoid(g1_ref[...].astype(F32)) * jnp.dot(osb_ref[...], wb_ref[1], preferred_element_type=F32)
    m = m + _sigmoid(g2_ref[...].astype(F32)) * jnp.dot(ossm_ref[...], wb_ref[2], preferred_element_type=F32)
    o_ref[...] = x_ref[...] + jnp.dot(_bf(m), wo_ref[...], preferred_element_type=F32)


def _merge(x, o_dn, o_sb, o_ssm, big, w_branch, w_out, layer, tm=512):
    t = x.shape[0]
    tm = min(tm, t)
    rows = lambda i: (i, 0)
    gcol = COL_GATES // D_MODEL
    return pl.pallas_call(
        _merge_kernel,
        grid=(t // tm,),
        in_specs=[
            pl.BlockSpec((tm, D_MODEL), rows),
            pl.BlockSpec((tm, MIX_W), rows),
            pl.BlockSpec((tm, MIX_W), rows),
            pl.BlockSpec((tm, MIX_W), rows),
            pl.BlockSpec((tm, D_MODEL), lambda i: (i, gcol)),
            pl.BlockSpec((tm, D_MODEL), lambda i: (i, gcol + 1)),
            pl.BlockSpec((tm, D_MODEL), lambda i: (i, gcol + 2)),
            pl.BlockSpec((None, 3, MIX_W, D_MODEL), lambda i: (layer, 0, 0, 0)),
            pl.BlockSpec((None, D_MODEL, D_MODEL), lambda i: (layer, 0, 0)),
        ],
        out_specs=pl.BlockSpec((tm, D_MODEL), rows),
        out_shape=jax.ShapeDtypeStruct((t, D_MODEL), F32),
        compiler_params=_cparams("parallel"),
        name="merge",
    )(x, o_dn, o_sb, o_ssm, big, big, big, w_branch, w_out)


def _mlp_kernel(x_ref, g_ref, wu_ref, wd_ref, gf_ref, o_ref, xn_scr, acc_scr, *, final_norm):
    j = pl.program_id(1)

    @pl.when(j == 0)
    def _():
        xn_scr[...] = _bf(_rms(x_ref[...], g_ref[...]))
        acc_scr[...] = jnp.zeros_like(acc_scr)

    h = jnp.dot(xn_scr[...], wu_ref[...], preferred_element_type=F32)
    h = jnp.square(jnp.maximum(h, 0.0))
    acc_scr[...] += jnp.dot(_bf(h), wd_ref[...], preferred_element_type=F32)

    @pl.when(j == pl.num_programs(1) - 1)
    def _():
        y = x_ref[...] + acc_scr[...]
        o_ref[...] = _rms(y, gf_ref[...]) if final_norm else y


def _mlp(x, g, w_up, w_down, g_final, layer, final_norm, tm=1024, tf=1024):
    t = x.shape[0]
    tm = min(tm, t)
    return pl.pallas_call(
        functools.partial(_mlp_kernel, final_norm=final_norm),
        grid=(t // tm, D_FF // tf),
        in_specs=[
            pl.BlockSpec((tm, D_MODEL), lambda i, j: (i, 0)),
            pl.BlockSpec((1, D_MODEL), lambda i, j: (0, 0)),
            pl.BlockSpec((None, D_MODEL, tf), lambda i, j: (layer, 0, j)),
            pl.BlockSpec((None, tf, D_MODEL), lambda i, j: (layer, j, 0)),
            pl.BlockSpec((1, D_MODEL), lambda i, j: (0, 0)),
        ],
        out_specs=pl.BlockSpec((tm, D_MODEL), lambda i, j: (i, 0)),
        out_shape=jax.ShapeDtypeStruct((t, D_MODEL), F32),
        scratch_shapes=[pltpu.VMEM((tm, D_MODEL), BF16), pltpu.VMEM((tm, D_MODEL), F32)],
        compiler_params=_cparams("parallel", "arbitrary"),
        name="mlp",
    )(x, g, w_up, w_down, g_final)


def _lane_row(values, lane0):
    return jnp.zeros((LANES,), F32).at[lane0:lane0 + values.shape[0]].set(values.astype(F32))


def _param_rows(a_log, dt_bias, lane0):
    rows = jnp.zeros((8, LANES), F32)
    return rows.at[0].set(_lane_row(a_log, lane0)).at[1].set(_lane_row(dt_bias, lane0))


def _sb_cumsum_matrix():
    j = jnp.arange(SB_TILE)[:, None]
    s = jnp.arange(SB_TILE)[None, :]
    half = jnp.concatenate([(j >= s).astype(BF16), jnp.ones((SB_TILE, SB_TILE), BF16)], axis=1)
    return jnp.concatenate([half, half], axis=0)


def _ssm_expand_matrix():
    lane = jnp.arange(LANES)[:, None]
    ch = jnp.arange(MIX_W)[None, :]
    return (lane == LANE_SSM_DT + ch // SSM_HEAD_DIM).astype(BF16)


def _split_w_in(w):
    big = jnp.concatenate([w[..., 0:4096], w[..., 4112:10256], w[..., 10272:13344]], axis=-1)
    pad = jnp.zeros(w.shape[:-1] + (LANES - 32,), w.dtype)
    small = jnp.concatenate([w[..., 4096:4112], w[..., 10256:10272], pad], axis=-1)
    return _bf(big), _bf(small)


def kernel(x, norm_mix, w_in, dn_conv_w, dn_a_log, dn_dt_bias, dn_norm_w, ssm_conv_w, ssm_conv_b, ssm_a_log,
           ssm_dt_bias, ssm_d, ssm_norm_w, w_branch, w_out, norm_mlp, w_up, w_down, norm_final):
    batch, seq, _ = x.shape
    depth = w_in.shape[0]
    h = x.reshape(batch * seq, D_MODEL)
    sb_w = _sb_cumsum_matrix()
    emat = _ssm_expand_matrix()
    g_final = norm_final.reshape(1, D_MODEL)
    w_big, w_small = _split_w_in(w_in)
    w_branch, w_out, w_up, w_down = _bf(w_branch), _bf(w_out), _bf(w_up), _bf(w_down)
    for l in range(depth):
        big, small = _inproj(h, norm_mix[l].reshape(1, D_MODEL), w_big, w_small, l)
        o_dn = _dn_mixer(big, small, dn_conv_w[l], _param_rows(dn_a_log[l], dn_dt_bias[l], LANE_DN_A),
                         dn_norm_w[l].reshape(1, DN_HEAD_DIM), batch, seq)
        o_sb = _sb_mixer(big, sb_w, batch, seq)
        o_ssm = _ssd_mixer(big, small, ssm_conv_w[l], ssm_conv_b[l].reshape(1, -1),
                           _param_rows(ssm_a_log[l], ssm_dt_bias[l], LANE_SSM_DT),
                           jnp.repeat(ssm_d[l], SSM_HEAD_DIM).reshape(1, MIX_W),
                           ssm_norm_w[l].reshape(1, MIX_W), emat, batch, seq)
        h = _merge(h, o_dn, o_sb, o_ssm, big, w_branch, w_out, l)
        h = _mlp(h, norm_mlp[l].reshape(1, D_MODEL), w_up, w_down, g_final, l, final_norm=(l == depth - 1))
    return h.reshape(batch, seq, D_MODEL)
```

```python
import functools

import jax
import jax.numpy as jnp
from jax import lax
from jax.experimental import pallas as pl
from jax.experimental.pallas import tpu as pltpu

F32 = jnp.float32
BF16 = jnp.bfloat16

D_MODEL = 1024
MIX_W = D_MODEL
DN_HEADS = 8
DN_HEAD_DIM = 128
SB_HEADS = 16
SB_HEAD_DIM = 64
SSM_HEADS = 16
SSM_HEAD_DIM = 64
SSM_STATE = 128
SSM_GROUPS = 4
CHUNK = 64
D_FF = 4 * D_MODEL
EPS = 1e-6
CONV_WIDTH = 4

LANES = 128
TBLK = 2 * CHUNK
HALO = 8
DN_BASE_BLOCK = 8
DN_TILES = 2

COL_DN_QKV = 0
COL_DN_GATE = 3072
COL_SB_QKV = 4096
COL_SSM_Z = 7168
COL_SSM_XBC = 8192
COL_GATES = 10240
N_BIG = 13312
LANE_DN_A = 0
LANE_DN_B = 8
LANE_SSM_DT = 16

VMEM_LIMIT = 56 * 1024 * 1024


def _cparams(*sem):
    return pltpu.CompilerParams(dimension_semantics=sem, vmem_limit_bytes=VMEM_LIMIT)


def _bf(x):
    return x.astype(BF16)


def _mm(a, b):
    return jnp.dot(_bf(a), _bf(b), preferred_element_type=F32)


def _mm_nt(a, b):
    return lax.dot_general(_bf(a), _bf(b), (((1,), (1,)), ((), ())), preferred_element_type=F32)


def _split2(a):
    hi = _bf(a)
    lo = _bf(a - hi.astype(F32))
    return hi, lo


def _split3(a):
    hi = _bf(a)
    r = a - hi.astype(F32)
    mid = _bf(r)
    lo = _bf(r - mid.astype(F32))
    return hi, mid, lo


def _mm_x3s(a_split, b_split):
    ah, al = a_split
    bh, bl = b_split
    lhs = jnp.concatenate([ah, al, ah], axis=1)
    rhs = jnp.concatenate([bh, bh, bl], axis=0)
    return jnp.dot(lhs, rhs, preferred_element_type=F32)


def _mm_sel_lhs(sel, b):
    s = _bf(sel)
    bh, bm, bl = _split3(b)
    return jnp.dot(jnp.concatenate([s, s, s], axis=1), jnp.concatenate([bh, bm, bl], axis=0),
                   preferred_element_type=F32)


def _mm_sel_rhs(a, sel_bf):
    ah, am, al = _split3(a)
    return jnp.dot(jnp.concatenate([ah, am, al], axis=1), jnp.concatenate([sel_bf, sel_bf, sel_bf], axis=0),
                   preferred_element_type=F32)


def _softplus(x):
    return jnp.maximum(x, 0.0) + jnp.log(1.0 + jnp.exp(-jnp.abs(x)))


def _sigmoid(x):
    return 1.0 / (1.0 + jnp.exp(-x))


def _silu(x):
    return x * _sigmoid(x)


def _rms(x, w):
    return x * lax.rsqrt(jnp.mean(x * x, axis=-1, keepdims=True) + EPS) * w


def _causal_conv(halo, cur, w, bias):
    rows = cur.shape[0]
    xx = jnp.concatenate([halo, cur], axis=0)
    y = w[CONV_WIDTH - 1:CONV_WIDTH, :] * cur
    for k in range(CONV_WIDTH - 1):
        shifted = pltpu.roll(xx, CONV_WIDTH - 1 - k, 0)[HALO:HALO + rows]
        y = y + w[k:k + 1, :] * shifted
    if bias is not None:
        y = y + bias
    return y


def _chunk_masks():
    row = lax.broadcasted_iota(jnp.int32, (TBLK, TBLK), 0)
    col = lax.broadcasted_iota(jnp.int32, (TBLK, TBLK), 1)
    same = (row >= CHUNK) == (col >= CHUNK)
    return row, col, same & (col <= row), same & (col < row)


def _inproj_kernel(x_ref, g_ref, wb_ref, ws_ref, big_ref, small_ref, xn_scr):
    @pl.when(pl.program_id(1) == 0)
    def _():
        xn_scr[...] = _bf(_rms(x_ref[...], g_ref[...]))
        small_ref[...] = jnp.dot(xn_scr[...], ws_ref[...], preferred_element_type=F32)

    big_ref[...] = _bf(jnp.dot(xn_scr[...], wb_ref[...], preferred_element_type=F32))


def _inproj(x, g, w_big, w_small, layer, tm=2048, tn=1024):
    t = x.shape[0]
    tm = min(tm, t)
    return pl.pallas_call(
        _inproj_kernel,
        grid=(t // tm, N_BIG // tn),
        in_specs=[
            pl.BlockSpec((tm, D_MODEL), lambda i, j: (i, 0)),
            pl.BlockSpec((1, D_MODEL), lambda i, j: (0, 0)),
            pl.BlockSpec((None, D_MODEL, tn), lambda i, j: (layer, 0, j)),
            pl.BlockSpec((None, D_MODEL, LANES), lambda i, j: (layer, 0, 0)),
        ],
        out_specs=[
            pl.BlockSpec((tm, tn), lambda i, j: (i, j)),
            pl.BlockSpec((tm, LANES), lambda i, j: (i, 0)),
        ],
        out_shape=[jax.ShapeDtypeStruct((t, N_BIG), BF16), jax.ShapeDtypeStruct((t, LANES), F32)],
        scratch_shapes=[pltpu.VMEM((tm, D_MODEL), BF16)],
        compiler_params=_cparams("parallel", "arbitrary"),
        name="inproj",
    )(x, g, w_big, w_small)


def _dn_kernel(qkv_ref, gate_ref, sm_ref, cw_ref, prow_ref, nw_ref, o_ref, halo_scr, state_scr):
    @pl.when(pl.program_id(1) == 0)
    def _():
        halo_scr[...] = jnp.zeros_like(halo_scr)
        state_scr[...] = jnp.zeros_like(state_scr)

    row, col, m_incl, m_strict = _chunk_masks()
    tril = jnp.where(m_incl, 1.0, 0.0)
    eye = jnp.where(row == col, 1.0, 0.0)
    first = row[:, 0:1] < CHUNK

    units = [(j, h) for j in range(DN_TILES) for h in range(DN_HEADS)]
    heads = range(DN_HEADS)
    q, k, v, kb, vb, decay, gc_col, kk, qk = {}, {}, {}, {}, {}, {}, {}, {}, {}
    low, xs, tinv, cd = {}, {}, {}, {}
    u, w, attn, qg, kdt, gl = {}, {}, {}, {}, {}, {}
    pad = jnp.zeros((CHUNK, DN_HEAD_DIM), F32)

    def same_block(size):
        shift = size.bit_length() - 1
        return (row >> shift) == (col >> shift)

    def prepare(j):
        rows = slice(j * TBLK, (j + 1) * TBLK)
        sm = sm_ref[rows, :]
        g_all = -jnp.exp(prow_ref[0:1, :]) * _softplus(sm + prow_ref[1:2, :])
        beta_all = _sigmoid(sm)
        gc_all = _mm_sel_lhs(tril, g_all)
        gct_all = gc_all.T
        for h in heads:
            def conv_silu(seg):
                lanes = slice(seg * MIX_W + h * DN_HEAD_DIM, seg * MIX_W + (h + 1) * DN_HEAD_DIM)
                if j == 0:
                    halo = halo_scr[:, lanes]
                else:
                    halo = qkv_ref[j * TBLK - 16:j * TBLK, lanes].astype(F32)[16 - HALO:16]
                return _silu(_causal_conv(halo, qkv_ref[rows, lanes].astype(F32), cw_ref[:, lanes], None))

            un = (j, h)
            qh = conv_silu(0)
            kh = conv_silu(1)
            v[un] = conv_silu(2)
            q[un] = qh * lax.rsqrt(jnp.sum(qh * qh, axis=-1, keepdims=True) + EPS) * (DN_HEAD_DIM ** -0.5)
            k[un] = kh * lax.rsqrt(jnp.sum(kh * kh, axis=-1, keepdims=True) + EPS)
            gc_col[un] = gc_all[:, LANE_DN_A + h:LANE_DN_A + h + 1]
            beta = beta_all[:, LANE_DN_B + h:LANE_DN_B + h + 1]
            gc_row = gct_all[LANE_DN_A + h:LANE_DN_A + h + 1, :]
            decay[un] = jnp.exp(jnp.where(m_incl, gc_col[un] - gc_row, -jnp.inf))
            kb[un] = k[un] * beta
            vb[un] = v[un] * beta
    def gram(units):
        for un in units:
            kk[un] = _mm_nt(kb[un], k[un])
            qk[un] = _mm_nt(q[un], k[un])

    def solve(units):
        for un in units:
            low[un] = jnp.where(m_strict, kk[un] * decay[un], 0.0)
            x = -jnp.where(same_block(DN_BASE_BLOCK), low[un], 0.0)
            xs[un] = _split2(x)
            tinv[un] = eye + x
        for _ in range(2):
            for un in units:
                xs[un] = _split2(_mm_x3s(xs[un], xs[un]))
            for un in units:
                tinv[un] = tinv[un] + _mm_x3s(_split2(tinv[un]), xs[un])
        size = DN_BASE_BLOCK
        while size < CHUNK:
            off_diag = same_block(2 * size) & jnp.logical_not(same_block(size))
            for un in units:
                cd[un] = _mm(jnp.where(off_diag, low[un], 0.0), tinv[un])
            for un in units:
                tinv[un] = tinv[un] - _mm(tinv[un], cd[un])
            size *= 2
        for un in units:
            egc = jnp.exp(gc_col[un])
            u[un] = _mm(tinv[un], vb[un])
            w[un] = _mm(tinv[un], kb[un] * egc)
            attn[un] = _bf(jnp.where(m_incl, qk[un] * decay[un], 0.0))
            qg[un] = _bf(q[un] * egc)
            gl[un] = (gc_col[un][CHUNK - 1:CHUNK, :], gc_col[un][TBLK - 1:TBLK, :])
            kdt[un] = _bf((k[un] * jnp.exp(jnp.where(first, gl[un][0], gl[un][1]) - gc_col[un])).T)

    s = {h: state_scr[h] for h in heads}

    def scan(j):
        outs = {h: [] for h in heads}
        for c in range(2):
            r0, r1 = c * CHUNK, (c + 1) * CHUNK
            v_blk = {}
            for h in heads:
                un = (j, h)
                v_new = u[un][r0:r1] - _mm(w[un][r0:r1], s[h])
                v_blk[h] = _bf(jnp.concatenate([v_new, pad] if c == 0 else [pad, v_new], axis=0))
            for h in heads:
                un = (j, h)
                outs[h].append(_mm(qg[un][r0:r1], s[h]) + _mm(attn[un][r0:r1, :], v_blk[h]))
                s[h] = s[h] * jnp.exp(gl[un][c]) + _mm(kdt[un], v_blk[h])
        for h in heads:
            lanes = slice(h * DN_HEAD_DIM, (h + 1) * DN_HEAD_DIM)
            rows = slice(j * TBLK, (j + 1) * TBLK)
            o = jnp.concatenate(outs[h], axis=0)
            o_ref[rows, lanes] = _bf(_rms(o, nw_ref[...]) * _silu(gate_ref[rows, lanes].astype(F32)))

    for j in range(DN_TILES):
        prepare(j)
    gram(units)
    solve(units)
    for j in range(DN_TILES):
        scan(j)
    for h in heads:
        state_scr[h] = s[h]
    last = DN_TILES * TBLK
    halo_scr[...] = qkv_ref[last - 16:last, :].astype(F32)[16 - HALO:16]


def _dn_mixer(big, small, conv_w, prow, norm_w, batch, seq):
    rows = DN_TILES * TBLK
    nd = seq // rows
    row_map = lambda b, d: (b * nd + d, 0)
    return pl.pallas_call(
        _dn_kernel,
        grid=(batch, nd),
        in_specs=[
            pl.BlockSpec((rows, 3 * MIX_W), lambda b, d: (b * nd + d, COL_DN_QKV // (3 * MIX_W))),
            pl.BlockSpec((rows, MIX_W), lambda b, d: (b * nd + d, COL_DN_GATE // MIX_W)),
            pl.BlockSpec((rows, LANES), row_map),
            pl.BlockSpec((CONV_WIDTH, 3 * MIX_W), lambda b, d: (0, 0)),
            pl.BlockSpec((8, LANES), lambda b, d: (0, 0)),
            pl.BlockSpec((1, DN_HEAD_DIM), lambda b, d: (0, 0)),
        ],
        out_specs=pl.BlockSpec((rows, MIX_W), row_map),
        out_shape=jax.ShapeDtypeStruct((batch * seq, MIX_W), BF16),
        scratch_shapes=[pltpu.VMEM((HALO, 3 * MIX_W), F32), pltpu.VMEM((DN_HEADS, DN_HEAD_DIM, DN_HEAD_DIM), F32)],
        compiler_params=_cparams("parallel", "arbitrary"),
        name="deltanet",
    )(big, big, small, conv_w, prow, norm_w)


SB_TILE = 128
SB_TQ = 256
SB_SUB = SB_TQ // SB_TILE
SB_STRIP = 128
SB_HEADS_PER_STEP = 4
LOG2E = 1.4426950408889634
SB_LAG = 4


def _sb_kernel(q_ref, k_ref, v_ref, w_ref, o_ref, r_scr, acc_scr, za_scr, zb_scr, wgt_scr):
    qi = pl.program_id(2)
    row = lax.broadcasted_iota(jnp.int32, (SB_STRIP, SB_TILE), 0)
    col = lax.broadcasted_iota(jnp.int32, (SB_STRIP, SB_TILE), 1)
    wmat = w_ref[...]
    r_scr[...] = jnp.zeros_like(r_scr)
    acc_scr[...] = jnp.zeros_like(acc_scr)
    strips = range(SB_TQ // SB_STRIP)
    heads = range(SB_HEADS_PER_STEP)
    pairs = range(SB_HEADS_PER_STEP // 2)
    order = list(reversed(range(SB_SUB)))
    qh = {}
    for s in strips:
        for h in heads:
            lanes = slice((h // 2) * LANES, (h // 2 + 1) * LANES)
            q = q_ref[s * SB_STRIP:(s + 1) * SB_STRIP, lanes].astype(F32) * (SB_HEAD_DIM ** -0.5)
            qh[s, h] = _bf(jnp.where((col >= SB_HEAD_DIM) == (h % 2 == 1), q, 0.0))

    def chain_id(t, s, h):
        return (t * len(strips) + s) * SB_HEADS_PER_STEP + h

    def masked_out(t, s, diagonal):
        return diagonal and t * SB_TILE >= (s + 1) * SB_STRIP

    def block_rows(kb_lo):
        return pl.ds(pl.multiple_of(kb_lo * SB_TILE, SB_TILE), SB_SUB * SB_TILE)

    def score_ops(kb_lo, z_dst, diagonal):
        def one(t, s, h):
            def emit():
                rows = pl.ds(pl.multiple_of((kb_lo + t) * SB_TILE, SB_TILE), SB_TILE)
                kblk = k_ref[rows, (h // 2) * LANES:(h // 2 + 1) * LANES]
                z_dst[chain_id(t, s, h)] = _mm_nt(qh[s, h], kblk) * LOG2E
            return emit
        return [one(t, s, h) for t in order for s in strips for h in heads if not masked_out(t, s, diagonal)]

    def value_ops(kb_lo):
        def one(h):
            def emit():
                vblk = v_ref[block_rows(kb_lo), (h // 2) * LANES:(h // 2 + 1) * LANES]
                acc_scr[h] += jnp.dot(wgt_scr[h], vblk, preferred_element_type=F32)
            return emit
        return [one(h) for h in heads]

    def weight_stage(z_src, diagonal, extras):
        chains = [(t, s, h) for t in order for s in strips for h in heads]
        cs, tri = {}, {}

        def cumsum(c):
            t, s, _ = c
            z = z_src[chain_id(*c)]
            neg_abs = pltpu.bitcast(pltpu.bitcast(z, jnp.uint32) | jnp.uint32(0x80000000), F32)
            sp = jnp.maximum(z, 0.0) + jnp.log2(1.0 + jnp.exp2(neg_abs))
            if diagonal:
                tri[c] = col + t * SB_TILE < row + s * SB_STRIP
                sp = jnp.where(tri[c], sp, 0.0)
            hi, lo = _split2(sp)
            cs[c] = jnp.dot(jnp.concatenate([hi, lo], axis=1), wmat, preferred_element_type=F32)

        def weigh(c):
            t, s, h = c
            rows = slice(s * SB_STRIP, (s + 1) * SB_STRIP)
            lanes = slice(t * SB_TILE, (t + 1) * SB_TILE)
            wgt = jnp.exp2(z_src[chain_id(*c)] - cs[c][:, :SB_TILE] - r_scr[h, rows])
            wgt_scr[h, rows, lanes] = _bf(jnp.where(tri[c], wgt, 0.0) if diagonal else wgt)
            r_scr[h, rows] += cs[c][:, SB_TILE:]

        live = [c for c in chains if not masked_out(c[0], c[1], diagonal)]
        for t, s, h in chains:
            if masked_out(t, s, diagonal):
                wgt_scr[h, s * SB_STRIP:(s + 1) * SB_STRIP, t * SB_TILE:(t + 1) * SB_TILE] = (
                    jnp.zeros((SB_STRIP, SB_TILE), BF16))
        per_slot = -(-len(extras) // len(live))
        for i, c in enumerate(live):
            cumsum(c)
            for emit in extras[i * per_slot:(i + 1) * per_slot]:
                emit()
            if i >= SB_LAG:
                weigh(live[i - SB_LAG])
        for c in live[-SB_LAG:]:
            weigh(c)

    def step(cur, z_src, z_dst, issue_next=True):
        extras = value_ops((cur + 1) * SB_SUB)
        if issue_next:
            extras = extras + score_ops(jnp.maximum(cur - 1, 0) * SB_SUB, z_dst, False)
        weight_stage(z_src, False, extras)

    for emit in score_ops(qi * SB_SUB, za_scr, True):
        emit()
    weight_stage(za_scr, True, score_ops(jnp.maximum(qi - 1, 0) * SB_SUB, zb_scr, False))

    def body(j, carry):
        step(qi - 1 - 2 * j, zb_scr, za_scr)
        step(qi - 2 - 2 * j, za_scr, zb_scr)
        return carry

    lax.fori_loop(0, qi // 2, body, 0)

    @pl.when(qi % 2 == 1)
    def _():
        step(0, zb_scr, za_scr, issue_next=False)

    for emit in value_ops(0):
        emit()
    lane = lax.broadcasted_iota(jnp.int32, (SB_TQ, SB_TILE), 1)
    for p in range(SB_HEADS_PER_STEP // 2):
        o_ref[:, p * LANES:(p + 1) * LANES] = _bf(jnp.where(lane < SB_HEAD_DIM, acc_scr[2 * p], acc_scr[2 * p + 1]))


def _sb_mixer(big, wmat, batch, seq):
    nq = seq // SB_TQ
    width = SB_HEADS_PER_STEP * SB_HEAD_DIM
    groups = MIX_W // width
    qoff = COL_SB_QKV // width
    return pl.pallas_call(
        _sb_kernel,
        grid=(batch, groups, nq),
        in_specs=[
            pl.BlockSpec((SB_TQ, width), lambda b, p, i: (b * nq + i, qoff + p)),
            pl.BlockSpec((seq, width), lambda b, p, i: (b, qoff + groups + p)),
            pl.BlockSpec((seq, width), lambda b, p, i: (b, qoff + 2 * groups + p)),
            pl.BlockSpec((2 * SB_TILE, 2 * SB_TILE), lambda b, p, i: (0, 0)),
        ],
        out_specs=pl.BlockSpec((SB_TQ, width), lambda b, p, i: (b * nq + i, p)),
        out_shape=jax.ShapeDtypeStruct((batch * seq, MIX_W), BF16),
        scratch_shapes=[pltpu.VMEM((SB_HEADS_PER_STEP, SB_TQ, SB_TILE), F32),
                        pltpu.VMEM((SB_HEADS_PER_STEP, SB_TQ, SB_TILE), F32),
                        pltpu.VMEM((SB_SUB * (SB_TQ // SB_STRIP) * SB_HEADS_PER_STEP, SB_STRIP, SB_TILE), F32),
                        pltpu.VMEM((SB_SUB * (SB_TQ // SB_STRIP) * SB_HEADS_PER_STEP, SB_STRIP, SB_TILE), F32),
                        pltpu.VMEM((SB_HEADS_PER_STEP, SB_TQ, SB_SUB * SB_TILE), BF16)],
        compiler_params=_cparams("parallel", "parallel", "arbitrary"),
        name="stickbreak",
    )(big, big, big, wmat)


SSD_TILES = 2
GROUP_W = MIX_W // SSM_GROUPS
HEADS_PER_GROUP = SSM_HEADS // SSM_GROUPS


def _ssd_kernel(z_ref, xbc_ref, sm_ref, cw_ref, cb_ref, prow_ref, drow_ref, nw_ref, e_ref, o_ref,
                halo_scr, state_scr):
    @pl.when(pl.program_id(1) == 0)
    def _():
        halo_scr[...] = jnp.zeros_like(halo_scr)
        state_scr[...] = jnp.zeros_like(state_scr)

    row, col, m_incl, _ = _chunk_masks()
    tril = jnp.where(m_incl, 1.0, 0.0)
    first = row[:, 0:1] < CHUNK
    gcol = lax.broadcasted_iota(jnp.int32, (TBLK, GROUP_W), 1)

    gn = SSM_GROUPS * SSM_STATE
    e = e_ref[...]
    for j in range(SSD_TILES):
        rows = slice(j * TBLK, (j + 1) * TBLK)
        cur = xbc_ref[rows, :].astype(F32)
        halo = halo_scr[...] if j == 0 else xbc_ref[j * TBLK - 16:j * TBLK, :].astype(F32)[16 - HALO:16]
        xbc = _silu(_causal_conv(halo, cur, cw_ref[...], cb_ref[...]))
        xs = xbc[:, :MIX_W]
        bm = xbc[:, MIX_W:MIX_W + gn]
        cm = xbc[:, MIX_W + gn:]

        dt_all = _softplus(sm_ref[rows, :] + prow_ref[1:2, :])
        a_all = -jnp.exp(prow_ref[0:1, :]) * dt_all
        acum_all = _mm_sel_lhs(tril, a_all)
        acum_t = acum_all.T
        dt_exp = _mm_sel_rhs(dt_all, e)
        acum_exp = _mm_sel_rhs(acum_all, e)
        xdt = xs * dt_exp

        for g in range(SSM_GROUPS):
            c_g = cm[:, g * SSM_STATE:(g + 1) * SSM_STATE]
            b_g = bm[:, g * SSM_STATE:(g + 1) * SSM_STATE]
            scores = _mm_nt(c_g, b_g)
            lo, hi = g * GROUP_W, (g + 1) * GROUP_W
            x_g = xdt[:, lo:hi]
            a_g = acum_exp[:, lo:hi]
            y_g = jnp.zeros((TBLK, GROUP_W), F32)
            for hh in range(HEADS_PER_GROUP):
                h = g * HEADS_PER_GROUP + hh
                a_col = acum_exp[:, h * SSM_HEAD_DIM:h * SSM_HEAD_DIM + 1]
                a_row = acum_t[LANE_SSM_DT + h:LANE_SSM_DT + h + 1, :]
                lmat = jnp.exp(jnp.where(m_incl, a_col - a_row, -jnp.inf))
                y_g = jnp.where(gcol // SSM_HEAD_DIM == hh, _mm(scores * lmat, x_g), y_g)

            b_t = b_g.T
            a_last0 = a_g[CHUNK - 1:CHUNK, :]
            a_last1 = a_g[TBLK - 1:TBLK, :]
            x_sc = x_g * jnp.exp(jnp.where(first, a_last0, a_last1) - a_g)
            st = state_scr[g]
            yoff = []
            for c, a_last in ((0, a_last0), (1, a_last1)):
                r0, r1 = c * CHUNK, (c + 1) * CHUNK
                yoff.append(_mm(c_g[r0:r1], st) * jnp.exp(a_g[r0:r1]))
                st = st * jnp.exp(a_last) + _mm(b_t, jnp.where(first == (c == 0), x_sc, 0.0))
            state_scr[g] = st
            y_g = y_g + jnp.concatenate(yoff, axis=0) + xs[:, lo:hi] * drow_ref[:, lo:hi]
            y_g = y_g * _silu(z_ref[rows, lo:hi].astype(F32))
            o_ref[rows, lo:hi] = _bf(_rms(y_g, nw_ref[:, lo:hi]))
    last = SSD_TILES * TBLK
    halo_scr[...] = xbc_ref[last - 16:last, :].astype(F32)[16 - HALO:16]


def _ssd_mixer(big, small, conv_w, conv_b, prow, drow, norm_w, emat, batch, seq):
    rows = SSD_TILES * TBLK
    nd = seq // rows
    conv_dim = MIX_W + 2 * SSM_GROUPS * SSM_STATE
    row_map = lambda b, d: (b * nd + d, 0)
    const = lambda b, d: (0, 0)
    return pl.pallas_call(
        _ssd_kernel,
        grid=(batch, nd),
        in_specs=[
            pl.BlockSpec((rows, MIX_W), lambda b, d: (b * nd + d, COL_SSM_Z // MIX_W)),
            pl.BlockSpec((rows, conv_dim), lambda b, d: (b * nd + d, COL_SSM_XBC // conv_dim)),
            pl.BlockSpec((rows, LANES), row_map),
            pl.BlockSpec((CONV_WIDTH, conv_dim), const),
            pl.BlockSpec((1, conv_dim), const),
            pl.BlockSpec((8, LANES), const),
            pl.BlockSpec((1, MIX_W), const),
            pl.BlockSpec((1, MIX_W), const),
            pl.BlockSpec((LANES, MIX_W), const),
        ],
        out_specs=pl.BlockSpec((rows, MIX_W), row_map),
        out_shape=jax.ShapeDtypeStruct((batch * seq, MIX_W), BF16),
        scratch_shapes=[pltpu.VMEM((HALO, conv_dim), F32), pltpu.VMEM((SSM_GROUPS, SSM_STATE, GROUP_W), F32)],
        compiler_params=_cparams("parallel", "arbitrary"),
        name="ssd",
    )(big, big, small, conv_w, conv_b, prow, drow, norm_w, emat)


def _merge_kernel(x_ref, odn_ref, osb_ref, ossm_ref, g0_ref, g1_ref, g2_ref, wb_ref, wo_ref, o_ref):
    m = _sigmoid(g0_ref[...].astype(F32)) * jnp.dot(odn_ref[...], wb_ref[0], preferred_element_type=F32)
    m = m + _sigmoid(g1_ref[...].astype(F32)) * jnp.dot(osb_ref[...], wb_ref[1], preferred_element_type=F32)
    m = m + _sigmoid(g2_ref[...].astype(F32)) * jnp.dot(ossm_ref[...], wb_ref[2], preferred_element_type=F32)
    o_ref[...] = x_ref[...] + jnp.dot(_bf(m), wo_ref[...], preferred_element_type=F32)


def _merge(x, o_dn, o_sb, o_ssm, big, w_branch, w_out, layer, tm=512):
    t = x.shape[0]
    tm = min(tm, t)
    rows = lambda i: (i, 0)
    gcol = COL_GATES // D_MODEL
    return pl.pallas_call(
        _merge_kernel,
        grid=(t // tm,),
        in_specs=[
            pl.BlockSpec((tm, D_MODEL), rows),
            pl.BlockSpec((tm, MIX_W), rows),
            pl.BlockSpec((tm, MIX_W), rows),
            pl.BlockSpec((tm, MIX_W), rows),
            pl.BlockSpec((tm, D_MODEL), lambda i: (i, gcol)),
            pl.BlockSpec((tm, D_MODEL), lambda i: (i, gcol + 1)),
            pl.BlockSpec((tm, D_MODEL), lambda i: (i, gcol + 2)),
            pl.BlockSpec((None, 3, MIX_W, D_MODEL), lambda i: (layer, 0, 0, 0)),
            pl.BlockSpec((None, D_MODEL, D_MODEL), lambda i: (layer, 0, 0)),
        ],
        out_specs=pl.BlockSpec((tm, D_MODEL), rows),
        out_shape=jax.ShapeDtypeStruct((t, D_MODEL), F32),
        compiler_params=_cparams("parallel"),
        name="merge",
    )(x, o_dn, o_sb, o_ssm, big, big, big, w_branch, w_out)


def _mlp_kernel(x_ref, g_ref, wu_ref, wd_ref, gf_ref, o_ref, xn_scr, acc_scr, *, final_norm):
    j = pl.program_id(1)

    @pl.when(j == 0)
    def _():
        xn_scr[...] = _bf(_rms(x_ref[...], g_ref[...]))
        acc_scr[...] = jnp.zeros_like(acc_scr)

    h = jnp.dot(xn_scr[...], wu_ref[...], preferred_element_type=F32)
    h = jnp.square(jnp.maximum(h, 0.0))
    acc_scr[...] += jnp.dot(_bf(h), wd_ref[...], preferred_element_type=F32)

    @pl.when(j == pl.num_programs(1) - 1)
    def _():
        y = x_ref[...] + acc_scr[...]
        o_ref[...] = _rms(y, gf_ref[...]) if final_norm else y


def _mlp(x, g, w_up, w_down, g_final, layer, final_norm, tm=1024, tf=2048):
    t = x.shape[0]
    tm = min(tm, t)
    return pl.pallas_call(
        functools.partial(_mlp_kernel, final_norm=final_norm),
        grid=(t // tm, D_FF // tf),
        in_specs=[
            pl.BlockSpec((tm, D_MODEL), lambda i, j: (i, 0)),
            pl.BlockSpec((1, D_MODEL), lambda i, j: (0, 0)),
            pl.BlockSpec((None, D_MODEL, tf), lambda i, j: (layer, 0, j)),
            pl.BlockSpec((None, tf, D_MODEL), lambda i, j: (layer, j, 0)),
            pl.BlockSpec((1, D_MODEL), lambda i, j: (0, 0)),
        ],
        out_specs=pl.BlockSpec((tm, D_MODEL), lambda i, j: (i, 0)),
        out_shape=jax.ShapeDtypeStruct((t, D_MODEL), F32),
        scratch_shapes=[pltpu.VMEM((tm, D_MODEL), BF16), pltpu.VMEM((tm, D_MODEL), F32)],
        compiler_params=_cparams("parallel", "arbitrary"),
        name="mlp",
    )(x, g, w_up, w_down, g_final)


def _lane_row(values, lane0):
    return jnp.zeros((LANES,), F32).at[lane0:lane0 + values.shape[0]].set(values.astype(F32))


def _param_rows(a_log, dt_bias, lane0):
    rows = jnp.zeros((8, LANES), F32)
    return rows.at[0].set(_lane_row(a_log, lane0)).at[1].set(_lane_row(dt_bias, lane0))


def _sb_cumsum_matrix():
    j = jnp.arange(SB_TILE)[:, None]
    s = jnp.arange(SB_TILE)[None, :]
    half = jnp.concatenate([(j >= s).astype(BF16), jnp.ones((SB_TILE, SB_TILE), BF16)], axis=1)
    return jnp.concatenate([half, half], axis=0)


def _ssm_expand_matrix():
    lane = jnp.arange(LANES)[:, None]
    ch = jnp.arange(MIX_W)[None, :]
    return (lane == LANE_SSM_DT + ch // SSM_HEAD_DIM).astype(BF16)


def _split_w_in(w):
    w = _bf(w)
    big = jnp.concatenate([w[..., 0:4096], w[..., 4112:10256], w[..., 10272:13344]], axis=-1)
    pad = jnp.zeros(w.shape[:-1] + (LANES - 32,), w.dtype)
    small = jnp.concatenate([w[..., 4096:4112], w[..., 10256:10272], pad], axis=-1)
    return big, small


def kernel(x, norm_mix, w_in, dn_conv_w, dn_a_log, dn_dt_bias, dn_norm_w, ssm_conv_w, ssm_conv_b, ssm_a_log,
           ssm_dt_bias, ssm_d, ssm_norm_w, w_branch, w_out, norm_mlp, w_up, w_down, norm_final):
    batch, seq, _ = x.shape
    depth = w_in.shape[0]
    h = x.reshape(batch * seq, D_MODEL)
    sb_w = _sb_cumsum_matrix()
    emat = _ssm_expand_matrix()
    g_final = norm_final.reshape(1, D_MODEL)
    w_big, w_small = _split_w_in(w_in)
    w_branch, w_out, w_up, w_down = _bf(w_branch), _bf(w_out), _bf(w_up), _bf(w_down)
    for l in range(depth):
        big, small = _inproj(h, norm_mix[l].reshape(1, D_MODEL), w_big, w_small, l)
        o_dn = _dn_mixer(big, small, dn_conv_w[l], _param_rows(dn_a_log[l], dn_dt_bias[l], LANE_DN_A),
                         dn_norm_w[l].reshape(1, DN_HEAD_DIM), batch, seq)
        o_sb = _sb_mixer(big, sb_w, batch, seq)
        o_ssm = _ssd_mixer(big, small, ssm_conv_w[l], ssm_conv_b[l].reshape(1, -1),
                           _param_rows(ssm_a_log[l], ssm_dt_bias[l], LANE_SSM_DT),
                           jnp.repeat(ssm_d[l], SSM_HEAD_DIM).reshape(1, MIX_W),
                           ssm_norm_w[l].reshape(1, MIX_W), emat, batch, seq)
        h = _merge(h, o_dn, o_sb, o_ssm, big, w_branch, w_out, l)
        h = _mlp(h, norm_mlp[l].reshape(1, D_MODEL), w_up, w_down, g_final, l, final_norm=(l == depth - 1))
    return h.reshape(batch, seq, D_MODEL)
```

```python
import functools

import jax
import jax.numpy as jnp
from jax import lax
from jax.experimental import pallas as pl
from jax.experimental.pallas import tpu as pltpu

F32 = jnp.float32
BF16 = jnp.bfloat16

D_MODEL = 1024
MIX_W = D_MODEL
DN_HEADS = 8
DN_HEAD_DIM = 128
SB_HEADS = 16
SB_HEAD_DIM = 64
SSM_HEADS = 16
SSM_HEAD_DIM = 64
SSM_STATE = 128
SSM_GROUPS = 4
CHUNK = 64
D_FF = 4 * D_MODEL
EPS = 1e-6
CONV_WIDTH = 4

LANES = 128
TBLK = 2 * CHUNK
HALO = 8
DN_BASE_BLOCK = 8
DN_TILES = 2

COL_DN_QKV = 0
COL_DN_GATE = 3072
COL_SB_QKV = 4096
COL_SSM_Z = 7168
COL_SSM_XBC = 8192
COL_GATES = 10240
N_BIG = 13312
LANE_DN_A = 0
LANE_DN_B = 8
LANE_SSM_DT = 16

VMEM_LIMIT = 56 * 1024 * 1024


def _cparams(*sem):
    return pltpu.CompilerParams(dimension_semantics=sem, vmem_limit_bytes=VMEM_LIMIT)


def _bf(x):
    return x.astype(BF16)


def _mm(a, b):
    return jnp.dot(_bf(a), _bf(b), preferred_element_type=F32)


def _mm_nt(a, b):
    return lax.dot_general(_bf(a), _bf(b), (((1,), (1,)), ((), ())), preferred_element_type=F32)


def _split2(a):
    hi = _bf(a)
    lo = _bf(a - hi.astype(F32))
    return hi, lo


def _split3(a):
    hi = _bf(a)
    r = a - hi.astype(F32)
    mid = _bf(r)
    lo = _bf(r - mid.astype(F32))
    return hi, mid, lo


def _mm_x3s(a_split, b_split):
    ah, al = a_split
    bh, bl = b_split
    lhs = jnp.concatenate([ah, al, ah], axis=1)
    rhs = jnp.concatenate([bh, bh, bl], axis=0)
    return jnp.dot(lhs, rhs, preferred_element_type=F32)


def _mm_sel_lhs(sel, b):
    s = _bf(sel)
    bh, bm, bl = _split3(b)
    return jnp.dot(jnp.concatenate([s, s, s], axis=1), jnp.concatenate([bh, bm, bl], axis=0),
                   preferred_element_type=F32)


def _mm_sel_rhs(a, sel_bf):
    ah, am, al = _split3(a)
    return jnp.dot(jnp.concatenate([ah, am, al], axis=1), jnp.concatenate([sel_bf, sel_bf, sel_bf], axis=0),
                   preferred_element_type=F32)


def _softplus(x):
    return jnp.maximum(x, 0.0) + jnp.log(1.0 + jnp.exp(-jnp.abs(x)))


def _sigmoid(x):
    return 1.0 / (1.0 + jnp.exp(-x))


def _silu(x):
    return x * _sigmoid(x)


def _rms(x, w):
    return x * lax.rsqrt(jnp.mean(x * x, axis=-1, keepdims=True) + EPS) * w


def _causal_conv(halo, cur, w, bias):
    rows = cur.shape[0]
    xx = jnp.concatenate([halo, cur], axis=0)
    y = w[CONV_WIDTH - 1:CONV_WIDTH, :] * cur
    for k in range(CONV_WIDTH - 1):
        shifted = pltpu.roll(xx, CONV_WIDTH - 1 - k, 0)[HALO:HALO + rows]
        y = y + w[k:k + 1, :] * shifted
    if bias is not None:
        y = y + bias
    return y


def _chunk_masks():
    row = lax.broadcasted_iota(jnp.int32, (TBLK, TBLK), 0)
    col = lax.broadcasted_iota(jnp.int32, (TBLK, TBLK), 1)
    same = (row >= CHUNK) == (col >= CHUNK)
    return row, col, same & (col <= row), same & (col < row)


def _inproj_kernel(x_ref, g_ref, wb_ref, ws_ref, big_ref, small_ref, xn_scr):
    @pl.when(pl.program_id(1) == 0)
    def _():
        xn_scr[...] = _bf(_rms(x_ref[...], g_ref[...]))
        small_ref[...] = jnp.dot(xn_scr[...], ws_ref[...], preferred_element_type=F32)

    big_ref[...] = _bf(jnp.dot(xn_scr[...], wb_ref[...], preferred_element_type=F32))


def _inproj(x, g, w_big, w_small, layer, tm=2048, tn=1024):
    t = x.shape[0]
    tm = min(tm, t)
    return pl.pallas_call(
        _inproj_kernel,
        grid=(t // tm, N_BIG // tn),
        in_specs=[
            pl.BlockSpec((tm, D_MODEL), lambda i, j: (i, 0)),
            pl.BlockSpec((1, D_MODEL), lambda i, j: (0, 0)),
            pl.BlockSpec((None, D_MODEL, tn), lambda i, j: (layer, 0, j)),
            pl.BlockSpec((None, D_MODEL, LANES), lambda i, j: (layer, 0, 0)),
        ],
        out_specs=[
            pl.BlockSpec((tm, tn), lambda i, j: (i, j)),
            pl.BlockSpec((tm, LANES), lambda i, j: (i, 0)),
        ],
        out_shape=[jax.ShapeDtypeStruct((t, N_BIG), BF16), jax.ShapeDtypeStruct((t, LANES), F32)],
        scratch_shapes=[pltpu.VMEM((tm, D_MODEL), BF16)],
        compiler_params=_cparams("parallel", "arbitrary"),
        name="inproj",
    )(x, g, w_big, w_small)


def _dn_kernel(qkv_ref, gate_ref, sm_ref, cw_ref, prow_ref, nw_ref, o_ref, halo_scr, state_scr):
    @pl.when(pl.program_id(1) == 0)
    def _():
        halo_scr[...] = jnp.zeros_like(halo_scr)
        state_scr[...] = jnp.zeros_like(state_scr)

    row, col, m_incl, m_strict = _chunk_masks()
    tril = jnp.where(m_incl, 1.0, 0.0)
    eye = jnp.where(row == col, 1.0, 0.0)
    first = row[:, 0:1] < CHUNK

    units = [(j, h) for j in range(DN_TILES) for h in range(DN_HEADS)]
    heads = range(DN_HEADS)
    q, k, v, kb, vb, decay, gc_col, kk, qk = {}, {}, {}, {}, {}, {}, {}, {}, {}
    low, xs, tinv, cd = {}, {}, {}, {}
    u, w, attn, qg, kdt, gl = {}, {}, {}, {}, {}, {}
    pad = jnp.zeros((CHUNK, DN_HEAD_DIM), F32)

    def same_block(size):
        shift = size.bit_length() - 1
        return (row >> shift) == (col >> shift)

    def prepare(j):
        rows = slice(j * TBLK, (j + 1) * TBLK)
        sm = sm_ref[rows, :]
        g_all = -jnp.exp(prow_ref[0:1, :]) * _softplus(sm + prow_ref[1:2, :])
        beta_all = _sigmoid(sm)
        gc_all = _mm_sel_lhs(tril, g_all)
        gct_all = gc_all.T
        for h in heads:
            def conv_silu(seg):
                lanes = slice(seg * MIX_W + h * DN_HEAD_DIM, seg * MIX_W + (h + 1) * DN_HEAD_DIM)
                if j == 0:
                    halo = halo_scr[:, lanes]
                else:
                    halo = qkv_ref[j * TBLK - 16:j * TBLK, lanes].astype(F32)[16 - HALO:16]
                return _silu(_causal_conv(halo, qkv_ref[rows, lanes].astype(F32), cw_ref[:, lanes], None))

            un = (j, h)
            qh = conv_silu(0)
            kh = conv_silu(1)
            v[un] = conv_silu(2)
            q[un] = qh * lax.rsqrt(jnp.sum(qh * qh, axis=-1, keepdims=True) + EPS) * (DN_HEAD_DIM ** -0.5)
            k[un] = kh * lax.rsqrt(jnp.sum(kh * kh, axis=-1, keepdims=True) + EPS)
            gc_col[un] = gc_all[:, LANE_DN_A + h:LANE_DN_A + h + 1]
            beta = beta_all[:, LANE_DN_B + h:LANE_DN_B + h + 1]
            gc_row = gct_all[LANE_DN_A + h:LANE_DN_A + h + 1, :]
            decay[un] = jnp.exp(jnp.where(m_incl, gc_col[un] - gc_row, -jnp.inf))
            kb[un] = k[un] * beta
            vb[un] = v[un] * beta
    def gram(units):
        for un in units:
            kk[un] = _mm_nt(kb[un], k[un])
            qk[un] = _mm_nt(q[un], k[un])

    def solve(units):
        for un in units:
            low[un] = jnp.where(m_strict, kk[un] * decay[un], 0.0)
            x = -jnp.where(same_block(DN_BASE_BLOCK), low[un], 0.0)
            xs[un] = _split2(x)
            tinv[un] = eye + x
        for _ in range(2):
            for un in units:
                xs[un] = _split2(_mm_x3s(xs[un], xs[un]))
            for un in units:
                tinv[un] = tinv[un] + _mm_x3s(_split2(tinv[un]), xs[un])
        size = DN_BASE_BLOCK
        while size < CHUNK:
            off_diag = same_block(2 * size) & jnp.logical_not(same_block(size))
            for un in units:
                cd[un] = _mm(jnp.where(off_diag, low[un], 0.0), tinv[un])
            for un in units:
                tinv[un] = tinv[un] - _mm(tinv[un], cd[un])
            size *= 2
        for un in units:
            egc = jnp.exp(gc_col[un])
            u[un] = _mm(tinv[un], vb[un])
            w[un] = _mm(tinv[un], kb[un] * egc)
            attn[un] = _bf(jnp.where(m_incl, qk[un] * decay[un], 0.0))
            qg[un] = _bf(q[un] * egc)
            gl[un] = (gc_col[un][CHUNK - 1:CHUNK, :], gc_col[un][TBLK - 1:TBLK, :])
            kdt[un] = _bf((k[un] * jnp.exp(jnp.where(first, gl[un][0], gl[un][1]) - gc_col[un])).T)

    s = {h: state_scr[h] for h in heads}

    def scan(j):
        outs = {h: [] for h in heads}
        for c in range(2):
            r0, r1 = c * CHUNK, (c + 1) * CHUNK
            v_blk = {}
            for h in heads:
                un = (j, h)
                v_new = u[un][r0:r1] - _mm(w[un][r0:r1], s[h])
                v_blk[h] = _bf(jnp.concatenate([v_new, pad] if c == 0 else [pad, v_new], axis=0))
            for h in heads:
                un = (j, h)
                outs[h].append(_mm(qg[un][r0:r1], s[h]) + _mm(attn[un][r0:r1, :], v_blk[h]))
                s[h] = s[h] * jnp.exp(gl[un][c]) + _mm(kdt[un], v_blk[h])
        for h in heads:
            lanes = slice(h * DN_HEAD_DIM, (h + 1) * DN_HEAD_DIM)
            rows = slice(j * TBLK, (j + 1) * TBLK)
            o = jnp.concatenate(outs[h], axis=0)
            o_ref[rows, lanes] = _bf(_rms(o, nw_ref[...]) * _silu(gate_ref[rows, lanes].astype(F32)))

    for j in range(DN_TILES):
        prepare(j)
    gram(units)
    solve(units)
    for j in range(DN_TILES):
        scan(j)
    for h in heads:
        state_scr[h] = s[h]
    last = DN_TILES * TBLK
    halo_scr[...] = qkv_ref[last - 16:last, :].astype(F32)[16 - HALO:16]


def _dn_mixer(big, small, conv_w, prow, norm_w, batch, seq):
    rows = DN_TILES * TBLK
    nd = seq // rows
    row_map = lambda b, d: (b * nd + d, 0)
    return pl.pallas_call(
        _dn_kernel,
        grid=(batch, nd),
        in_specs=[
            pl.BlockSpec((rows, 3 * MIX_W), lambda b, d: (b * nd + d, COL_DN_QKV // (3 * MIX_W))),
            pl.BlockSpec((rows, MIX_W), lambda b, d: (b * nd + d, COL_DN_GATE // MIX_W)),
            pl.BlockSpec((rows, LANES), row_map),
            pl.BlockSpec((CONV_WIDTH, 3 * MIX_W), lambda b, d: (0, 0)),
            pl.BlockSpec((8, LANES), lambda b, d: (0, 0)),
            pl.BlockSpec((1, DN_HEAD_DIM), lambda b, d: (0, 0)),
        ],
        out_specs=pl.BlockSpec((rows, MIX_W), row_map),
        out_shape=jax.ShapeDtypeStruct((batch * seq, MIX_W), BF16),
        scratch_shapes=[pltpu.VMEM((HALO, 3 * MIX_W), F32), pltpu.VMEM((DN_HEADS, DN_HEAD_DIM, DN_HEAD_DIM), F32)],
        compiler_params=_cparams("parallel", "arbitrary"),
        name="deltanet",
    )(big, big, small, conv_w, prow, norm_w)


SB_TILE = 128
SB_TQ = 256
SB_SUB = SB_TQ // SB_TILE
SB_STRIP = 128
SB_HEADS_PER_STEP = 4
LOG2E = 1.4426950408889634
SB_LAG = 4


def _sb_kernel(q_ref, k_ref, v_ref, w_ref, o_ref, r_scr, acc_scr, za_scr, zb_scr, wgt_scr):
    qi = pl.program_id(2)
    row = lax.broadcasted_iota(jnp.int32, (SB_STRIP, SB_TILE), 0)
    col = lax.broadcasted_iota(jnp.int32, (SB_STRIP, SB_TILE), 1)
    wmat = w_ref[...]
    r_scr[...] = jnp.zeros_like(r_scr)
    acc_scr[...] = jnp.zeros_like(acc_scr)
    strips = range(SB_TQ // SB_STRIP)
    heads = range(SB_HEADS_PER_STEP)
    pairs = range(SB_HEADS_PER_STEP // 2)
    order = list(reversed(range(SB_SUB)))
    qh = {}
    for s in strips:
        for h in heads:
            lanes = slice((h // 2) * LANES, (h // 2 + 1) * LANES)
            q = q_ref[s * SB_STRIP:(s + 1) * SB_STRIP, lanes].astype(F32) * (SB_HEAD_DIM ** -0.5)
            qh[s, h] = _bf(jnp.where((col >= SB_HEAD_DIM) == (h % 2 == 1), q, 0.0))

    def chain_id(t, s, h):
        return (t * len(strips) + s) * SB_HEADS_PER_STEP + h

    def masked_out(t, s, diagonal):
        return diagonal and t * SB_TILE >= (s + 1) * SB_STRIP

    def block_rows(kb_lo):
        return pl.ds(pl.multiple_of(kb_lo * SB_TILE, SB_TILE), SB_SUB * SB_TILE)

    def score_ops(kb_lo, z_dst, diagonal):
        def one(t, s, h):
            def emit():
                rows = pl.ds(pl.multiple_of((kb_lo + t) * SB_TILE, SB_TILE), SB_TILE)
                kblk = k_ref[rows, (h // 2) * LANES:(h // 2 + 1) * LANES]
                z_dst[chain_id(t, s, h)] = _mm_nt(qh[s, h], kblk) * LOG2E
            return emit
        return [one(t, s, h) for t in order for s in strips for h in heads if not masked_out(t, s, diagonal)]

    def value_ops(kb_lo):
        def one(h):
            def emit():
                vblk = v_ref[block_rows(kb_lo), (h // 2) * LANES:(h // 2 + 1) * LANES]
                acc_scr[h] += jnp.dot(wgt_scr[h], vblk, preferred_element_type=F32)
            return emit
        return [one(h) for h in heads]

    def weight_stage(z_src, diagonal, extras):
        chains = [(t, s, h) for t in order for s in strips for h in heads]
        cs, tri = {}, {}

        def cumsum(c):
            t, s, _ = c
            z = z_src[chain_id(*c)]
            neg_abs = pltpu.bitcast(pltpu.bitcast(z, jnp.uint32) | jnp.uint32(0x80000000), F32)
            sp = jnp.maximum(z, 0.0) + jnp.log2(1.0 + jnp.exp2(neg_abs))
            if diagonal:
                tri[c] = col + t * SB_TILE < row + s * SB_STRIP
                sp = jnp.where(tri[c], sp, 0.0)
            cs[c] = jnp.dot(_bf(sp), wmat, preferred_element_type=F32)

        def weigh(c):
            t, s, h = c
            rows = slice(s * SB_STRIP, (s + 1) * SB_STRIP)
            lanes = slice(t * SB_TILE, (t + 1) * SB_TILE)
            wgt = jnp.exp2(z_src[chain_id(*c)] - cs[c][:, :SB_TILE] - r_scr[h, rows])
            wgt_scr[h, rows, lanes] = _bf(jnp.where(tri[c], wgt, 0.0) if diagonal else wgt)
            r_scr[h, rows] += cs[c][:, SB_TILE:]

        live = [c for c in chains if not masked_out(c[0], c[1], diagonal)]
        for t, s, h in chains:
            if masked_out(t, s, diagonal):
                wgt_scr[h, s * SB_STRIP:(s + 1) * SB_STRIP, t * SB_TILE:(t + 1) * SB_TILE] = (
                    jnp.zeros((SB_STRIP, SB_TILE), BF16))
        per_slot = -(-len(extras) // len(live))
        for i, c in enumerate(live):
            cumsum(c)
            for emit in extras[i * per_slot:(i + 1) * per_slot]:
                emit()
            if i >= SB_LAG:
                weigh(live[i - SB_LAG])
        for c in live[-SB_LAG:]:
            weigh(c)

    def step(cur, z_src, z_dst, issue_next=True):
        extras = value_ops((cur + 1) * SB_SUB)
        if issue_next:
            extras = extras + score_ops(jnp.maximum(cur - 1, 0) * SB_SUB, z_dst, False)
        weight_stage(z_src, False, extras)

    for emit in score_ops(qi * SB_SUB, za_scr, True):
        emit()
    weight_stage(za_scr, True, score_ops(jnp.maximum(qi - 1, 0) * SB_SUB, zb_scr, False))

    def body(j, carry):
        step(qi - 1 - 2 * j, zb_scr, za_scr)
        step(qi - 2 - 2 * j, za_scr, zb_scr)
        return carry

    lax.fori_loop(0, qi // 2, body, 0)

    @pl.when(qi % 2 == 1)
    def _():
        step(0, zb_scr, za_scr, issue_next=False)

    for emit in value_ops(0):
        emit()
    lane = lax.broadcasted_iota(jnp.int32, (SB_TQ, SB_TILE), 1)
    for p in range(SB_HEADS_PER_STEP // 2):
        o_ref[:, p * LANES:(p + 1) * LANES] = _bf(jnp.where(lane < SB_HEAD_DIM, acc_scr[2 * p], acc_scr[2 * p + 1]))


def _sb_mixer(big, wmat, batch, seq):
    nq = seq // SB_TQ
    width = SB_HEADS_PER_STEP * SB_HEAD_DIM
    groups = MIX_W // width
    qoff = COL_SB_QKV // width
    return pl.pallas_call(
        _sb_kernel,
        grid=(batch, groups, nq),
        in_specs=[
            pl.BlockSpec((SB_TQ, width), lambda b, p, i: (b * nq + i, qoff + p)),
            pl.BlockSpec((seq, width), lambda b, p, i: (b, qoff + groups + p)),
            pl.BlockSpec((seq, width), lambda b, p, i: (b, qoff + 2 * groups + p)),
            pl.BlockSpec((SB_TILE, 2 * SB_TILE), lambda b, p, i: (0, 0)),
        ],
        out_specs=pl.BlockSpec((SB_TQ, width), lambda b, p, i: (b * nq + i, p)),
        out_shape=jax.ShapeDtypeStruct((batch * seq, MIX_W), BF16),
        scratch_shapes=[pltpu.VMEM((SB_HEADS_PER_STEP, SB_TQ, SB_TILE), F32),
                        pltpu.VMEM((SB_HEADS_PER_STEP, SB_TQ, SB_TILE), F32),
                        pltpu.VMEM((SB_SUB * (SB_TQ // SB_STRIP) * SB_HEADS_PER_STEP, SB_STRIP, SB_TILE), F32),
                        pltpu.VMEM((SB_SUB * (SB_TQ // SB_STRIP) * SB_HEADS_PER_STEP, SB_STRIP, SB_TILE), F32),
                        pltpu.VMEM((SB_HEADS_PER_STEP, SB_TQ, SB_SUB * SB_TILE), BF16)],
        compiler_params=_cparams("parallel", "parallel", "arbitrary"),
        name="stickbreak",
    )(big, big, big, wmat)


SSD_TILES = 2
GROUP_W = MIX_W // SSM_GROUPS
HEADS_PER_GROUP = SSM_HEADS // SSM_GROUPS


def _ssd_kernel(z_ref, xbc_ref, sm_ref, cw_ref, cb_ref, prow_ref, drow_ref, nw_ref, e_ref, o_ref,
                halo_scr, state_scr):
    @pl.when(pl.program_id(1) == 0)
    def _():
        halo_scr[...] = jnp.zeros_like(halo_scr)
        state_scr[...] = jnp.zeros_like(state_scr)

    row, col, m_incl, _ = _chunk_masks()
    tril = jnp.where(m_incl, 1.0, 0.0)
    first = row[:, 0:1] < CHUNK
    gcol = lax.broadcasted_iota(jnp.int32, (TBLK, GROUP_W), 1)

    gn = SSM_GROUPS * SSM_STATE
    e = e_ref[...]
    for j in range(SSD_TILES):
        rows = slice(j * TBLK, (j + 1) * TBLK)
        cur = xbc_ref[rows, :].astype(F32)
        halo = halo_scr[...] if j == 0 else xbc_ref[j * TBLK - 16:j * TBLK, :].astype(F32)[16 - HALO:16]
        xbc = _silu(_causal_conv(halo, cur, cw_ref[...], cb_ref[...]))
        xs = xbc[:, :MIX_W]
        bm = xbc[:, MIX_W:MIX_W + gn]
        cm = xbc[:, MIX_W + gn:]

        dt_all = _softplus(sm_ref[rows, :] + prow_ref[1:2, :])
        a_all = -jnp.exp(prow_ref[0:1, :]) * dt_all
        acum_all = _mm_sel_lhs(tril, a_all)
        acum_t = acum_all.T
        dt_exp = _mm_sel_rhs(dt_all, e)
        acum_exp = _mm_sel_rhs(acum_all, e)
        xdt = xs * dt_exp

        for g in range(SSM_GROUPS):
            c_g = cm[:, g * SSM_STATE:(g + 1) * SSM_STATE]
            b_g = bm[:, g * SSM_STATE:(g + 1) * SSM_STATE]
            scores = _mm_nt(c_g, b_g)
            lo, hi = g * GROUP_W, (g + 1) * GROUP_W
            x_g = xdt[:, lo:hi]
            a_g = acum_exp[:, lo:hi]
            y_g = jnp.zeros((TBLK, GROUP_W), F32)
            for hh in range(HEADS_PER_GROUP):
                h = g * HEADS_PER_GROUP + hh
                a_col = acum_exp[:, h * SSM_HEAD_DIM:h * SSM_HEAD_DIM + 1]
                a_row = acum_t[LANE_SSM_DT + h:LANE_SSM_DT + h + 1, :]
                lmat = jnp.exp(jnp.where(m_incl, a_col - a_row, -jnp.inf))
                y_g = jnp.where(gcol // SSM_HEAD_DIM == hh, _mm(scores * lmat, x_g), y_g)

            b_t = b_g.T
            a_last0 = a_g[CHUNK - 1:CHUNK, :]
            a_last1 = a_g[TBLK - 1:TBLK, :]
            x_sc = x_g * jnp.exp(jnp.where(first, a_last0, a_last1) - a_g)
            st = state_scr[g]
            yoff = []
            for c, a_last in ((0, a_last0), (1, a_last1)):
                r0, r1 = c * CHUNK, (c + 1) * CHUNK
                yoff.append(_mm(c_g[r0:r1], st) * jnp.exp(a_g[r0:r1]))
                st = st * jnp.exp(a_last) + _mm(b_t, jnp.where(first == (c == 0), x_sc, 0.0))
            state_scr[g] = st
            y_g = y_g + jnp.concatenate(yoff, axis=0) + xs[:, lo:hi] * drow_ref[:, lo:hi]
            y_g = y_g * _silu(z_ref[rows, lo:hi].astype(F32))
            o_ref[rows, lo:hi] = _bf(_rms(y_g, nw_ref[:, lo:hi]))
    last = SSD_TILES * TBLK
    halo_scr[...] = xbc_ref[last - 16:last, :].astype(F32)[16 - HALO:16]


def _ssd_mixer(big, small, conv_w, conv_b, prow, drow, norm_w, emat, batch, seq):
    rows = SSD_TILES * TBLK
    nd = seq // rows
    conv_dim = MIX_W + 2 * SSM_GROUPS * SSM_STATE
    row_map = lambda b, d: (b * nd + d, 0)
    const = lambda b, d: (0, 0)
    return pl.pallas_call(
        _ssd_kernel,
        grid=(batch, nd),
        in_specs=[
            pl.BlockSpec((rows, MIX_W), lambda b, d: (b * nd + d, COL_SSM_Z // MIX_W)),
            pl.BlockSpec((rows, conv_dim), lambda b, d: (b * nd + d, COL_SSM_XBC // conv_dim)),
            pl.BlockSpec((rows, LANES), row_map),
            pl.BlockSpec((CONV_WIDTH, conv_dim), const),
            pl.BlockSpec((1, conv_dim), const),
            pl.BlockSpec((8, LANES), const),
            pl.BlockSpec((1, MIX_W), const),
            pl.BlockSpec((1, MIX_W), const),
            pl.BlockSpec((LANES, MIX_W), const),
        ],
        out_specs=pl.BlockSpec((rows, MIX_W), row_map),
        out_shape=jax.ShapeDtypeStruct((batch * seq, MIX_W), BF16),
        scratch_shapes=[pltpu.VMEM((HALO, conv_dim), F32), pltpu.VMEM((SSM_GROUPS, SSM_STATE, GROUP_W), F32)],
        compiler_params=_cparams("parallel", "arbitrary"),
        name="ssd",
    )(big, big, small, conv_w, conv_b, prow, drow, norm_w, emat)


def _merge_kernel(x_ref, odn_ref, osb_ref, ossm_ref, g0_ref, g1_ref, g2_ref, wb_ref, wo_ref, o_ref):
    m = _sigmoid(g0_ref[...].astype(F32)) * jnp.dot(odn_ref[...], wb_ref[0], preferred_element_type=F32)
    m = m + _sigmoid(g1_ref[...].astype(F32)) * jnp.dot(osb_ref[...], wb_ref[1], preferred_element_type=F32)
    m = m + _sigmoid(g2_ref[...].astype(F32)) * jnp.dot(ossm_ref[...], wb_ref[2], preferred_element_type=F32)
    o_ref[...] = x_ref[...] + jnp.dot(_bf(m), wo_ref[...], preferred_element_type=F32)


def _merge(x, o_dn, o_sb, o_ssm, big, w_branch, w_out, layer, tm=512):
    t = x.shape[0]
    tm = min(tm, t)
    rows = lambda i: (i, 0)
    gcol = COL_GATES // D_MODEL
    return pl.pallas_call(
        _merge_kernel,
        grid=(t // tm,),
        in_specs=[
            pl.BlockSpec((tm, D_MODEL), rows),
            pl.BlockSpec((tm, MIX_W), rows),
            pl.BlockSpec((tm, MIX_W), rows),
            pl.BlockSpec((tm, MIX_W), rows),
            pl.BlockSpec((tm, D_MODEL), lambda i: (i, gcol)),
            pl.BlockSpec((tm, D_MODEL), lambda i: (i, gcol + 1)),
            pl.BlockSpec((tm, D_MODEL), lambda i: (i, gcol + 2)),
            pl.BlockSpec((None, 3, MIX_W, D_MODEL), lambda i: (layer, 0, 0, 0)),
            pl.BlockSpec((None, D_MODEL, D_MODEL), lambda i: (layer, 0, 0)),
        ],
        out_specs=pl.BlockSpec((tm, D_MODEL), rows),
        out_shape=jax.ShapeDtypeStruct((t, D_MODEL), F32),
        compiler_params=_cparams("parallel"),
        name="merge",
    )(x, o_dn, o_sb, o_ssm, big, big, big, w_branch, w_out)


def _mlp_kernel(x_ref, g_ref, wu_ref, wd_ref, gf_ref, o_ref, xn_scr, acc_scr, *, final_norm):
    j = pl.program_id(1)

    @pl.when(j == 0)
    def _():
        xn_scr[...] = _bf(_rms(x_ref[...], g_ref[...]))
        acc_scr[...] = jnp.zeros_like(acc_scr)

    h = jnp.dot(xn_scr[...], wu_ref[...], preferred_element_type=F32)
    h = jnp.square(jnp.maximum(h, 0.0))
    acc_scr[...] += jnp.dot(_bf(h), wd_ref[...], preferred_element_type=F32)

    @pl.when(j == pl.num_programs(1) - 1)
    def _():
        y = x_ref[...] + acc_scr[...]
        o_ref[...] = _rms(y, gf_ref[...]) if final_norm else y


def _mlp(x, g, w_up, w_down, g_final, layer, final_norm, tm=1024, tf=2048):
    t = x.shape[0]
    tm = min(tm, t)
    return pl.pallas_call(
        functools.partial(_mlp_kernel, final_norm=final_norm),
        grid=(t // tm, D_FF // tf),
        in_specs=[
            pl.BlockSpec((tm, D_MODEL), lambda i, j: (i, 0)),
            pl.BlockSpec((1, D_MODEL), lambda i, j: (0, 0)),
            pl.BlockSpec((None, D_MODEL, tf), lambda i, j: (layer, 0, j)),
            pl.BlockSpec((None, tf, D_MODEL), lambda i, j: (layer, j, 0)),
            pl.BlockSpec((1, D_MODEL), lambda i, j: (0, 0)),
        ],
        out_specs=pl.BlockSpec((tm, D_MODEL), lambda i, j: (i, 0)),
        out_shape=jax.ShapeDtypeStruct((t, D_MODEL), F32),
        scratch_shapes=[pltpu.VMEM((tm, D_MODEL), BF16), pltpu.VMEM((tm, D_MODEL), F32)],
        compiler_params=_cparams("parallel", "arbitrary"),
        name="mlp",
    )(x, g, w_up, w_down, g_final)


def _lane_row(values, lane0):
    return jnp.zeros((LANES,), F32).at[lane0:lane0 + values.shape[0]].set(values.astype(F32))


def _param_rows(a_log, dt_bias, lane0):
    rows = jnp.zeros((8, LANES), F32)
    return rows.at[0].set(_lane_row(a_log, lane0)).at[1].set(_lane_row(dt_bias, lane0))


def _sb_cumsum_matrix():
    j = jnp.arange(SB_TILE)[:, None]
    s = jnp.arange(SB_TILE)[None, :]
    return jnp.concatenate([(j >= s).astype(BF16), jnp.ones((SB_TILE, SB_TILE), BF16)], axis=1)


def _ssm_expand_matrix():
    lane = jnp.arange(LANES)[:, None]
    ch = jnp.arange(MIX_W)[None, :]
    return (lane == LANE_SSM_DT + ch // SSM_HEAD_DIM).astype(BF16)


def _split_w_in(w):
    w = _bf(w)
    big = jnp.concatenate([w[..., 0:4096], w[..., 4112:10256], w[..., 10272:13344]], axis=-1)
    pad = jnp.zeros(w.shape[:-1] + (LANES - 32,), w.dtype)
    small = jnp.concatenate([w[..., 4096:4112], w[..., 10256:10272], pad], axis=-1)
    return big, small


def kernel(x, norm_mix, w_in, dn_conv_w, dn_a_log, dn_dt_bias, dn_norm_w, ssm_conv_w, ssm_conv_b, ssm_a_log,
           ssm_dt_bias, ssm_d, ssm_norm_w, w_branch, w_out, norm_mlp, w_up, w_down, norm_final):
    batch, seq, _ = x.shape
    depth = w_in.shape[0]
    h = x.reshape(batch * seq, D_MODEL)
    sb_w = _sb_cumsum_matrix()
    emat = _ssm_expand_matrix()
    g_final = norm_final.reshape(1, D_MODEL)
    w_big, w_small = _split_w_in(w_in)
    w_branch, w_out, w_up, w_down = _bf(w_branch), _bf(w_out), _bf(w_up), _bf(w_down)
    for l in range(depth):
        big, small = _inproj(h, norm_mix[l].reshape(1, D_MODEL), w_big, w_small, l)
        o_dn = _dn_mixer(big, small, dn_conv_w[l], _param_rows(dn_a_log[l], dn_dt_bias[l], LANE_DN_A),
                         dn_norm_w[l].reshape(1, DN_HEAD_DIM), batch, seq)
        o_sb = _sb_mixer(big, sb_w, batch, seq)
        o_ssm = _ssd_mixer(big, small, ssm_conv_w[l], ssm_conv_b[l].reshape(1, -1),
                           _param_rows(ssm_a_log[l], ssm_dt_bias[l], LANE_SSM_DT),
                           jnp.repeat(ssm_d[l], SSM_HEAD_DIM).reshape(1, MIX_W),
                           ssm_norm_w[l].reshape(1, MIX_W), emat, batch, seq)
        h = _merge(h, o_dn, o_sb, o_ssm, big, w_branch, w_out, l)
        h = _mlp(h, norm_mlp[l].reshape(1, D_MODEL), w_up, w_down, g_final, l, final_norm=(l == depth - 1))
    return h.reshape(batch, seq, D_MODEL)
```

```python
import functools

import jax
import jax.numpy as jnp
from jax import lax
from jax.experimental import pallas as pl
from jax.experimental.pallas import tpu as pltpu

F32 = jnp.float32
BF16 = jnp.bfloat16

D_MODEL = 1024
MIX_W = D_MODEL
DN_HEADS = 8
DN_HEAD_DIM = 128
SB_HEADS = 16
SB_HEAD_DIM = 64
SSM_HEADS = 16
SSM_HEAD_DIM = 64
SSM_STATE = 128
SSM_GROUPS = 4
CHUNK = 64
D_FF = 4 * D_MODEL
EPS = 1e-6
CONV_WIDTH = 4

LANES = 128
TBLK = 2 * CHUNK
HALO = 8
DN_BASE_BLOCK = 8
DN_TILES = 2

COL_DN_QKV = 0
COL_DN_GATE = 3072
COL_SB_QKV = 4096
COL_SSM_Z = 7168
COL_SSM_XBC = 8192
COL_GATES = 10240
N_BIG = 13312
LANE_DN_A = 0
LANE_DN_B = 8
LANE_SSM_DT = 16

VMEM_LIMIT = 56 * 1024 * 1024


def _cparams(*sem):
    return pltpu.CompilerParams(dimension_semantics=sem, vmem_limit_bytes=VMEM_LIMIT)


def _bf(x):
    return x.astype(BF16)


def _mm(a, b):
    return jnp.dot(_bf(a), _bf(b), preferred_element_type=F32)


def _mm_nt(a, b):
    return lax.dot_general(_bf(a), _bf(b), (((1,), (1,)), ((), ())), preferred_element_type=F32)


def _split2(a):
    hi = _bf(a)
    lo = _bf(a - hi.astype(F32))
    return hi, lo


def _split3(a):
    hi = _bf(a)
    r = a - hi.astype(F32)
    mid = _bf(r)
    lo = _bf(r - mid.astype(F32))
    return hi, mid, lo


def _mm_x3s(a_split, b_split):
    ah, al = a_split
    bh, bl = b_split
    lhs = jnp.concatenate([ah, al, ah], axis=1)
    rhs = jnp.concatenate([bh, bh, bl], axis=0)
    return jnp.dot(lhs, rhs, preferred_element_type=F32)


def _mm_sel_lhs(sel, b):
    s = _bf(sel)
    bh, bm, bl = _split3(b)
    return jnp.dot(jnp.concatenate([s, s, s], axis=1), jnp.concatenate([bh, bm, bl], axis=0),
                   preferred_element_type=F32)


def _mm_sel_rhs(a, sel_bf):
    ah, am, al = _split3(a)
    return jnp.dot(jnp.concatenate([ah, am, al], axis=1), jnp.concatenate([sel_bf, sel_bf, sel_bf], axis=0),
                   preferred_element_type=F32)


def _softplus(x):
    return jnp.maximum(x, 0.0) + jnp.log(1.0 + jnp.exp(-jnp.abs(x)))


def _sigmoid(x):
    return 1.0 / (1.0 + jnp.exp(-x))


def _silu(x):
    return x * _sigmoid(x)


def _rms(x, w):
    return x * lax.rsqrt(jnp.mean(x * x, axis=-1, keepdims=True) + EPS) * w


def _causal_conv(halo, cur, w, bias):
    rows = cur.shape[0]
    xx = jnp.concatenate([halo, cur], axis=0)
    y = w[CONV_WIDTH - 1:CONV_WIDTH, :] * cur
    for k in range(CONV_WIDTH - 1):
        shifted = pltpu.roll(xx, CONV_WIDTH - 1 - k, 0)[HALO:HALO + rows]
        y = y + w[k:k + 1, :] * shifted
    if bias is not None:
        y = y + bias
    return y


def _chunk_masks():
    row = lax.broadcasted_iota(jnp.int32, (TBLK, TBLK), 0)
    col = lax.broadcasted_iota(jnp.int32, (TBLK, TBLK), 1)
    same = (row >= CHUNK) == (col >= CHUNK)
    return row, col, same & (col <= row), same & (col < row)


def _inproj_kernel(x_ref, g_ref, wb_ref, ws_ref, big_ref, small_ref, xn_scr):
    @pl.when(pl.program_id(1) == 0)
    def _():
        xn_scr[...] = _bf(_rms(x_ref[...], g_ref[...]))
        small_ref[...] = jnp.dot(xn_scr[...], ws_ref[...], preferred_element_type=F32)

    big_ref[...] = _bf(jnp.dot(xn_scr[...], wb_ref[...], preferred_element_type=F32))


def _inproj(x, g, w_big, w_small, layer, tm=2048, tn=1024):
    t = x.shape[0]
    tm = min(tm, t)
    return pl.pallas_call(
        _inproj_kernel,
        grid=(t // tm, N_BIG // tn),
        in_specs=[
            pl.BlockSpec((tm, D_MODEL), lambda i, j: (i, 0)),
            pl.BlockSpec((1, D_MODEL), lambda i, j: (0, 0)),
            pl.BlockSpec((None, D_MODEL, tn), lambda i, j: (layer, 0, j)),
            pl.BlockSpec((None, D_MODEL, LANES), lambda i, j: (layer, 0, 0)),
        ],
        out_specs=[
            pl.BlockSpec((tm, tn), lambda i, j: (i, j)),
            pl.BlockSpec((tm, LANES), lambda i, j: (i, 0)),
        ],
        out_shape=[jax.ShapeDtypeStruct((t, N_BIG), BF16), jax.ShapeDtypeStruct((t, LANES), F32)],
        scratch_shapes=[pltpu.VMEM((tm, D_MODEL), BF16)],
        compiler_params=_cparams("parallel", "arbitrary"),
        name="inproj",
    )(x, g, w_big, w_small)


def _dn_kernel(qkv_ref, gate_ref, sm_ref, cw_ref, prow_ref, nw_ref, o_ref, halo_scr, state_scr):
    @pl.when(pl.program_id(1) == 0)
    def _():
        halo_scr[...] = jnp.zeros_like(halo_scr)
        state_scr[...] = jnp.zeros_like(state_scr)

    row, col, m_incl, m_strict = _chunk_masks()
    tril = jnp.where(m_incl, 1.0, 0.0)
    eye = jnp.where(row == col, 1.0, 0.0)
    first = row[:, 0:1] < CHUNK

    units = [(j, h) for j in range(DN_TILES) for h in range(DN_HEADS)]
    heads = range(DN_HEADS)
    q, k, v, kb, vb, decay, gc_col, kk, qk = {}, {}, {}, {}, {}, {}, {}, {}, {}
    low, xs, tinv, cd = {}, {}, {}, {}
    u, w, attn, qg, kdt, gl = {}, {}, {}, {}, {}, {}
    pad = jnp.zeros((CHUNK, DN_HEAD_DIM), F32)

    def same_block(size):
        shift = size.bit_length() - 1
        return (row >> shift) == (col >> shift)

    def prepare(j):
        rows = slice(j * TBLK, (j + 1) * TBLK)
        sm = sm_ref[rows, :]
        g_all = -jnp.exp(prow_ref[0:1, :]) * _softplus(sm + prow_ref[1:2, :])
        beta_all = _sigmoid(sm)
        gc_all = _mm_sel_lhs(tril, g_all)
        gct_all = gc_all.T
        for h in heads:
            def conv_silu(seg):
                lanes = slice(seg * MIX_W + h * DN_HEAD_DIM, seg * MIX_W + (h + 1) * DN_HEAD_DIM)
                if j == 0:
                    halo = halo_scr[:, lanes]
                else:
                    halo = qkv_ref[j * TBLK - 16:j * TBLK, lanes].astype(F32)[16 - HALO:16]
                return _silu(_causal_conv(halo, qkv_ref[rows, lanes].astype(F32), cw_ref[:, lanes], None))

            un = (j, h)
            qh = conv_silu(0)
            kh = conv_silu(1)
            v[un] = conv_silu(2)
            q[un] = qh * lax.rsqrt(jnp.sum(qh * qh, axis=-1, keepdims=True) + EPS) * (DN_HEAD_DIM ** -0.5)
            k[un] = kh * lax.rsqrt(jnp.sum(kh * kh, axis=-1, keepdims=True) + EPS)
            gc_col[un] = gc_all[:, LANE_DN_A + h:LANE_DN_A + h + 1]
            beta = beta_all[:, LANE_DN_B + h:LANE_DN_B + h + 1]
            gc_row = gct_all[LANE_DN_A + h:LANE_DN_A + h + 1, :]
            decay[un] = jnp.exp(jnp.where(m_incl, gc_col[un] - gc_row, -jnp.inf))
            kb[un] = k[un] * beta
            vb[un] = v[un] * beta
    def gram(units):
        for un in units:
            kk[un] = _mm_nt(kb[un], k[un])
            qk[un] = _mm_nt(q[un], k[un])

    def solve(units):
        for un in units:
            low[un] = jnp.where(m_strict, kk[un] * decay[un], 0.0)
            x = -jnp.where(same_block(DN_BASE_BLOCK), low[un], 0.0)
            xs[un] = _split2(x)
            tinv[un] = eye + x
        for _ in range(2):
            for un in units:
                xs[un] = _split2(_mm_x3s(xs[un], xs[un]))
            for un in units:
                tinv[un] = tinv[un] + _mm_x3s(_split2(tinv[un]), xs[un])
        size = DN_BASE_BLOCK
        while size < CHUNK:
            off_diag = same_block(2 * size) & jnp.logical_not(same_block(size))
            for un in units:
                cd[un] = _mm(jnp.where(off_diag, low[un], 0.0), tinv[un])
            for un in units:
                tinv[un] = tinv[un] - _mm(tinv[un], cd[un])
            size *= 2
        for un in units:
            egc = jnp.exp(gc_col[un])
            u[un] = _mm(tinv[un], vb[un])
            w[un] = _mm(tinv[un], kb[un] * egc)
            attn[un] = _bf(jnp.where(m_incl, qk[un] * decay[un], 0.0))
            qg[un] = _bf(q[un] * egc)
            gl[un] = (gc_col[un][CHUNK - 1:CHUNK, :], gc_col[un][TBLK - 1:TBLK, :])
            kdt[un] = _bf((k[un] * jnp.exp(jnp.where(first, gl[un][0], gl[un][1]) - gc_col[un])).T)

    s = {h: state_scr[h] for h in heads}

    def scan(j):
        outs = {h: [] for h in heads}
        for c in range(2):
            r0, r1 = c * CHUNK, (c + 1) * CHUNK
            v_blk = {}
            for h in heads:
                un = (j, h)
                v_new = u[un][r0:r1] - _mm(w[un][r0:r1], s[h])
                v_blk[h] = _bf(jnp.concatenate([v_new, pad] if c == 0 else [pad, v_new], axis=0))
            for h in heads:
                un = (j, h)
                outs[h].append(_mm(qg[un][r0:r1], s[h]) + _mm(attn[un][r0:r1, :], v_blk[h]))
                s[h] = s[h] * jnp.exp(gl[un][c]) + _mm(kdt[un], v_blk[h])
        for h in heads:
            lanes = slice(h * DN_HEAD_DIM, (h + 1) * DN_HEAD_DIM)
            rows = slice(j * TBLK, (j + 1) * TBLK)
            o = jnp.concatenate(outs[h], axis=0)
            o_ref[rows, lanes] = _bf(_rms(o, nw_ref[...]) * _silu(gate_ref[rows, lanes].astype(F32)))

    for j in range(DN_TILES):
        prepare(j)
    gram(units)
    solve(units)
    for j in range(DN_TILES):
        scan(j)
    for h in heads:
        state_scr[h] = s[h]
    last = DN_TILES * TBLK
    halo_scr[...] = qkv_ref[last - 16:last, :].astype(F32)[16 - HALO:16]


def _dn_mixer(big, small, conv_w, prow, norm_w, batch, seq):
    rows = DN_TILES * TBLK
    nd = seq // rows
    row_map = lambda b, d: (b * nd + d, 0)
    return pl.pallas_call(
        _dn_kernel,
        grid=(batch, nd),
        in_specs=[
            pl.BlockSpec((rows, 3 * MIX_W), lambda b, d: (b * nd + d, COL_DN_QKV // (3 * MIX_W))),
            pl.BlockSpec((rows, MIX_W), lambda b, d: (b * nd + d, COL_DN_GATE // MIX_W)),
            pl.BlockSpec((rows, LANES), row_map),
            pl.BlockSpec((CONV_WIDTH, 3 * MIX_W), lambda b, d: (0, 0)),
            pl.BlockSpec((8, LANES), lambda b, d: (0, 0)),
            pl.BlockSpec((1, DN_HEAD_DIM), lambda b, d: (0, 0)),
        ],
        out_specs=pl.BlockSpec((rows, MIX_W), row_map),
        out_shape=jax.ShapeDtypeStruct((batch * seq, MIX_W), BF16),
        scratch_shapes=[pltpu.VMEM((HALO, 3 * MIX_W), F32), pltpu.VMEM((DN_HEADS, DN_HEAD_DIM, DN_HEAD_DIM), F32)],
        compiler_params=_cparams("parallel", "arbitrary"),
        name="deltanet",
    )(big, big, small, conv_w, prow, norm_w)


SB_TILE = 128
SB_TQ = 512
SB_SUB = SB_TQ // SB_TILE
SB_STRIP = 128
SB_HEADS_PER_STEP = 4
LOG2E = 1.4426950408889634
SB_LAG = 4


def _sb_kernel(q_ref, k_ref, v_ref, w_ref, o_ref, r_scr, acc_scr, za_scr, zb_scr, wgt_scr):
    qi = pl.program_id(2)
    row = lax.broadcasted_iota(jnp.int32, (SB_STRIP, SB_TILE), 0)
    col = lax.broadcasted_iota(jnp.int32, (SB_STRIP, SB_TILE), 1)
    wmat = w_ref[...]
    r_scr[...] = jnp.zeros_like(r_scr)
    acc_scr[...] = jnp.zeros_like(acc_scr)
    strips = range(SB_TQ // SB_STRIP)
    heads = range(SB_HEADS_PER_STEP)
    pairs = range(SB_HEADS_PER_STEP // 2)
    order = list(reversed(range(SB_SUB)))
    qh = {}
    for s in strips:
        for h in heads:
            lanes = slice((h // 2) * LANES, (h // 2 + 1) * LANES)
            q = q_ref[s * SB_STRIP:(s + 1) * SB_STRIP, lanes].astype(F32) * (SB_HEAD_DIM ** -0.5)
            qh[s, h] = _bf(jnp.where((col >= SB_HEAD_DIM) == (h % 2 == 1), q, 0.0))

    def chain_id(t, s, h):
        return (t * len(strips) + s) * SB_HEADS_PER_STEP + h

    def masked_out(t, s, diagonal):
        return diagonal and t * SB_TILE >= (s + 1) * SB_STRIP

    def block_rows(kb_lo):
        return pl.ds(pl.multiple_of(kb_lo * SB_TILE, SB_TILE), SB_SUB * SB_TILE)

    def score_ops(kb_lo, z_dst, diagonal):
        def one(t, s, h):
            def emit():
                rows = pl.ds(pl.multiple_of((kb_lo + t) * SB_TILE, SB_TILE), SB_TILE)
                kblk = k_ref[rows, (h // 2) * LANES:(h // 2 + 1) * LANES]
                z_dst[chain_id(t, s, h)] = _mm_nt(qh[s, h], kblk) * LOG2E
            return emit
        return [one(t, s, h) for t in order for s in strips for h in heads if not masked_out(t, s, diagonal)]

    def value_ops(kb_lo):
        def one(h):
            def emit():
                vblk = v_ref[block_rows(kb_lo), (h // 2) * LANES:(h // 2 + 1) * LANES]
                acc_scr[h] += jnp.dot(wgt_scr[h], vblk, preferred_element_type=F32)
            return emit
        return [one(h) for h in heads]

    def weight_stage(z_src, diagonal, extras):
        chains = [(t, s, h) for t in order for s in strips for h in heads]
        cs, tri = {}, {}

        def cumsum(c):
            t, s, _ = c
            z = z_src[chain_id(*c)]
            neg_abs = pltpu.bitcast(pltpu.bitcast(z, jnp.uint32) | jnp.uint32(0x80000000), F32)
            sp = jnp.maximum(z, 0.0) + jnp.log2(1.0 + jnp.exp2(neg_abs))
            if diagonal:
                tri[c] = col + t * SB_TILE < row + s * SB_STRIP
                sp = jnp.where(tri[c], sp, 0.0)
            cs[c] = jnp.dot(_bf(sp), wmat, preferred_element_type=F32)

        def weigh(c):
            t, s, h = c
            rows = slice(s * SB_STRIP, (s + 1) * SB_STRIP)
            lanes = slice(t * SB_TILE, (t + 1) * SB_TILE)
            wgt = jnp.exp2(z_src[chain_id(*c)] - cs[c][:, :SB_TILE] - r_scr[h, rows])
            wgt_scr[h, rows, lanes] = _bf(jnp.where(tri[c], wgt, 0.0) if diagonal else wgt)
            r_scr[h, rows] += cs[c][:, SB_TILE:]

        live = [c for c in chains if not masked_out(c[0], c[1], diagonal)]
        for t, s, h in chains:
            if masked_out(t, s, diagonal):
                wgt_scr[h, s * SB_STRIP:(s + 1) * SB_STRIP, t * SB_TILE:(t + 1) * SB_TILE] = (
                    jnp.zeros((SB_STRIP, SB_TILE), BF16))
        per_slot = -(-len(extras) // len(live))
        for i, c in enumerate(live):
            cumsum(c)
            for emit in extras[i * per_slot:(i + 1) * per_slot]:
                emit()
            if i >= SB_LAG:
                weigh(live[i - SB_LAG])
        for c in live[-SB_LAG:]:
            weigh(c)

    def step(cur, z_src, z_dst, issue_next=True):
        extras = value_ops((cur + 1) * SB_SUB)
        if issue_next:
            extras = extras + score_ops(jnp.maximum(cur - 1, 0) * SB_SUB, z_dst, False)
        weight_stage(z_src, False, extras)

    for emit in score_ops(qi * SB_SUB, za_scr, True):
        emit()
    weight_stage(za_scr, True, score_ops(jnp.maximum(qi - 1, 0) * SB_SUB, zb_scr, False))

    def body(j, carry):
        step(qi - 1 - 2 * j, zb_scr, za_scr)
        step(qi - 2 - 2 * j, za_scr, zb_scr)
        return carry

    lax.fori_loop(0, qi // 2, body, 0)

    @pl.when(qi % 2 == 1)
    def _():
        step(0, zb_scr, za_scr, issue_next=False)

    for emit in value_ops(0):
        emit()
    lane = lax.broadcasted_iota(jnp.int32, (SB_TQ, SB_TILE), 1)
    for p in range(SB_HEADS_PER_STEP // 2):
        o_ref[:, p * LANES:(p + 1) * LANES] = _bf(jnp.where(lane < SB_HEAD_DIM, acc_scr[2 * p], acc_scr[2 * p + 1]))


def _sb_mixer(big, wmat, batch, seq):
    nq = seq // SB_TQ
    width = SB_HEADS_PER_STEP * SB_HEAD_DIM
    groups = MIX_W // width
    qoff = COL_SB_QKV // width
    return pl.pallas_call(
        _sb_kernel,
        grid=(batch, groups, nq),
        in_specs=[
            pl.BlockSpec((SB_TQ, width), lambda b, p, i: (b * nq + i, qoff + p)),
            pl.BlockSpec((seq, width), lambda b, p, i: (b, qoff + groups + p)),
            pl.BlockSpec((seq, width), lambda b, p, i: (b, qoff + 2 * groups + p)),
            pl.BlockSpec((SB_TILE, 2 * SB_TILE), lambda b, p, i: (0, 0)),
        ],
        out_specs=pl.BlockSpec((SB_TQ, width), lambda b, p, i: (b * nq + i, p)),
        out_shape=jax.ShapeDtypeStruct((batch * seq, MIX_W), BF16),
        scratch_shapes=[pltpu.VMEM((SB_HEADS_PER_STEP, SB_TQ, SB_TILE), F32),
                        pltpu.VMEM((SB_HEADS_PER_STEP, SB_TQ, SB_TILE), F32),
                        pltpu.VMEM((SB_SUB * (SB_TQ // SB_STRIP) * SB_HEADS_PER_STEP, SB_STRIP, SB_TILE), F32),
                        pltpu.VMEM((SB_SUB * (SB_TQ // SB_STRIP) * SB_HEADS_PER_STEP, SB_STRIP, SB_TILE), F32),
                        pltpu.VMEM((SB_HEADS_PER_STEP, SB_TQ, SB_SUB * SB_TILE), BF16)],
        compiler_params=_cparams("parallel", "parallel", "arbitrary"),
        name="stickbreak",
    )(big, big, big, wmat)


SSD_TILES = 2
GROUP_W = MIX_W // SSM_GROUPS
HEADS_PER_GROUP = SSM_HEADS // SSM_GROUPS


def _ssd_kernel(z_ref, xbc_ref, sm_ref, cw_ref, cb_ref, prow_ref, drow_ref, nw_ref, e_ref, o_ref,
                halo_scr, state_scr):
    @pl.when(pl.program_id(1) == 0)
    def _():
        halo_scr[...] = jnp.zeros_like(halo_scr)
        state_scr[...] = jnp.zeros_like(state_scr)

    row, col, m_incl, _ = _chunk_masks()
    tril = jnp.where(m_incl, 1.0, 0.0)
    first = row[:, 0:1] < CHUNK
    gcol = lax.broadcasted_iota(jnp.int32, (TBLK, GROUP_W), 1)

    gn = SSM_GROUPS * SSM_STATE
    e = e_ref[...]
    for j in range(SSD_TILES):
        rows = slice(j * TBLK, (j + 1) * TBLK)
        cur = xbc_ref[rows, :].astype(F32)
        halo = halo_scr[...] if j == 0 else xbc_ref[j * TBLK - 16:j * TBLK, :].astype(F32)[16 - HALO:16]
        xbc = _silu(_causal_conv(halo, cur, cw_ref[...], cb_ref[...]))
        xs = xbc[:, :MIX_W]
        bm = xbc[:, MIX_W:MIX_W + gn]
        cm = xbc[:, MIX_W + gn:]

        dt_all = _softplus(sm_ref[rows, :] + prow_ref[1:2, :])
        a_all = -jnp.exp(prow_ref[0:1, :]) * dt_all
        acum_all = _mm_sel_lhs(tril, a_all)
        acum_t = acum_all.T
        dt_exp = _mm_sel_rhs(dt_all, e)
        acum_exp = _mm_sel_rhs(acum_all, e)
        xdt = xs * dt_exp

        for g in range(SSM_GROUPS):
            c_g = cm[:, g * SSM_STATE:(g + 1) * SSM_STATE]
            b_g = bm[:, g * SSM_STATE:(g + 1) * SSM_STATE]
            scores = _mm_nt(c_g, b_g)
            lo, hi = g * GROUP_W, (g + 1) * GROUP_W
            x_g = xdt[:, lo:hi]
            a_g = acum_exp[:, lo:hi]
            y_g = jnp.zeros((TBLK, GROUP_W), F32)
            for hh in range(HEADS_PER_GROUP):
                h = g * HEADS_PER_GROUP + hh
                a_col = acum_exp[:, h * SSM_HEAD_DIM:h * SSM_HEAD_DIM + 1]
                a_row = acum_t[LANE_SSM_DT + h:LANE_SSM_DT + h + 1, :]
                lmat = jnp.exp(jnp.where(m_incl, a_col - a_row, -jnp.inf))
                y_g = jnp.where(gcol // SSM_HEAD_DIM == hh, _mm(scores * lmat, x_g), y_g)

            b_t = b_g.T
            a_last0 = a_g[CHUNK - 1:CHUNK, :]
            a_last1 = a_g[TBLK - 1:TBLK, :]
            x_sc = x_g * jnp.exp(jnp.where(first, a_last0, a_last1) - a_g)
            st = state_scr[g]
            yoff = []
            for c, a_last in ((0, a_last0), (1, a_last1)):
                r0, r1 = c * CHUNK, (c + 1) * CHUNK
                yoff.append(_mm(c_g[r0:r1], st) * jnp.exp(a_g[r0:r1]))
                st = st * jnp.exp(a_last) + _mm(b_t, jnp.where(first == (c == 0), x_sc, 0.0))
            state_scr[g] = st
            y_g = y_g + jnp.concatenate(yoff, axis=0) + xs[:, lo:hi] * drow_ref[:, lo:hi]
            y_g = y_g * _silu(z_ref[rows, lo:hi].astype(F32))
            o_ref[rows, lo:hi] = _bf(_rms(y_g, nw_ref[:, lo:hi]))
    last = SSD_TILES * TBLK
    halo_scr[...] = xbc_ref[last - 16:last, :].astype(F32)[16 - HALO:16]


def _ssd_mixer(big, small, conv_w, conv_b, prow, drow, norm_w, emat, batch, seq):
    rows = SSD_TILES * TBLK
    nd = seq // rows
    conv_dim = MIX_W + 2 * SSM_GROUPS * SSM_STATE
    row_map = lambda b, d: (b * nd + d, 0)
    const = lambda b, d: (0, 0)
    return pl.pallas_call(
        _ssd_kernel,
        grid=(batch, nd),
        in_specs=[
            pl.BlockSpec((rows, MIX_W), lambda b, d: (b * nd + d, COL_SSM_Z // MIX_W)),
            pl.BlockSpec((rows, conv_dim), lambda b, d: (b * nd + d, COL_SSM_XBC // conv_dim)),
            pl.BlockSpec((rows, LANES), row_map),
            pl.BlockSpec((CONV_WIDTH, conv_dim), const),
            pl.BlockSpec((1, conv_dim), const),
            pl.BlockSpec((8, LANES), const),
            pl.BlockSpec((1, MIX_W), const),
            pl.BlockSpec((1, MIX_W), const),
            pl.BlockSpec((LANES, MIX_W), const),
        ],
        out_specs=pl.BlockSpec((rows, MIX_W), row_map),
        out_shape=jax.ShapeDtypeStruct((batch * seq, MIX_W), BF16),
        scratch_shapes=[pltpu.VMEM((HALO, conv_dim), F32), pltpu.VMEM((SSM_GROUPS, SSM_STATE, GROUP_W), F32)],
        compiler_params=_cparams("parallel", "arbitrary"),
        name="ssd",
    )(big, big, small, conv_w, conv_b, prow, drow, norm_w, emat)


def _merge_kernel(x_ref, odn_ref, osb_ref, ossm_ref, g0_ref, g1_ref, g2_ref, wb_ref, wo_ref, o_ref):
    m = _sigmoid(g0_ref[...].astype(F32)) * jnp.dot(odn_ref[...], wb_ref[0], preferred_element_type=F32)
    m = m + _sigmoid(g1_ref[...].astype(F32)) * jnp.dot(osb_ref[...], wb_ref[1], preferred_element_type=F32)
    m = m + _sigmoid(g2_ref[...].astype(F32)) * jnp.dot(ossm_ref[...], wb_ref[2], preferred_element_type=F32)
    o_ref[...] = x_ref[...] + jnp.dot(_bf(m), wo_ref[...], preferred_element_type=F32)


def _merge(x, o_dn, o_sb, o_ssm, big, w_branch, w_out, layer, tm=512):
    t = x.shape[0]
    tm = min(tm, t)
    rows = lambda i: (i, 0)
    gcol = COL_GATES // D_MODEL
    return pl.pallas_call(
        _merge_kernel,
        grid=(t // tm,),
        in_specs=[
            pl.BlockSpec((tm, D_MODEL), rows),
            pl.BlockSpec((tm, MIX_W), rows),
            pl.BlockSpec((tm, MIX_W), rows),
            pl.BlockSpec((tm, MIX_W), rows),
            pl.BlockSpec((tm, D_MODEL), lambda i: (i, gcol)),
            pl.BlockSpec((tm, D_MODEL), lambda i: (i, gcol + 1)),
            pl.BlockSpec((tm, D_MODEL), lambda i: (i, gcol + 2)),
            pl.BlockSpec((None, 3, MIX_W, D_MODEL), lambda i: (layer, 0, 0, 0)),
            pl.BlockSpec((None, D_MODEL, D_MODEL), lambda i: (layer, 0, 0)),
        ],
        out_specs=pl.BlockSpec((tm, D_MODEL), rows),
        out_shape=jax.ShapeDtypeStruct((t, D_MODEL), F32),
        compiler_params=_cparams("parallel"),
        name="merge",
    )(x, o_dn, o_sb, o_ssm, big, big, big, w_branch, w_out)


def _mlp_kernel(x_ref, g_ref, wu_ref, wd_ref, gf_ref, o_ref, xn_scr, acc_scr, *, final_norm):
    j = pl.program_id(1)

    @pl.when(j == 0)
    def _():
        xn_scr[...] = _bf(_rms(x_ref[...], g_ref[...]))
        acc_scr[...] = jnp.zeros_like(acc_scr)

    h = jnp.dot(xn_scr[...], wu_ref[...], preferred_element_type=F32)
    h = jnp.square(jnp.maximum(h, 0.0))
    acc_scr[...] += jnp.dot(_bf(h), wd_ref[...], preferred_element_type=F32)

    @pl.when(j == pl.num_programs(1) - 1)
    def _():
        y = x_ref[...] + acc_scr[...]
        o_ref[...] = _rms(y, gf_ref[...]) if final_norm else y


def _mlp(x, g, w_up, w_down, g_final, layer, final_norm, tm=1024, tf=2048):
    t = x.shape[0]
    tm = min(tm, t)
    return pl.pallas_call(
        functools.partial(_mlp_kernel, final_norm=final_norm),
        grid=(t // tm, D_FF // tf),
        in_specs=[
            pl.BlockSpec((tm, D_MODEL), lambda i, j: (i, 0)),
            pl.BlockSpec((1, D_MODEL), lambda i, j: (0, 0)),
            pl.BlockSpec((None, D_MODEL, tf), lambda i, j: (layer, 0, j)),
            pl.BlockSpec((None, tf, D_MODEL), lambda i, j: (layer, j, 0)),
            pl.BlockSpec((1, D_MODEL), lambda i, j: (0, 0)),
        ],
        out_specs=pl.BlockSpec((tm, D_MODEL), lambda i, j: (i, 0)),
        out_shape=jax.ShapeDtypeStruct((t, D_MODEL), F32),
        scratch_shapes=[pltpu.VMEM((tm, D_MODEL), BF16), pltpu.VMEM((tm, D_MODEL), F32)],
        compiler_params=_cparams("parallel", "arbitrary"),
        name="mlp",
    )(x, g, w_up, w_down, g_final)


def _lane_row(values, lane0):
    return jnp.zeros((LANES,), F32).at[lane0:lane0 + values.shape[0]].set(values.astype(F32))


def _param_rows(a_log, dt_bias, lane0):
    rows = jnp.zeros((8, LANES), F32)
    return rows.at[0].set(_lane_row(a_log, lane0)).at[1].set(_lane_row(dt_bias, lane0))


def _sb_cumsum_matrix():
    j = jnp.arange(SB_TILE)[:, None]
    s = jnp.arange(SB_TILE)[None, :]
    return jnp.concatenate([(j >= s).astype(BF16), jnp.ones((SB_TILE, SB_TILE), BF16)], axis=1)


def _ssm_expand_matrix():
    lane = jnp.arange(LANES)[:, None]
    ch = jnp.arange(MIX_W)[None, :]
    return (lane == LANE_SSM_DT + ch // SSM_HEAD_DIM).astype(BF16)


def _split_w_in(w):
    w = _bf(w)
    big = jnp.concatenate([w[..., 0:4096], w[..., 4112:10256], w[..., 10272:13344]], axis=-1)
    pad = jnp.zeros(w.shape[:-1] + (LANES - 32,), w.dtype)
    small = jnp.concatenate([w[..., 4096:4112], w[..., 10256:10272], pad], axis=-1)
    return big, small


def kernel(x, norm_mix, w_in, dn_conv_w, dn_a_log, dn_dt_bias, dn_norm_w, ssm_conv_w, ssm_conv_b, ssm_a_log,
           ssm_dt_bias, ssm_d, ssm_norm_w, w_branch, w_out, norm_mlp, w_up, w_down, norm_final):
    batch, seq, _ = x.shape
    depth = w_in.shape[0]
    h = x.reshape(batch * seq, D_MODEL)
    sb_w = _sb_cumsum_matrix()
    emat = _ssm_expand_matrix()
    g_final = norm_final.reshape(1, D_MODEL)
    w_big, w_small = _split_w_in(w_in)
    w_branch, w_out, w_up, w_down = _bf(w_branch), _bf(w_out), _bf(w_up), _bf(w_down)
    for l in range(depth):
        big, small = _inproj(h, norm_mix[l].reshape(1, D_MODEL), w_big, w_small, l)
        o_dn = _dn_mixer(big, small, dn_conv_w[l], _param_rows(dn_a_log[l], dn_dt_bias[l], LANE_DN_A),
                         dn_norm_w[l].reshape(1, DN_HEAD_DIM), batch, seq)
        o_sb = _sb_mixer(big, sb_w, batch, seq)
        o_ssm = _ssd_mixer(big, small, ssm_conv_w[l], ssm_conv_b[l].reshape(1, -1),
                           _param_rows(ssm_a_log[l], ssm_dt_bias[l], LANE_SSM_DT),
                           jnp.repeat(ssm_d[l], SSM_HEAD_DIM).reshape(1, MIX_W),
                           ssm_norm_w[l].reshape(1, MIX_W), emat, batch, seq)
        h = _merge(h, o_dn, o_sb, o_ssm, big, w_branch, w_out, l)
        h = _mlp(h, norm_mlp[l].reshape(1, D_MODEL), w_up, w_down, g_final, l, final_norm=(l == depth - 1))
    return h.reshape(batch, seq, D_MODEL)
```

```python
import functools

import jax
import jax.numpy as jnp
from jax import lax
from jax.experimental import pallas as pl
from jax.experimental.pallas import tpu as pltpu

F32 = jnp.float32
BF16 = jnp.bfloat16

D_MODEL = 1024
MIX_W = D_MODEL
DN_HEADS = 8
DN_HEAD_DIM = 128
SB_HEADS = 16
SB_HEAD_DIM = 64
SSM_HEADS = 16
SSM_HEAD_DIM = 64
SSM_STATE = 128
SSM_GROUPS = 4
CHUNK = 64
D_FF = 4 * D_MODEL
EPS = 1e-6
CONV_WIDTH = 4

LANES = 128
TBLK = 2 * CHUNK
HALO = 8
DN_BASE_BLOCK = 8
DN_TILES = 2

COL_DN_QKV = 0
COL_DN_GATE = 3072
COL_SB_QKV = 4096
COL_SSM_Z = 7168
COL_SSM_XBC = 8192
COL_GATES = 10240
N_BIG = 13312
LANE_DN_A = 0
LANE_DN_B = 8
LANE_SSM_DT = 16

VMEM_LIMIT = 56 * 1024 * 1024


def _cparams(*sem):
    return pltpu.CompilerParams(dimension_semantics=sem, vmem_limit_bytes=VMEM_LIMIT)


def _bf(x):
    return x.astype(BF16)


def _mm(a, b):
    return jnp.dot(_bf(a), _bf(b), preferred_element_type=F32)


def _mm_nt(a, b):
    return lax.dot_general(_bf(a), _bf(b), (((1,), (1,)), ((), ())), preferred_element_type=F32)


def _split2(a):
    hi = _bf(a)
    lo = _bf(a - hi.astype(F32))
    return hi, lo


def _split3(a):
    hi = _bf(a)
    r = a - hi.astype(F32)
    mid = _bf(r)
    lo = _bf(r - mid.astype(F32))
    return hi, mid, lo


def _mm_x3s(a_split, b_split):
    ah, al = a_split
    bh, bl = b_split
    lhs = jnp.concatenate([ah, al, ah], axis=1)
    rhs = jnp.concatenate([bh, bh, bl], axis=0)
    return jnp.dot(lhs, rhs, preferred_element_type=F32)


def _mm_sel_lhs(sel, b):
    s = _bf(sel)
    bh, bm, bl = _split3(b)
    return jnp.dot(jnp.concatenate([s, s, s], axis=1), jnp.concatenate([bh, bm, bl], axis=0),
                   preferred_element_type=F32)


def _mm_sel_rhs(a, sel_bf):
    ah, am, al = _split3(a)
    return jnp.dot(jnp.concatenate([ah, am, al], axis=1), jnp.concatenate([sel_bf, sel_bf, sel_bf], axis=0),
                   preferred_element_type=F32)


def _softplus(x):
    return jnp.maximum(x, 0.0) + jnp.log(1.0 + jnp.exp(-jnp.abs(x)))


def _sigmoid(x):
    return 1.0 / (1.0 + jnp.exp(-x))


def _silu(x):
    return x * _sigmoid(x)


def _rms(x, w):
    return x * lax.rsqrt(jnp.mean(x * x, axis=-1, keepdims=True) + EPS) * w


def _causal_conv(halo, cur, w, bias):
    rows = cur.shape[0]
    xx = jnp.concatenate([halo, cur], axis=0)
    y = w[CONV_WIDTH - 1:CONV_WIDTH, :] * cur
    for k in range(CONV_WIDTH - 1):
        shifted = pltpu.roll(xx, CONV_WIDTH - 1 - k, 0)[HALO:HALO + rows]
        y = y + w[k:k + 1, :] * shifted
    if bias is not None:
        y = y + bias
    return y


def _chunk_masks():
    row = lax.broadcasted_iota(jnp.int32, (TBLK, TBLK), 0)
    col = lax.broadcasted_iota(jnp.int32, (TBLK, TBLK), 1)
    same = (row >= CHUNK) == (col >= CHUNK)
    return row, col, same & (col <= row), same & (col < row)


def _inproj_kernel(x_ref, g_ref, wb_ref, ws_ref, big_ref, small_ref, xn_scr):
    @pl.when(pl.program_id(1) == 0)
    def _():
        xn_scr[...] = _bf(_rms(x_ref[...], g_ref[...]))
        small_ref[...] = jnp.dot(xn_scr[...], ws_ref[...], preferred_element_type=F32)

    big_ref[...] = _bf(jnp.dot(xn_scr[...], wb_ref[...], preferred_element_type=F32))


def _inproj(x, g, w_big, w_small, layer, tm=2048, tn=1024):
    t = x.shape[0]
    tm = min(tm, t)
    return pl.pallas_call(
        _inproj_kernel,
        grid=(t // tm, N_BIG // tn),
        in_specs=[
            pl.BlockSpec((tm, D_MODEL), lambda i, j: (i, 0)),
            pl.BlockSpec((1, D_MODEL), lambda i, j: (0, 0)),
            pl.BlockSpec((None, D_MODEL, tn), lambda i, j: (layer, 0, j)),
            pl.BlockSpec((None, D_MODEL, LANES), lambda i, j: (layer, 0, 0)),
        ],
        out_specs=[
            pl.BlockSpec((tm, tn), lambda i, j: (i, j)),
            pl.BlockSpec((tm, LANES), lambda i, j: (i, 0)),
        ],
        out_shape=[jax.ShapeDtypeStruct((t, N_BIG), BF16), jax.ShapeDtypeStruct((t, LANES), F32)],
        scratch_shapes=[pltpu.VMEM((tm, D_MODEL), BF16)],
        compiler_params=_cparams("parallel", "arbitrary"),
        name="inproj",
    )(x, g, w_big, w_small)


def _dn_kernel(qkv_ref, gate_ref, sm_ref, cw_ref, prow_ref, nw_ref, o_ref, halo_scr, state_scr):
    @pl.when(pl.program_id(1) == 0)
    def _():
        halo_scr[...] = jnp.zeros_like(halo_scr)
        state_scr[...] = jnp.zeros_like(state_scr)

    row, col, m_incl, m_strict = _chunk_masks()
    tril = jnp.where(m_incl, 1.0, 0.0)
    eye = jnp.where(row == col, 1.0, 0.0)
    first = row[:, 0:1] < CHUNK

    units = [(j, h) for j in range(DN_TILES) for h in range(DN_HEADS)]
    heads = range(DN_HEADS)
    q, k, v, kb, vb, decay, gc_col, kk, qk = {}, {}, {}, {}, {}, {}, {}, {}, {}
    low, xs, tinv, cd = {}, {}, {}, {}
    u, w, attn, qg, kdt, gl = {}, {}, {}, {}, {}, {}
    pad = jnp.zeros((CHUNK, DN_HEAD_DIM), F32)

    def same_block(size):
        shift = size.bit_length() - 1
        return (row >> shift) == (col >> shift)

    def prepare(j):
        rows = slice(j * TBLK, (j + 1) * TBLK)
        sm = sm_ref[rows, :]
        g_all = -jnp.exp(prow_ref[0:1, :]) * _softplus(sm + prow_ref[1:2, :])
        beta_all = _sigmoid(sm)
        gc_all = _mm_sel_lhs(tril, g_all)
        gct_all = gc_all.T
        for h in heads:
            def conv_silu(seg):
                lanes = slice(seg * MIX_W + h * DN_HEAD_DIM, seg * MIX_W + (h + 1) * DN_HEAD_DIM)
                if j == 0:
                    halo = halo_scr[:, lanes]
                else:
                    halo = qkv_ref[j * TBLK - 16:j * TBLK, lanes].astype(F32)[16 - HALO:16]
                return _silu(_causal_conv(halo, qkv_ref[rows, lanes].astype(F32), cw_ref[:, lanes], None))

            un = (j, h)
            qh = conv_silu(0)
            kh = conv_silu(1)
            v[un] = conv_silu(2)
            q[un] = qh * lax.rsqrt(jnp.sum(qh * qh, axis=-1, keepdims=True) + EPS) * (DN_HEAD_DIM ** -0.5)
            k[un] = kh * lax.rsqrt(jnp.sum(kh * kh, axis=-1, keepdims=True) + EPS)
            gc_col[un] = gc_all[:, LANE_DN_A + h:LANE_DN_A + h + 1]
            beta = beta_all[:, LANE_DN_B + h:LANE_DN_B + h + 1]
            gc_row = gct_all[LANE_DN_A + h:LANE_DN_A + h + 1, :]
            decay[un] = jnp.exp(jnp.where(m_incl, gc_col[un] - gc_row, -jnp.inf))
            kb[un] = k[un] * beta
            vb[un] = v[un] * beta
    def gram(units):
        for un in units:
            kk[un] = _mm_nt(kb[un], k[un])
            qk[un] = _mm_nt(q[un], k[un])

    def solve(units):
        for un in units:
            low[un] = jnp.where(m_strict, kk[un] * decay[un], 0.0)
            x = -jnp.where(same_block(DN_BASE_BLOCK), low[un], 0.0)
            xs[un] = _split2(x)
            tinv[un] = eye + x
        for _ in range(2):
            for un in units:
                xs[un] = _split2(_mm_x3s(xs[un], xs[un]))
            for un in units:
                tinv[un] = tinv[un] + _mm_x3s(_split2(tinv[un]), xs[un])
        size = DN_BASE_BLOCK
        while size < CHUNK:
            off_diag = same_block(2 * size) & jnp.logical_not(same_block(size))
            for un in units:
                cd[un] = _mm(jnp.where(off_diag, low[un], 0.0), tinv[un])
            for un in units:
                tinv[un] = tinv[un] - _mm(tinv[un], cd[un])
            size *= 2
        for un in units:
            egc = jnp.exp(gc_col[un])
            u[un] = _mm(tinv[un], vb[un])
            w[un] = _mm(tinv[un], kb[un] * egc)
            attn[un] = _bf(jnp.where(m_incl, qk[un] * decay[un], 0.0))
            qg[un] = _bf(q[un] * egc)
            gl[un] = (gc_col[un][CHUNK - 1:CHUNK, :], gc_col[un][TBLK - 1:TBLK, :])
            kdt[un] = _bf((k[un] * jnp.exp(jnp.where(first, gl[un][0], gl[un][1]) - gc_col[un])).T)

    s = {h: state_scr[h] for h in heads}

    def scan(j):
        outs = {h: [] for h in heads}
        for c in range(2):
            r0, r1 = c * CHUNK, (c + 1) * CHUNK
            v_blk = {}
            for h in heads:
                un = (j, h)
                v_new = u[un][r0:r1] - _mm(w[un][r0:r1], s[h])
                v_blk[h] = _bf(jnp.concatenate([v_new, pad] if c == 0 else [pad, v_new], axis=0))
            for h in heads:
                un = (j, h)
                outs[h].append(_mm(qg[un][r0:r1], s[h]) + _mm(attn[un][r0:r1, :], v_blk[h]))
                s[h] = s[h] * jnp.exp(gl[un][c]) + _mm(kdt[un], v_blk[h])
        for h in heads:
            lanes = slice(h * DN_HEAD_DIM, (h + 1) * DN_HEAD_DIM)
            rows = slice(j * TBLK, (j + 1) * TBLK)
            o = jnp.concatenate(outs[h], axis=0)
            o_ref[rows, lanes] = _bf(_rms(o, nw_ref[...]) * _silu(gate_ref[rows, lanes].astype(F32)))

    for j in range(DN_TILES):
        prepare(j)
    gram(units)
    solve(units)
    for j in range(DN_TILES):
        scan(j)
    for h in heads:
        state_scr[h] = s[h]
    last = DN_TILES * TBLK
    halo_scr[...] = qkv_ref[last - 16:last, :].astype(F32)[16 - HALO:16]


def _dn_mixer(big, small, conv_w, prow, norm_w, batch, seq):
    rows = DN_TILES * TBLK
    nd = seq // rows
    row_map = lambda b, d: (b * nd + d, 0)
    return pl.pallas_call(
        _dn_kernel,
        grid=(batch, nd),
        in_specs=[
            pl.BlockSpec((rows, 3 * MIX_W), lambda b, d: (b * nd + d, COL_DN_QKV // (3 * MIX_W))),
            pl.BlockSpec((rows, MIX_W), lambda b, d: (b * nd + d, COL_DN_GATE // MIX_W)),
            pl.BlockSpec((rows, LANES), row_map),
            pl.BlockSpec((CONV_WIDTH, 3 * MIX_W), lambda b, d: (0, 0)),
            pl.BlockSpec((8, LANES), lambda b, d: (0, 0)),
            pl.BlockSpec((1, DN_HEAD_DIM), lambda b, d: (0, 0)),
        ],
        out_specs=pl.BlockSpec((rows, MIX_W), row_map),
        out_shape=jax.ShapeDtypeStruct((batch * seq, MIX_W), BF16),
        scratch_shapes=[pltpu.VMEM((HALO, 3 * MIX_W), F32), pltpu.VMEM((DN_HEADS, DN_HEAD_DIM, DN_HEAD_DIM), F32)],
        compiler_params=_cparams("parallel", "arbitrary"),
        name="deltanet",
    )(big, big, small, conv_w, prow, norm_w)


SB_TILE = 128
SB_TQ = 256
SB_SUB = SB_TQ // SB_TILE
SB_STRIP = 256
SB_HEADS_PER_STEP = 4
LOG2E = 1.4426950408889634
SB_LAG = 4


def _sb_kernel(q_ref, k_ref, v_ref, w_ref, o_ref, r_scr, acc_scr, za_scr, zb_scr, wgt_scr):
    qi = pl.program_id(2)
    row = lax.broadcasted_iota(jnp.int32, (SB_STRIP, SB_TILE), 0)
    col = lax.broadcasted_iota(jnp.int32, (SB_STRIP, SB_TILE), 1)
    wmat = w_ref[...]
    r_scr[...] = jnp.zeros_like(r_scr)
    acc_scr[...] = jnp.zeros_like(acc_scr)
    strips = range(SB_TQ // SB_STRIP)
    heads = range(SB_HEADS_PER_STEP)
    pairs = range(SB_HEADS_PER_STEP // 2)
    order = list(reversed(range(SB_SUB)))
    qh = {}
    for s in strips:
        for h in heads:
            lanes = slice((h // 2) * LANES, (h // 2 + 1) * LANES)
            q = q_ref[s * SB_STRIP:(s + 1) * SB_STRIP, lanes].astype(F32) * (SB_HEAD_DIM ** -0.5)
            qh[s, h] = _bf(jnp.where((col >= SB_HEAD_DIM) == (h % 2 == 1), q, 0.0))

    def chain_id(t, s, h):
        return (t * len(strips) + s) * SB_HEADS_PER_STEP + h

    def masked_out(t, s, diagonal):
        return diagonal and t * SB_TILE >= (s + 1) * SB_STRIP

    def block_rows(kb_lo):
        return pl.ds(pl.multiple_of(kb_lo * SB_TILE, SB_TILE), SB_SUB * SB_TILE)

    def score_ops(kb_lo, z_dst, diagonal):
        def one(t, s, h):
            def emit():
                rows = pl.ds(pl.multiple_of((kb_lo + t) * SB_TILE, SB_TILE), SB_TILE)
                kblk = k_ref[rows, (h // 2) * LANES:(h // 2 + 1) * LANES]
                z_dst[chain_id(t, s, h)] = _mm_nt(qh[s, h], kblk) * LOG2E
            return emit
        return [one(t, s, h) for t in order for s in strips for h in heads if not masked_out(t, s, diagonal)]

    def value_ops(kb_lo):
        def one(h):
            def emit():
                vblk = v_ref[block_rows(kb_lo), (h // 2) * LANES:(h // 2 + 1) * LANES]
                acc_scr[h] += jnp.dot(wgt_scr[h], vblk, preferred_element_type=F32)
            return emit
        return [one(h) for h in heads]

    def weight_stage(z_src, diagonal, extras):
        chains = [(t, s, h) for t in order for s in strips for h in heads]
        cs, tri = {}, {}

        def cumsum(c):
            t, s, _ = c
            z = z_src[chain_id(*c)]
            neg_abs = pltpu.bitcast(pltpu.bitcast(z, jnp.uint32) | jnp.uint32(0x80000000), F32)
            sp = jnp.maximum(z, 0.0) + jnp.log2(1.0 + jnp.exp2(neg_abs))
            if diagonal:
                tri[c] = col + t * SB_TILE < row + s * SB_STRIP
                sp = jnp.where(tri[c], sp, 0.0)
            cs[c] = jnp.dot(_bf(sp), wmat, preferred_element_type=F32)

        def weigh(c):
            t, s, h = c
            rows = slice(s * SB_STRIP, (s + 1) * SB_STRIP)
            lanes = slice(t * SB_TILE, (t + 1) * SB_TILE)
            wgt = jnp.exp2(z_src[chain_id(*c)] - cs[c][:, :SB_TILE] - r_scr[h, rows])
            wgt_scr[h, rows, lanes] = _bf(jnp.where(tri[c], wgt, 0.0) if diagonal else wgt)
            r_scr[h, rows] += cs[c][:, SB_TILE:]

        live = [c for c in chains if not masked_out(c[0], c[1], diagonal)]
        for t, s, h in chains:
            if masked_out(t, s, diagonal):
                wgt_scr[h, s * SB_STRIP:(s + 1) * SB_STRIP, t * SB_TILE:(t + 1) * SB_TILE] = (
                    jnp.zeros((SB_STRIP, SB_TILE), BF16))
        per_slot = -(-len(extras) // len(live))
        for i, c in enumerate(live):
            cumsum(c)
            for emit in extras[i * per_slot:(i + 1) * per_slot]:
                emit()
            if i >= SB_LAG:
                weigh(live[i - SB_LAG])
        for c in live[-SB_LAG:]:
            weigh(c)

    def step(cur, z_src, z_dst, issue_next=True):
        extras = value_ops((cur + 1) * SB_SUB)
        if issue_next:
            extras = extras + score_ops(jnp.maximum(cur - 1, 0) * SB_SUB, z_dst, False)
        weight_stage(z_src, False, extras)

    for emit in score_ops(qi * SB_SUB, za_scr, True):
        emit()
    weight_stage(za_scr, True, score_ops(jnp.maximum(qi - 1, 0) * SB_SUB, zb_scr, False))

    def body(j, carry):
        step(qi - 1 - 2 * j, zb_scr, za_scr)
        step(qi - 2 - 2 * j, za_scr, zb_scr)
        return carry

    lax.fori_loop(0, qi // 2, body, 0)

    @pl.when(qi % 2 == 1)
    def _():
        step(0, zb_scr, za_scr, issue_next=False)

    for emit in value_ops(0):
        emit()
    lane = lax.broadcasted_iota(jnp.int32, (SB_TQ, SB_TILE), 1)
    for p in range(SB_HEADS_PER_STEP // 2):
        o_ref[:, p * LANES:(p + 1) * LANES] = _bf(jnp.where(lane < SB_HEAD_DIM, acc_scr[2 * p], acc_scr[2 * p + 1]))


def _sb_mixer(big, wmat, batch, seq):
    nq = seq // SB_TQ
    width = SB_HEADS_PER_STEP * SB_HEAD_DIM
    groups = MIX_W // width
    qoff = COL_SB_QKV // width
    return pl.pallas_call(
        _sb_kernel,
        grid=(batch, groups, nq),
        in_specs=[
            pl.BlockSpec((SB_TQ, width), lambda b, p, i: (b * nq + i, qoff + p)),
            pl.BlockSpec((seq, width), lambda b, p, i: (b, qoff + groups + p)),
            pl.BlockSpec((seq, width), lambda b, p, i: (b, qoff + 2 * groups + p)),
            pl.BlockSpec((SB_TILE, 2 * SB_TILE), lambda b, p, i: (0, 0)),
        ],
        out_specs=pl.BlockSpec((SB_TQ, width), lambda b, p, i: (b * nq + i, p)),
        out_shape=jax.ShapeDtypeStruct((batch * seq, MIX_W), BF16),
        scratch_shapes=[pltpu.VMEM((SB_HEADS_PER_STEP, SB_TQ, SB_TILE), F32),
                        pltpu.VMEM((SB_HEADS_PER_STEP, SB_TQ, SB_TILE), F32),
                        pltpu.VMEM((SB_SUB * (SB_TQ // SB_STRIP) * SB_HEADS_PER_STEP, SB_STRIP, SB_TILE), F32),
                        pltpu.VMEM((SB_SUB * (SB_TQ // SB_STRIP) * SB_HEADS_PER_STEP, SB_STRIP, SB_TILE), F32),
                        pltpu.VMEM((SB_HEADS_PER_STEP, SB_TQ, SB_SUB * SB_TILE), BF16)],
        compiler_params=_cparams("parallel", "parallel", "arbitrary"),
        name="stickbreak",
    )(big, big, big, wmat)


SSD_TILES = 2
GROUP_W = MIX_W // SSM_GROUPS
HEADS_PER_GROUP = SSM_HEADS // SSM_GROUPS


def _ssd_kernel(z_ref, xbc_ref, sm_ref, cw_ref, cb_ref, prow_ref, drow_ref, nw_ref, e_ref, o_ref,
                halo_scr, state_scr):
    @pl.when(pl.program_id(1) == 0)
    def _():
        halo_scr[...] = jnp.zeros_like(halo_scr)
        state_scr[...] = jnp.zeros_like(state_scr)

    row, col, m_incl, _ = _chunk_masks()
    tril = jnp.where(m_incl, 1.0, 0.0)
    first = row[:, 0:1] < CHUNK
    gcol = lax.broadcasted_iota(jnp.int32, (TBLK, GROUP_W), 1)

    gn = SSM_GROUPS * SSM_STATE
    e = e_ref[...]
    for j in range(SSD_TILES):
        rows = slice(j * TBLK, (j + 1) * TBLK)
        cur = xbc_ref[rows, :].astype(F32)
        halo = halo_scr[...] if j == 0 else xbc_ref[j * TBLK - 16:j * TBLK, :].astype(F32)[16 - HALO:16]
        xbc = _silu(_causal_conv(halo, cur, cw_ref[...], cb_ref[...]))
        xs = xbc[:, :MIX_W]
        bm = xbc[:, MIX_W:MIX_W + gn]
        cm = xbc[:, MIX_W + gn:]

        dt_all = _softplus(sm_ref[rows, :] + prow_ref[1:2, :])
        a_all = -jnp.exp(prow_ref[0:1, :]) * dt_all
        acum_all = _mm_sel_lhs(tril, a_all)
        acum_t = acum_all.T
        dt_exp = _mm_sel_rhs(dt_all, e)
        acum_exp = _mm_sel_rhs(acum_all, e)
        xdt = xs * dt_exp

        for g in range(SSM_GROUPS):
            c_g = cm[:, g * SSM_STATE:(g + 1) * SSM_STATE]
            b_g = bm[:, g * SSM_STATE:(g + 1) * SSM_STATE]
            scores = _mm_nt(c_g, b_g)
            lo, hi = g * GROUP_W, (g + 1) * GROUP_W
            x_g = xdt[:, lo:hi]
            a_g = acum_exp[:, lo:hi]
            y_g = jnp.zeros((TBLK, GROUP_W), F32)
            for hh in range(HEADS_PER_GROUP):
                h = g * HEADS_PER_GROUP + hh
                a_col = acum_exp[:, h * SSM_HEAD_DIM:h * SSM_HEAD_DIM + 1]
                a_row = acum_t[LANE_SSM_DT + h:LANE_SSM_DT + h + 1, :]
                lmat = jnp.exp(jnp.where(m_incl, a_col - a_row, -jnp.inf))
                y_g = jnp.where(gcol // SSM_HEAD_DIM == hh, _mm(scores * lmat, x_g), y_g)

            b_t = b_g.T
            a_last0 = a_g[CHUNK - 1:CHUNK, :]
            a_last1 = a_g[TBLK - 1:TBLK, :]
            x_sc = x_g * jnp.exp(jnp.where(first, a_last0, a_last1) - a_g)
            st = state_scr[g]
            yoff = []
            for c, a_last in ((0, a_last0), (1, a_last1)):
                r0, r1 = c * CHUNK, (c + 1) * CHUNK
                yoff.append(_mm(c_g[r0:r1], st) * jnp.exp(a_g[r0:r1]))
                st = st * jnp.exp(a_last) + _mm(b_t, jnp.where(first == (c == 0), x_sc, 0.0))
            state_scr[g] = st
            y_g = y_g + jnp.concatenate(yoff, axis=0) + xs[:, lo:hi] * drow_ref[:, lo:hi]
            y_g = y_g * _silu(z_ref[rows, lo:hi].astype(F32))
            o_ref[rows, lo:hi] = _bf(_rms(y_g, nw_ref[:, lo:hi]))
    last = SSD_TILES * TBLK
    halo_scr[...] = xbc_ref[last - 16:last, :].astype(F32)[16 - HALO:16]


def _ssd_mixer(big, small, conv_w, conv_b, prow, drow, norm_w, emat, batch, seq):
    rows = SSD_TILES * TBLK
    nd = seq // rows
    conv_dim = MIX_W + 2 * SSM_GROUPS * SSM_STATE
    row_map = lambda b, d: (b * nd + d, 0)
    const = lambda b, d: (0, 0)
    return pl.pallas_call(
        _ssd_kernel,
        grid=(batch, nd),
        in_specs=[
            pl.BlockSpec((rows, MIX_W), lambda b, d: (b * nd + d, COL_SSM_Z // MIX_W)),
            pl.BlockSpec((rows, conv_dim), lambda b, d: (b * nd + d, COL_SSM_XBC // conv_dim)),
            pl.BlockSpec((rows, LANES), row_map),
            pl.BlockSpec((CONV_WIDTH, conv_dim), const),
            pl.BlockSpec((1, conv_dim), const),
            pl.BlockSpec((8, LANES), const),
            pl.BlockSpec((1, MIX_W), const),
            pl.BlockSpec((1, MIX_W), const),
            pl.BlockSpec((LANES, MIX_W), const),
        ],
        out_specs=pl.BlockSpec((rows, MIX_W), row_map),
        out_shape=jax.ShapeDtypeStruct((batch * seq, MIX_W), BF16),
        scratch_shapes=[pltpu.VMEM((HALO, conv_dim), F32), pltpu.VMEM((SSM_GROUPS, SSM_STATE, GROUP_W), F32)],
        compiler_params=_cparams("parallel", "arbitrary"),
        name="ssd",
    )(big, big, small, conv_w, conv_b, prow, drow, norm_w, emat)


def _merge_kernel(x_ref, odn_ref, osb_ref, ossm_ref, g0_ref, g1_ref, g2_ref, wb_ref, wo_ref, o_ref):
    m = _sigmoid(g0_ref[...].astype(F32)) * jnp.dot(odn_ref[...], wb_ref[0], preferred_element_type=F32)
    m = m + _sigmoid(g1_ref[...].astype(F32)) * jnp.dot(osb_ref[...], wb_ref[1], preferred_element_type=F32)
    m = m + _sigmoid(g2_ref[...].astype(F32)) * jnp.dot(ossm_ref[...], wb_ref[2], preferred_element_type=F32)
    o_ref[...] = x_ref[...] + jnp.dot(_bf(m), wo_ref[...], preferred_element_type=F32)


def _merge(x, o_dn, o_sb, o_ssm, big, w_branch, w_out, layer, tm=512):
    t = x.shape[0]
    tm = min(tm, t)
    rows = lambda i: (i, 0)
    gcol = COL_GATES // D_MODEL
    return pl.pallas_call(
        _merge_kernel,
        grid=(t // tm,),
        in_specs=[
            pl.BlockSpec((tm, D_MODEL), rows),
            pl.BlockSpec((tm, MIX_W), rows),
            pl.BlockSpec((tm, MIX_W), rows),
            pl.BlockSpec((tm, MIX_W), rows),
            pl.BlockSpec((tm, D_MODEL), lambda i: (i, gcol)),
            pl.BlockSpec((tm, D_MODEL), lambda i: (i, gcol + 1)),
            pl.BlockSpec((tm, D_MODEL), lambda i: (i, gcol + 2)),
            pl.BlockSpec((None, 3, MIX_W, D_MODEL), lambda i: (layer, 0, 0, 0)),
            pl.BlockSpec((None, D_MODEL, D_MODEL), lambda i: (layer, 0, 0)),
        ],
        out_specs=pl.BlockSpec((tm, D_MODEL), rows),
        out_shape=jax.ShapeDtypeStruct((t, D_MODEL), F32),
        compiler_params=_cparams("parallel"),
        name="merge",
    )(x, o_dn, o_sb, o_ssm, big, big, big, w_branch, w_out)


def _mlp_kernel(x_ref, g_ref, wu_ref, wd_ref, gf_ref, o_ref, xn_scr, acc_scr, *, final_norm):
    j = pl.program_id(1)

    @pl.when(j == 0)
    def _():
        xn_scr[...] = _bf(_rms(x_ref[...], g_ref[...]))
        acc_scr[...] = jnp.zeros_like(acc_scr)

    h = jnp.dot(xn_scr[...], wu_ref[...], preferred_element_type=F32)
    h = jnp.square(jnp.maximum(h, 0.0))
    acc_scr[...] += jnp.dot(_bf(h), wd_ref[...], preferred_element_type=F32)

    @pl.when(j == pl.num_programs(1) - 1)
    def _():
        y = x_ref[...] + acc_scr[...]
        o_ref[...] = _rms(y, gf_ref[...]) if final_norm else y


def _mlp(x, g, w_up, w_down, g_final, layer, final_norm, tm=1024, tf=2048):
    t = x.shape[0]
    tm = min(tm, t)
    return pl.pallas_call(
        functools.partial(_mlp_kernel, final_norm=final_norm),
        grid=(t // tm, D_FF // tf),
        in_specs=[
            pl.BlockSpec((tm, D_MODEL), lambda i, j: (i, 0)),
            pl.BlockSpec((1, D_MODEL), lambda i, j: (0, 0)),
            pl.BlockSpec((None, D_MODEL, tf), lambda i, j: (layer, 0, j)),
            pl.BlockSpec((None, tf, D_MODEL), lambda i, j: (layer, j, 0)),
            pl.BlockSpec((1, D_MODEL), lambda i, j: (0, 0)),
        ],
        out_specs=pl.BlockSpec((tm, D_MODEL), lambda i, j: (i, 0)),
        out_shape=jax.ShapeDtypeStruct((t, D_MODEL), F32),
        scratch_shapes=[pltpu.VMEM((tm, D_MODEL), BF16), pltpu.VMEM((tm, D_MODEL), F32)],
        compiler_params=_cparams("parallel", "arbitrary"),
        name="mlp",
    )(x, g, w_up, w_down, g_final)


def _lane_row(values, lane0):
    return jnp.zeros((LANES,), F32).at[lane0:lane0 + values.shape[0]].set(values.astype(F32))


def _param_rows(a_log, dt_bias, lane0):
    rows = jnp.zeros((8, LANES), F32)
    return rows.at[0].set(_lane_row(a_log, lane0)).at[1].set(_lane_row(dt_bias, lane0))


def _sb_cumsum_matrix():
    j = jnp.arange(SB_TILE)[:, None]
    s = jnp.arange(SB_TILE)[None, :]
    return jnp.concatenate([(j >= s).astype(BF16), jnp.ones((SB_TILE, SB_TILE), BF16)], axis=1)


def _ssm_expand_matrix():
    lane = jnp.arange(LANES)[:, None]
    ch = jnp.arange(MIX_W)[None, :]
    return (lane == LANE_SSM_DT + ch // SSM_HEAD_DIM).astype(BF16)


def _split_w_in(w):
    w = _bf(w)
    big = jnp.concatenate([w[..., 0:4096], w[..., 4112:10256], w[..., 10272:13344]], axis=-1)
    pad = jnp.zeros(w.shape[:-1] + (LANES - 32,), w.dtype)
    small = jnp.concatenate([w[..., 4096:4112], w[..., 10256:10272], pad], axis=-1)
    return big, small


def kernel(x, norm_mix, w_in, dn_conv_w, dn_a_log, dn_dt_bias, dn_norm_w, ssm_conv_w, ssm_conv_b, ssm_a_log,
           ssm_dt_bias, ssm_d, ssm_norm_w, w_branch, w_out, norm_mlp, w_up, w_down, norm_final):
    batch, seq, _ = x.shape
    depth = w_in.shape[0]
    h = x.reshape(batch * seq, D_MODEL)
    sb_w = _sb_cumsum_matrix()
    emat = _ssm_expand_matrix()
    g_final = norm_final.reshape(1, D_MODEL)
    w_big, w_small = _split_w_in(w_in)
    w_branch, w_out, w_up, w_down = _bf(w_branch), _bf(w_out), _bf(w_up), _bf(w_down)
    for l in range(depth):
        big, small = _inproj(h, norm_mix[l].reshape(1, D_MODEL), w_big, w_small, l)
        o_dn = _dn_mixer(big, small, dn_conv_w[l], _param_rows(dn_a_log[l], dn_dt_bias[l], LANE_DN_A),
                         dn_norm_w[l].reshape(1, DN_HEAD_DIM), batch, seq)
        o_sb = _sb_mixer(big, sb_w, batch, seq)
        o_ssm = _ssd_mixer(big, small, ssm_conv_w[l], ssm_conv_b[l].reshape(1, -1),
                           _param_rows(ssm_a_log[l], ssm_dt_bias[l], LANE_SSM_DT),
                           jnp.repeat(ssm_d[l], SSM_HEAD_DIM).reshape(1, MIX_W),
                           ssm_norm_w[l].reshape(1, MIX_W), emat, batch, seq)
        h = _merge(h, o_dn, o_sb, o_ssm, big, w_branch, w_out, l)
        h = _mlp(h, norm_mlp[l].reshape(1, D_MODEL), w_up, w_down, g_final, l, final_norm=(l == depth - 1))
    return h.reshape(batch, seq, D_MODEL)
```

```python
import functools

import jax
import jax.numpy as jnp
from jax import lax
from jax.experimental import pallas as pl
from jax.experimental.pallas import tpu as pltpu

F32 = jnp.float32
BF16 = jnp.bfloat16

D_MODEL = 1024
MIX_W = D_MODEL
DN_HEADS = 8
DN_HEAD_DIM = 128
SB_HEADS = 16
SB_HEAD_DIM = 64
SSM_HEADS = 16
SSM_HEAD_DIM = 64
SSM_STATE = 128
SSM_GROUPS = 4
CHUNK = 64
D_FF = 4 * D_MODEL
EPS = 1e-6
CONV_WIDTH = 4

LANES = 128
TBLK = 2 * CHUNK
HALO = 8
DN_BASE_BLOCK = 8
DN_TILES = 2

COL_DN_QKV = 0
COL_DN_GATE = 3072
COL_SB_QKV = 4096
COL_SSM_Z = 7168
COL_SSM_XBC = 8192
COL_GATES = 10240
N_BIG = 13312
LANE_DN_A = 0
LANE_DN_B = 8
LANE_SSM_DT = 16

VMEM_LIMIT = 56 * 1024 * 1024


def _cparams(*sem):
    return pltpu.CompilerParams(dimension_semantics=sem, vmem_limit_bytes=VMEM_LIMIT)


def _bf(x):
    return x.astype(BF16)


def _mm(a, b):
    return jnp.dot(_bf(a), _bf(b), preferred_element_type=F32)


def _mm_nt(a, b):
    return lax.dot_general(_bf(a), _bf(b), (((1,), (1,)), ((), ())), preferred_element_type=F32)


def _split2(a):
    hi = _bf(a)
    lo = _bf(a - hi.astype(F32))
    return hi, lo


def _split3(a):
    hi = _bf(a)
    r = a - hi.astype(F32)
    mid = _bf(r)
    lo = _bf(r - mid.astype(F32))
    return hi, mid, lo


def _mm_x3s(a_split, b_split):
    ah, al = a_split
    bh, bl = b_split
    lhs = jnp.concatenate([ah, al, ah], axis=1)
    rhs = jnp.concatenate([bh, bh, bl], axis=0)
    return jnp.dot(lhs, rhs, preferred_element_type=F32)


def _mm_sel_lhs(sel, b):
    s = _bf(sel)
    bh, bm, bl = _split3(b)
    return jnp.dot(jnp.concatenate([s, s, s], axis=1), jnp.concatenate([bh, bm, bl], axis=0),
                   preferred_element_type=F32)


def _mm_sel_rhs(a, sel_bf):
    ah, am, al = _split3(a)
    return jnp.dot(jnp.concatenate([ah, am, al], axis=1), jnp.concatenate([sel_bf, sel_bf, sel_bf], axis=0),
                   preferred_element_type=F32)


def _softplus(x):
    return jnp.maximum(x, 0.0) + jnp.log(1.0 + jnp.exp(-jnp.abs(x)))


def _sigmoid(x):
    return 1.0 / (1.0 + jnp.exp(-x))


def _silu(x):
    return x * _sigmoid(x)


def _rms(x, w):
    return x * lax.rsqrt(jnp.mean(x * x, axis=-1, keepdims=True) + EPS) * w


def _causal_conv(halo, cur, w, bias):
    rows = cur.shape[0]
    xx = jnp.concatenate([halo, cur], axis=0)
    y = w[CONV_WIDTH - 1:CONV_WIDTH, :] * cur
    for k in range(CONV_WIDTH - 1):
        shifted = pltpu.roll(xx, CONV_WIDTH - 1 - k, 0)[HALO:HALO + rows]
        y = y + w[k:k + 1, :] * shifted
    if bias is not None:
        y = y + bias
    return y


def _chunk_masks():
    row = lax.broadcasted_iota(jnp.int32, (TBLK, TBLK), 0)
    col = lax.broadcasted_iota(jnp.int32, (TBLK, TBLK), 1)
    same = (row >= CHUNK) == (col >= CHUNK)
    return row, col, same & (col <= row), same & (col < row)


def _inproj_kernel(x_ref, g_ref, wb_ref, ws_ref, big_ref, small_ref, xn_scr):
    @pl.when(pl.program_id(1) == 0)
    def _():
        xn_scr[...] = _bf(_rms(x_ref[...], g_ref[...]))
        small_ref[...] = jnp.dot(xn_scr[...], ws_ref[...], preferred_element_type=F32)

    big_ref[...] = _bf(jnp.dot(xn_scr[...], wb_ref[...], preferred_element_type=F32))


def _inproj(x, g, w_big, w_small, layer, tm=2048, tn=1024):
    t = x.shape[0]
    tm = min(tm, t)
    return pl.pallas_call(
        _inproj_kernel,
        grid=(t // tm, N_BIG // tn),
        in_specs=[
            pl.BlockSpec((tm, D_MODEL), lambda i, j: (i, 0)),
            pl.BlockSpec((1, D_MODEL), lambda i, j: (0, 0)),
            pl.BlockSpec((None, D_MODEL, tn), lambda i, j: (layer, 0, j)),
            pl.BlockSpec((None, D_MODEL, LANES), lambda i, j: (layer, 0, 0)),
        ],
        out_specs=[
            pl.BlockSpec((tm, tn), lambda i, j: (i, j)),
            pl.BlockSpec((tm, LANES), lambda i, j: (i, 0)),
        ],
        out_shape=[jax.ShapeDtypeStruct((t, N_BIG), BF16), jax.ShapeDtypeStruct((t, LANES), F32)],
        scratch_shapes=[pltpu.VMEM((tm, D_MODEL), BF16)],
        compiler_params=_cparams("parallel", "arbitrary"),
        name="inproj",
    )(x, g, w_big, w_small)


def _dn_kernel(qkv_ref, gate_ref, sm_ref, cw_ref, prow_ref, nw_ref, o_ref, halo_scr, state_scr):
    @pl.when(pl.program_id(1) == 0)
    def _():
        halo_scr[...] = jnp.zeros_like(halo_scr)
        state_scr[...] = jnp.zeros_like(state_scr)

    row, col, m_incl, m_strict = _chunk_masks()
    tril = jnp.where(m_incl, 1.0, 0.0)
    eye = jnp.where(row == col, 1.0, 0.0)
    first = row[:, 0:1] < CHUNK

    units = [(j, h) for j in range(DN_TILES) for h in range(DN_HEADS)]
    heads = range(DN_HEADS)
    q, k, v, kb, vb, decay, gc_col, kk, qk = {}, {}, {}, {}, {}, {}, {}, {}, {}
    low, xs, tinv, cd = {}, {}, {}, {}
    u, w, attn, qg, kdt, gl = {}, {}, {}, {}, {}, {}
    pad = jnp.zeros((CHUNK, DN_HEAD_DIM), F32)

    def same_block(size):
        shift = size.bit_length() - 1
        return (row >> shift) == (col >> shift)

    def prepare(j):
        rows = slice(j * TBLK, (j + 1) * TBLK)
        sm = sm_ref[rows, :]
        g_all = -jnp.exp(prow_ref[0:1, :]) * _softplus(sm + prow_ref[1:2, :])
        beta_all = _sigmoid(sm)
        gc_all = _mm_sel_lhs(tril, g_all)
        gct_all = gc_all.T
        for h in heads:
            def conv_silu(seg):
                lanes = slice(seg * MIX_W + h * DN_HEAD_DIM, seg * MIX_W + (h + 1) * DN_HEAD_DIM)
                if j == 0:
                    halo = halo_scr[:, lanes]
                else:
                    halo = qkv_ref[j * TBLK - 16:j * TBLK, lanes].astype(F32)[16 - HALO:16]
                return _silu(_causal_conv(halo, qkv_ref[rows, lanes].astype(F32), cw_ref[:, lanes], None))

            un = (j, h)
            qh = conv_silu(0)
            kh = conv_silu(1)
            v[un] = conv_silu(2)
            q[un] = qh * lax.rsqrt(jnp.sum(qh * qh, axis=-1, keepdims=True) + EPS) * (DN_HEAD_DIM ** -0.5)
            k[un] = kh * lax.rsqrt(jnp.sum(kh * kh, axis=-1, keepdims=True) + EPS)
            gc_col[un] = gc_all[:, LANE_DN_A + h:LANE_DN_A + h + 1]
            beta = beta_all[:, LANE_DN_B + h:LANE_DN_B + h + 1]
            gc_row = gct_all[LANE_DN_A + h:LANE_DN_A + h + 1, :]
            decay[un] = jnp.exp(jnp.where(m_incl, gc_col[un] - gc_row, -jnp.inf))
            kb[un] = k[un] * beta
            vb[un] = v[un] * beta
    def gram(units):
        for un in units:
            kk[un] = _mm_nt(kb[un], k[un])
            qk[un] = _mm_nt(q[un], k[un])

    def solve(units):
        for un in units:
            low[un] = jnp.where(m_strict, kk[un] * decay[un], 0.0)
            x = -jnp.where(same_block(DN_BASE_BLOCK), low[un], 0.0)
            xs[un] = _split2(x)
            tinv[un] = eye + x
        for _ in range(2):
            for un in units:
                xs[un] = _split2(_mm_x3s(xs[un], xs[un]))
            for un in units:
                tinv[un] = tinv[un] + _mm_x3s(_split2(tinv[un]), xs[un])
        size = DN_BASE_BLOCK
        while size < CHUNK:
            off_diag = same_block(2 * size) & jnp.logical_not(same_block(size))
            for un in units:
                cd[un] = _mm(jnp.where(off_diag, low[un], 0.0), tinv[un])
            for un in units:
                tinv[un] = tinv[un] - _mm(tinv[un], cd[un])
            size *= 2
        for un in units:
            egc = jnp.exp(gc_col[un])
            u[un] = _mm(tinv[un], vb[un])
            w[un] = _mm(tinv[un], kb[un] * egc)
            attn[un] = _bf(jnp.where(m_incl, qk[un] * decay[un], 0.0))
            qg[un] = _bf(q[un] * egc)
            gl[un] = (gc_col[un][CHUNK - 1:CHUNK, :], gc_col[un][TBLK - 1:TBLK, :])
            kdt[un] = _bf((k[un] * jnp.exp(jnp.where(first, gl[un][0], gl[un][1]) - gc_col[un])).T)

    s = {h: state_scr[h] for h in heads}

    def scan(j):
        outs = {h: [] for h in heads}
        for c in range(2):
            r0, r1 = c * CHUNK, (c + 1) * CHUNK
            v_blk = {}
            for h in heads:
                un = (j, h)
                v_new = u[un][r0:r1] - _mm(w[un][r0:r1], s[h])
                v_blk[h] = _bf(jnp.concatenate([v_new, pad] if c == 0 else [pad, v_new], axis=0))
            for h in heads:
                un = (j, h)
                outs[h].append(_mm(qg[un][r0:r1], s[h]) + _mm(attn[un][r0:r1, :], v_blk[h]))
                s[h] = s[h] * jnp.exp(gl[un][c]) + _mm(kdt[un], v_blk[h])
        for h in heads:
            lanes = slice(h * DN_HEAD_DIM, (h + 1) * DN_HEAD_DIM)
            rows = slice(j * TBLK, (j + 1) * TBLK)
            o = jnp.concatenate(outs[h], axis=0)
            o_ref[rows, lanes] = _bf(_rms(o, nw_ref[...]) * _silu(gate_ref[rows, lanes].astype(F32)))

    for j in range(DN_TILES):
        prepare(j)
    gram(units)
    solve(units)
    for j in range(DN_TILES):
        scan(j)
    for h in heads:
        state_scr[h] = s[h]
    last = DN_TILES * TBLK
    halo_scr[...] = qkv_ref[last - 16:last, :].astype(F32)[16 - HALO:16]


def _dn_mixer(big, small, conv_w, prow, norm_w, batch, seq):
    rows = DN_TILES * TBLK
    nd = seq // rows
    row_map = lambda b, d: (b * nd + d, 0)
    return pl.pallas_call(
        _dn_kernel,
        grid=(batch, nd),
        in_specs=[
            pl.BlockSpec((rows, 3 * MIX_W), lambda b, d: (b * nd + d, COL_DN_QKV // (3 * MIX_W))),
            pl.BlockSpec((rows, MIX_W), lambda b, d: (b * nd + d, COL_DN_GATE // MIX_W)),
            pl.BlockSpec((rows, LANES), row_map),
            pl.BlockSpec((CONV_WIDTH, 3 * MIX_W), lambda b, d: (0, 0)),
            pl.BlockSpec((8, LANES), lambda b, d: (0, 0)),
            pl.BlockSpec((1, DN_HEAD_DIM), lambda b, d: (0, 0)),
        ],
        out_specs=pl.BlockSpec((rows, MIX_W), row_map),
        out_shape=jax.ShapeDtypeStruct((batch * seq, MIX_W), BF16),
        scratch_shapes=[pltpu.VMEM((HALO, 3 * MIX_W), F32), pltpu.VMEM((DN_HEADS, DN_HEAD_DIM, DN_HEAD_DIM), F32)],
        compiler_params=_cparams("parallel", "arbitrary"),
        name="deltanet",
    )(big, big, small, conv_w, prow, norm_w)


SB_TILE = 128
SB_TQ = 256
SB_SUB = SB_TQ // SB_TILE
SB_STRIP = 256
SB_HEADS_PER_STEP = 4
LOG2E = 1.4426950408889634
SB_LAG = 4


def _sb_kernel(q_ref, k_ref, v_ref, w_ref, o_ref, r_scr, acc_scr, za_scr, zb_scr, wgt_scr):
    qi = pl.program_id(2)
    row = lax.broadcasted_iota(jnp.int32, (SB_STRIP, SB_TILE), 0)
    col = lax.broadcasted_iota(jnp.int32, (SB_STRIP, SB_TILE), 1)
    wmat = w_ref[...]
    r_scr[...] = jnp.zeros_like(r_scr)
    acc_scr[...] = jnp.zeros_like(acc_scr)
    strips = range(SB_TQ // SB_STRIP)
    heads = range(SB_HEADS_PER_STEP)
    pairs = range(SB_HEADS_PER_STEP // 2)
    order = list(reversed(range(SB_SUB)))
    qh = {}
    for s in strips:
        for h in heads:
            lanes = slice((h // 2) * LANES, (h // 2 + 1) * LANES)
            q = q_ref[s * SB_STRIP:(s + 1) * SB_STRIP, lanes].astype(F32) * (SB_HEAD_DIM ** -0.5)
            qh[s, h] = _bf(jnp.where((col >= SB_HEAD_DIM) == (h % 2 == 1), q, 0.0))

    def chain_id(t, s, h):
        return (t * len(strips) + s) * SB_HEADS_PER_STEP + h

    def masked_out(t, s, diagonal):
        return diagonal and t * SB_TILE >= (s + 1) * SB_STRIP

    def block_rows(kb_lo):
        return pl.ds(pl.multiple_of(kb_lo * SB_TILE, SB_TILE), SB_SUB * SB_TILE)

    def score_ops(kb_lo, z_dst, diagonal):
        def one(t, s, p):
            def emit():
                rows = pl.ds(pl.multiple_of((kb_lo + t) * SB_TILE, SB_TILE), SB_TILE)
                kblk = k_ref[rows, p * LANES:(p + 1) * LANES]
                zz = _mm_nt(jnp.concatenate([qh[s, 2 * p], qh[s, 2 * p + 1]], axis=0), kblk) * LOG2E
                z_dst[chain_id(t, s, 2 * p)] = zz[:SB_STRIP]
                z_dst[chain_id(t, s, 2 * p + 1)] = zz[SB_STRIP:]
            return emit
        return [one(t, s, p) for t in order for s in strips for p in pairs if not masked_out(t, s, diagonal)]

    def value_ops(kb_lo):
        def one(p):
            def emit():
                vblk = v_ref[block_rows(kb_lo), p * LANES:(p + 1) * LANES]
                both = jnp.dot(jnp.concatenate([wgt_scr[2 * p], wgt_scr[2 * p + 1]], axis=0), vblk,
                               preferred_element_type=F32)
                acc_scr[2 * p] += both[:SB_TQ]
                acc_scr[2 * p + 1] += both[SB_TQ:]
            return emit
        return [one(p) for p in pairs]

    def weight_stage(z_src, diagonal, extras):
        chains = [(t, s, h) for t in order for s in strips for h in heads]
        cs, tri = {}, {}

        def cumsum(c):
            t, s, _ = c
            z = z_src[chain_id(*c)]
            neg_abs = pltpu.bitcast(pltpu.bitcast(z, jnp.uint32) | jnp.uint32(0x80000000), F32)
            sp = jnp.maximum(z, 0.0) + jnp.log2(1.0 + jnp.exp2(neg_abs))
            if diagonal:
                tri[c] = col + t * SB_TILE < row + s * SB_STRIP
                sp = jnp.where(tri[c], sp, 0.0)
            cs[c] = jnp.dot(_bf(sp), wmat, preferred_element_type=F32)

        def weigh(c):
            t, s, h = c
            rows = slice(s * SB_STRIP, (s + 1) * SB_STRIP)
            lanes = slice(t * SB_TILE, (t + 1) * SB_TILE)
            wgt = jnp.exp2(z_src[chain_id(*c)] - cs[c][:, :SB_TILE] - r_scr[h, rows])
            wgt_scr[h, rows, lanes] = _bf(jnp.where(tri[c], wgt, 0.0) if diagonal else wgt)
            r_scr[h, rows] += cs[c][:, SB_TILE:]

        live = [c for c in chains if not masked_out(c[0], c[1], diagonal)]
        for t, s, h in chains:
            if masked_out(t, s, diagonal):
                wgt_scr[h, s * SB_STRIP:(s + 1) * SB_STRIP, t * SB_TILE:(t + 1) * SB_TILE] = (
                    jnp.zeros((SB_STRIP, SB_TILE), BF16))
        per_slot = -(-len(extras) // len(live))
        for i, c in enumerate(live):
            cumsum(c)
            for emit in extras[i * per_slot:(i + 1) * per_slot]:
                emit()
            if i >= SB_LAG:
                weigh(live[i - SB_LAG])
        for c in live[-SB_LAG:]:
            weigh(c)

    def step(cur, z_src, z_dst, issue_next=True):
        extras = value_ops((cur + 1) * SB_SUB)
        if issue_next:
            extras = extras + score_ops(jnp.maximum(cur - 1, 0) * SB_SUB, z_dst, False)
        weight_stage(z_src, False, extras)

    for emit in score_ops(qi * SB_SUB, za_scr, True):
        emit()
    weight_stage(za_scr, True, score_ops(jnp.maximum(qi - 1, 0) * SB_SUB, zb_scr, False))

    def body(j, carry):
        step(qi - 1 - 2 * j, zb_scr, za_scr)
        step(qi - 2 - 2 * j, za_scr, zb_scr)
        return carry

    lax.fori_loop(0, qi // 2, body, 0)

    @pl.when(qi % 2 == 1)
    def _():
        step(0, zb_scr, za_scr, issue_next=False)

    for emit in value_ops(0):
        emit()
    lane = lax.broadcasted_iota(jnp.int32, (SB_TQ, SB_TILE), 1)
    for p in range(SB_HEADS_PER_STEP // 2):
        o_ref[:, p * LANES:(p + 1) * LANES] = _bf(jnp.where(lane < SB_HEAD_DIM, acc_scr[2 * p], acc_scr[2 * p + 1]))


def _sb_mixer(big, wmat, batch, seq):
    nq = seq // SB_TQ
    width = SB_HEADS_PER_STEP * SB_HEAD_DIM
    groups = MIX_W // width
    qoff = COL_SB_QKV // width
    return pl.pallas_call(
        _sb_kernel,
        grid=(batch, groups, nq),
        in_specs=[
            pl.BlockSpec((SB_TQ, width), lambda b, p, i: (b * nq + i, qoff + p)),
            pl.BlockSpec((seq, width), lambda b, p, i: (b, qoff + groups + p)),
            pl.BlockSpec((seq, width), lambda b, p, i: (b, qoff + 2 * groups + p)),
            pl.BlockSpec((SB_TILE, 2 * SB_TILE), lambda b, p, i: (0, 0)),
        ],
        out_specs=pl.BlockSpec((SB_TQ, width), lambda b, p, i: (b * nq + i, p)),
        out_shape=jax.ShapeDtypeStruct((batch * seq, MIX_W), BF16),
        scratch_shapes=[pltpu.VMEM((SB_HEADS_PER_STEP, SB_TQ, SB_TILE), F32),
                        pltpu.VMEM((SB_HEADS_PER_STEP, SB_TQ, SB_TILE), F32),
                        pltpu.VMEM((SB_SUB * (SB_TQ // SB_STRIP) * SB_HEADS_PER_STEP, SB_STRIP, SB_TILE), F32),
                        pltpu.VMEM((SB_SUB * (SB_TQ // SB_STRIP) * SB_HEADS_PER_STEP, SB_STRIP, SB_TILE), F32),
                        pltpu.VMEM((SB_HEADS_PER_STEP, SB_TQ, SB_SUB * SB_TILE), BF16)],
        compiler_params=_cparams("parallel", "parallel", "arbitrary"),
        name="stickbreak",
    )(big, big, big, wmat)


SSD_TILES = 2
GROUP_W = MIX_W // SSM_GROUPS
HEADS_PER_GROUP = SSM_HEADS // SSM_GROUPS


def _ssd_kernel(z_ref, xbc_ref, sm_ref, cw_ref, cb_ref, prow_ref, drow_ref, nw_ref, e_ref, o_ref,
                halo_scr, state_scr):
    @pl.when(pl.program_id(1) == 0)
    def _():
        halo_scr[...] = jnp.zeros_like(halo_scr)
        state_scr[...] = jnp.zeros_like(state_scr)

    row, col, m_incl, _ = _chunk_masks()
    tril = jnp.where(m_incl, 1.0, 0.0)
    first = row[:, 0:1] < CHUNK
    gcol = lax.broadcasted_iota(jnp.int32, (TBLK, GROUP_W), 1)

    gn = SSM_GROUPS * SSM_STATE
    e = e_ref[...]
    for j in range(SSD_TILES):
        rows = slice(j * TBLK, (j + 1) * TBLK)
        cur = xbc_ref[rows, :].astype(F32)
        halo = halo_scr[...] if j == 0 else xbc_ref[j * TBLK - 16:j * TBLK, :].astype(F32)[16 - HALO:16]
        xbc = _silu(_causal_conv(halo, cur, cw_ref[...], cb_ref[...]))
        xs = xbc[:, :MIX_W]
        bm = xbc[:, MIX_W:MIX_W + gn]
        cm = xbc[:, MIX_W + gn:]

        dt_all = _softplus(sm_ref[rows, :] + prow_ref[1:2, :])
        a_all = -jnp.exp(prow_ref[0:1, :]) * dt_all
        acum_all = _mm_sel_lhs(tril, a_all)
        acum_t = acum_all.T
        dt_exp = _mm_sel_rhs(dt_all, e)
        acum_exp = _mm_sel_rhs(acum_all, e)
        xdt = xs * dt_exp

        for g in range(SSM_GROUPS):
            c_g = cm[:, g * SSM_STATE:(g + 1) * SSM_STATE]
            b_g = bm[:, g * SSM_STATE:(g + 1) * SSM_STATE]
            scores = _mm_nt(c_g, b_g)
            lo, hi = g * GROUP_W, (g + 1) * GROUP_W
            x_g = xdt[:, lo:hi]
            a_g = acum_exp[:, lo:hi]
            y_g = jnp.zeros((TBLK, GROUP_W), F32)
            for hh in range(HEADS_PER_GROUP):
                h = g * HEADS_PER_GROUP + hh
                a_col = acum_exp[:, h * SSM_HEAD_DIM:h * SSM_HEAD_DIM + 1]
                a_row = acum_t[LANE_SSM_DT + h:LANE_SSM_DT + h + 1, :]
                lmat = jnp.exp(jnp.where(m_incl, a_col - a_row, -jnp.inf))
                y_g = jnp.where(gcol // SSM_HEAD_DIM == hh, _mm(scores * lmat, x_g), y_g)

            b_t = b_g.T
            a_last0 = a_g[CHUNK - 1:CHUNK, :]
            a_last1 = a_g[TBLK - 1:TBLK, :]
            x_sc = x_g * jnp.exp(jnp.where(first, a_last0, a_last1) - a_g)
            st = state_scr[g]
            yoff = []
            for c, a_last in ((0, a_last0), (1, a_last1)):
                r0, r1 = c * CHUNK, (c + 1) * CHUNK
                yoff.append(_mm(c_g[r0:r1], st) * jnp.exp(a_g[r0:r1]))
                st = st * jnp.exp(a_last) + _mm(b_t, jnp.where(first == (c == 0), x_sc, 0.0))
            state_scr[g] = st
            y_g = y_g + jnp.concatenate(yoff, axis=0) + xs[:, lo:hi] * drow_ref[:, lo:hi]
            y_g = y_g * _silu(z_ref[rows, lo:hi].astype(F32))
            o_ref[rows, lo:hi] = _bf(_rms(y_g, nw_ref[:, lo:hi]))
    last = SSD_TILES * TBLK
    halo_scr[...] = xbc_ref[last - 16:last, :].astype(F32)[16 - HALO:16]


def _ssd_mixer(big, small, conv_w, conv_b, prow, drow, norm_w, emat, batch, seq):
    rows = SSD_TILES * TBLK
    nd = seq // rows
    conv_dim = MIX_W + 2 * SSM_GROUPS * SSM_STATE
    row_map = lambda b, d: (b * nd + d, 0)
    const = lambda b, d: (0, 0)
    return pl.pallas_call(
        _ssd_kernel,
        grid=(batch, nd),
        in_specs=[
            pl.BlockSpec((rows, MIX_W), lambda b, d: (b * nd + d, COL_SSM_Z // MIX_W)),
            pl.BlockSpec((rows, conv_dim), lambda b, d: (b * nd + d, COL_SSM_XBC // conv_dim)),
            pl.BlockSpec((rows, LANES), row_map),
            pl.BlockSpec((CONV_WIDTH, conv_dim), const),
            pl.BlockSpec((1, conv_dim), const),
            pl.BlockSpec((8, LANES), const),
            pl.BlockSpec((1, MIX_W), const),
            pl.BlockSpec((1, MIX_W), const),
            pl.BlockSpec((LANES, MIX_W), const),
        ],
        out_specs=pl.BlockSpec((rows, MIX_W), row_map),
        out_shape=jax.ShapeDtypeStruct((batch * seq, MIX_W), BF16),
        scratch_shapes=[pltpu.VMEM((HALO, conv_dim), F32), pltpu.VMEM((SSM_GROUPS, SSM_STATE, GROUP_W), F32)],
        compiler_params=_cparams("parallel", "arbitrary"),
        name="ssd",
    )(big, big, small, conv_w, conv_b, prow, drow, norm_w, emat)


def _merge_kernel(x_ref, odn_ref, osb_ref, ossm_ref, g0_ref, g1_ref, g2_ref, wb_ref, wo_ref, o_ref):
    m = _sigmoid(g0_ref[...].astype(F32)) * jnp.dot(odn_ref[...], wb_ref[0], preferred_element_type=F32)
    m = m + _sigmoid(g1_ref[...].astype(F32)) * jnp.dot(osb_ref[...], wb_ref[1], preferred_element_type=F32)
    m = m + _sigmoid(g2_ref[...].astype(F32)) * jnp.dot(ossm_ref[...], wb_ref[2], preferred_element_type=F32)
    o_ref[...] = x_ref[...] + jnp.dot(_bf(m), wo_ref[...], preferred_element_type=F32)


def _merge(x, o_dn, o_sb, o_ssm, big, w_branch, w_out, layer, tm=512):
    t = x.shape[0]
    tm = min(tm, t)
    rows = lambda i: (i, 0)
    gcol = COL_GATES // D_MODEL
    return pl.pallas_call(
        _merge_kernel,
        grid=(t // tm,),
        in_specs=[
            pl.BlockSpec((tm, D_MODEL), rows),
            pl.BlockSpec((tm, MIX_W), rows),
            pl.BlockSpec((tm, MIX_W), rows),
            pl.BlockSpec((tm, MIX_W), rows),
            pl.BlockSpec((tm, D_MODEL), lambda i: (i, gcol)),
            pl.BlockSpec((tm, D_MODEL), lambda i: (i, gcol + 1)),
            pl.BlockSpec((tm, D_MODEL), lambda i: (i, gcol + 2)),
            pl.BlockSpec((None, 3, MIX_W, D_MODEL), lambda i: (layer, 0, 0, 0)),
            pl.BlockSpec((None, D_MODEL, D_MODEL), lambda i: (layer, 0, 0)),
        ],
        out_specs=pl.BlockSpec((tm, D_MODEL), rows),
        out_shape=jax.ShapeDtypeStruct((t, D_MODEL), F32),
        compiler_params=_cparams("parallel"),
        name="merge",
    )(x, o_dn, o_sb, o_ssm, big, big, big, w_branch, w_out)


def _mlp_kernel(x_ref, g_ref, wu_ref, wd_ref, gf_ref, o_ref, xn_scr, acc_scr, *, final_norm):
    j = pl.program_id(1)

    @pl.when(j == 0)
    def _():
        xn_scr[...] = _bf(_rms(x_ref[...], g_ref[...]))
        acc_scr[...] = jnp.zeros_like(acc_scr)

    h = jnp.dot(xn_scr[...], wu_ref[...], preferred_element_type=F32)
    h = jnp.square(jnp.maximum(h, 0.0))
    acc_scr[...] += jnp.dot(_bf(h), wd_ref[...], preferred_element_type=F32)

    @pl.when(j == pl.num_programs(1) - 1)
    def _():
        y = x_ref[...] + acc_scr[...]
        o_ref[...] = _rms(y, gf_ref[...]) if final_norm else y


def _mlp(x, g, w_up, w_down, g_final, layer, final_norm, tm=1024, tf=2048):
    t = x.shape[0]
    tm = min(tm, t)
    return pl.pallas_call(
        functools.partial(_mlp_kernel, final_norm=final_norm),
        grid=(t // tm, D_FF // tf),
        in_specs=[
            pl.BlockSpec((tm, D_MODEL), lambda i, j: (i, 0)),
            pl.BlockSpec((1, D_MODEL), lambda i, j: (0, 0)),
            pl.BlockSpec((None, D_MODEL, tf), lambda i, j: (layer, 0, j)),
            pl.BlockSpec((None, tf, D_MODEL), lambda i, j: (layer, j, 0)),
            pl.BlockSpec((1, D_MODEL), lambda i, j: (0, 0)),
        ],
        out_specs=pl.BlockSpec((tm, D_MODEL), lambda i, j: (i, 0)),
        out_shape=jax.ShapeDtypeStruct((t, D_MODEL), F32),
        scratch_shapes=[pltpu.VMEM((tm, D_MODEL), BF16), pltpu.VMEM((tm, D_MODEL), F32)],
        compiler_params=_cparams("parallel", "arbitrary"),
        name="mlp",
    )(x, g, w_up, w_down, g_final)


def _lane_row(values, lane0):
    return jnp.zeros((LANES,), F32).at[lane0:lane0 + values.shape[0]].set(values.astype(F32))


def _param_rows(a_log, dt_bias, lane0):
    rows = jnp.zeros((8, LANES), F32)
    return rows.at[0].set(_lane_row(a_log, lane0)).at[1].set(_lane_row(dt_bias, lane0))


def _sb_cumsum_matrix():
    j = jnp.arange(SB_TILE)[:, None]
    s = jnp.arange(SB_TILE)[None, :]
    return jnp.concatenate([(j >= s).astype(BF16), jnp.ones((SB_TILE, SB_TILE), BF16)], axis=1)


def _ssm_expand_matrix():
    lane = jnp.arange(LANES)[:, None]
    ch = jnp.arange(MIX_W)[None, :]
    return (lane == LANE_SSM_DT + ch // SSM_HEAD_DIM).astype(BF16)


def _split_w_in(w):
    w = _bf(w)
    big = jnp.concatenate([w[..., 0:4096], w[..., 4112:10256], w[..., 10272:13344]], axis=-1)
    pad = jnp.zeros(w.shape[:-1] + (LANES - 32,), w.dtype)
    small = jnp.concatenate([w[..., 4096:4112], w[..., 10256:10272], pad], axis=-1)
    return big, small


def kernel(x, norm_mix, w_in, dn_conv_w, dn_a_log, dn_dt_bias, dn_norm_w, ssm_conv_w, ssm_conv_b, ssm_a_log,
           ssm_dt_bias, ssm_d, ssm_norm_w, w_branch, w_out, norm_mlp, w_up, w_down, norm_final):
    batch, seq, _ = x.shape
    depth = w_in.shape[0]
    h = x.reshape(batch * seq, D_MODEL)
    sb_w = _sb_cumsum_matrix()
    emat = _ssm_expand_matrix()
    g_final = norm_final.reshape(1, D_MODEL)
    w_big, w_small = _split_w_in(w_in)
    w_branch, w_out, w_up, w_down = _bf(w_branch), _bf(w_out), _bf(w_up), _bf(w_down)
    for l in range(depth):
        big, small = _inproj(h, norm_mix[l].reshape(1, D_MODEL), w_big, w_small, l)
        o_dn = _dn_mixer(big, small, dn_conv_w[l], _param_rows(dn_a_log[l], dn_dt_bias[l], LANE_DN_A),
                         dn_norm_w[l].reshape(1, DN_HEAD_DIM), batch, seq)
        o_sb = _sb_mixer(big, sb_w, batch, seq)
        o_ssm = _ssd_mixer(big, small, ssm_conv_w[l], ssm_conv_b[l].reshape(1, -1),
                           _param_rows(ssm_a_log[l], ssm_dt_bias[l], LANE_SSM_DT),
                           jnp.repeat(ssm_d[l], SSM_HEAD_DIM).reshape(1, MIX_W),
                           ssm_norm_w[l].reshape(1, MIX_W), emat, batch, seq)
        h = _merge(h, o_dn, o_sb, o_ssm, big, w_branch, w_out, l)
        h = _mlp(h, norm_mlp[l].reshape(1, D_MODEL), w_up, w_down, g_final, l, final_norm=(l == depth - 1))
    return h.reshape(batch, seq, D_MODEL)
```

```python
import functools

import jax
import jax.numpy as jnp
from jax import lax
from jax.experimental import pallas as pl
from jax.experimental.pallas import tpu as pltpu

F32 = jnp.float32
BF16 = jnp.bfloat16

D_MODEL = 1024
MIX_W = D_MODEL
DN_HEADS = 8
DN_HEAD_DIM = 128
SB_HEADS = 16
SB_HEAD_DIM = 64
SSM_HEADS = 16
SSM_HEAD_DIM = 64
SSM_STATE = 128
SSM_GROUPS = 4
CHUNK = 64
D_FF = 4 * D_MODEL
EPS = 1e-6
CONV_WIDTH = 4

LANES = 128
TBLK = 2 * CHUNK
HALO = 8
DN_BASE_BLOCK = 8
DN_TILES = 2

COL_DN_QKV = 0
COL_DN_GATE = 3072
COL_SB_QKV = 4096
COL_SSM_Z = 7168
COL_SSM_XBC = 8192
COL_GATES = 10240
N_BIG = 13312
LANE_DN_A = 0
LANE_DN_B = 8
LANE_SSM_DT = 16

VMEM_LIMIT = 56 * 1024 * 1024


def _cparams(*sem):
    return pltpu.CompilerParams(dimension_semantics=sem, vmem_limit_bytes=VMEM_LIMIT)


def _bf(x):
    return x.astype(BF16)


def _mm(a, b):
    return jnp.dot(_bf(a), _bf(b), preferred_element_type=F32)


def _mm_nt(a, b):
    return lax.dot_general(_bf(a), _bf(b), (((1,), (1,)), ((), ())), preferred_element_type=F32)


def _split2(a):
    hi = _bf(a)
    lo = _bf(a - hi.astype(F32))
    return hi, lo


def _split3(a):
    hi = _bf(a)
    r = a - hi.astype(F32)
    mid = _bf(r)
    lo = _bf(r - mid.astype(F32))
    return hi, mid, lo


def _mm_x3s(a_split, b_split):
    ah, al = a_split
    bh, bl = b_split
    lhs = jnp.concatenate([ah, al, ah], axis=1)
    rhs = jnp.concatenate([bh, bh, bl], axis=0)
    return jnp.dot(lhs, rhs, preferred_element_type=F32)


def _mm_sel_lhs(sel, b):
    s = _bf(sel)
    bh, bm, bl = _split3(b)
    return jnp.dot(jnp.concatenate([s, s, s], axis=1), jnp.concatenate([bh, bm, bl], axis=0),
                   preferred_element_type=F32)


def _mm_sel_rhs(a, sel_bf):
    ah, am, al = _split3(a)
    return jnp.dot(jnp.concatenate([ah, am, al], axis=1), jnp.concatenate([sel_bf, sel_bf, sel_bf], axis=0),
                   preferred_element_type=F32)


def _softplus(x):
    return jnp.maximum(x, 0.0) + jnp.log(1.0 + jnp.exp(-jnp.abs(x)))


def _sigmoid(x):
    return 1.0 / (1.0 + jnp.exp(-x))


def _silu(x):
    return x * _sigmoid(x)


def _rms(x, w):
    return x * lax.rsqrt(jnp.mean(x * x, axis=-1, keepdims=True) + EPS) * w


def _causal_conv(halo, cur, w, bias):
    rows = cur.shape[0]
    xx = jnp.concatenate([halo, cur], axis=0)
    y = w[CONV_WIDTH - 1:CONV_WIDTH, :] * cur
    for k in range(CONV_WIDTH - 1):
        shifted = pltpu.roll(xx, CONV_WIDTH - 1 - k, 0)[HALO:HALO + rows]
        y = y + w[k:k + 1, :] * shifted
    if bias is not None:
        y = y + bias
    return y


def _chunk_masks():
    row = lax.broadcasted_iota(jnp.int32, (TBLK, TBLK), 0)
    col = lax.broadcasted_iota(jnp.int32, (TBLK, TBLK), 1)
    same = (row >= CHUNK) == (col >= CHUNK)
    return row, col, same & (col <= row), same & (col < row)


def _inproj_kernel(x_ref, g_ref, wb_ref, ws_ref, big_ref, small_ref, xn_scr):
    @pl.when(pl.program_id(1) == 0)
    def _():
        xn_scr[...] = _bf(_rms(x_ref[...], g_ref[...]))
        small_ref[...] = jnp.dot(xn_scr[...], ws_ref[...], preferred_element_type=F32)

    big_ref[...] = _bf(jnp.dot(xn_scr[...], wb_ref[...], preferred_element_type=F32))


def _inproj(x, g, w_big, w_small, layer, tm=2048, tn=1024):
    t = x.shape[0]
    tm = min(tm, t)
    return pl.pallas_call(
        _inproj_kernel,
        grid=(t // tm, N_BIG // tn),
        in_specs=[
            pl.BlockSpec((tm, D_MODEL), lambda i, j: (i, 0)),
            pl.BlockSpec((1, D_MODEL), lambda i, j: (0, 0)),
            pl.BlockSpec((None, D_MODEL, tn), lambda i, j: (layer, 0, j)),
            pl.BlockSpec((None, D_MODEL, LANES), lambda i, j: (layer, 0, 0)),
        ],
        out_specs=[
            pl.BlockSpec((tm, tn), lambda i, j: (i, j)),
            pl.BlockSpec((tm, LANES), lambda i, j: (i, 0)),
        ],
        out_shape=[jax.ShapeDtypeStruct((t, N_BIG), BF16), jax.ShapeDtypeStruct((t, LANES), F32)],
        scratch_shapes=[pltpu.VMEM((tm, D_MODEL), BF16)],
        compiler_params=_cparams("parallel", "arbitrary"),
        name="inproj",
    )(x, g, w_big, w_small)


def _dn_kernel(qkv_ref, gate_ref, sm_ref, cw_ref, prow_ref, nw_ref, o_ref, halo_scr, state_scr):
    @pl.when(pl.program_id(1) == 0)
    def _():
        halo_scr[...] = jnp.zeros_like(halo_scr)
        state_scr[...] = jnp.zeros_like(state_scr)

    row, col, m_incl, m_strict = _chunk_masks()
    tril = jnp.where(m_incl, 1.0, 0.0)
    eye = jnp.where(row == col, 1.0, 0.0)
    first = row[:, 0:1] < CHUNK

    units = [(j, h) for j in range(DN_TILES) for h in range(DN_HEADS)]
    heads = range(DN_HEADS)
    q, k, v, kb, vb, decay, gc_col, kk, qk = {}, {}, {}, {}, {}, {}, {}, {}, {}
    low, xs, tinv, cd = {}, {}, {}, {}
    u, w, attn, qg, kdt, gl = {}, {}, {}, {}, {}, {}
    pad = jnp.zeros((CHUNK, DN_HEAD_DIM), F32)

    def same_block(size):
        shift = size.bit_length() - 1
        return (row >> shift) == (col >> shift)

    def prepare(j):
        rows = slice(j * TBLK, (j + 1) * TBLK)
        sm = sm_ref[rows, :]
        g_all = -jnp.exp(prow_ref[0:1, :]) * _softplus(sm + prow_ref[1:2, :])
        beta_all = _sigmoid(sm)
        gc_all = _mm_sel_lhs(tril, g_all)
        gct_all = gc_all.T
        for h in heads:
            def conv_silu(seg):
                lanes = slice(seg * MIX_W + h * DN_HEAD_DIM, seg * MIX_W + (h + 1) * DN_HEAD_DIM)
                if j == 0:
                    halo = halo_scr[:, lanes]
                else:
                    halo = qkv_ref[j * TBLK - 16:j * TBLK, lanes].astype(F32)[16 - HALO:16]
                return _silu(_causal_conv(halo, qkv_ref[rows, lanes].astype(F32), cw_ref[:, lanes], None))

            un = (j, h)
            qh = conv_silu(0)
            kh = conv_silu(1)
            v[un] = conv_silu(2)
            q[un] = qh * lax.rsqrt(jnp.sum(qh * qh, axis=-1, keepdims=True) + EPS) * (DN_HEAD_DIM ** -0.5)
            k[un] = kh * lax.rsqrt(jnp.sum(kh * kh, axis=-1, keepdims=True) + EPS)
            gc_col[un] = gc_all[:, LANE_DN_A + h:LANE_DN_A + h + 1]
            beta = beta_all[:, LANE_DN_B + h:LANE_DN_B + h + 1]
            gc_row = gct_all[LANE_DN_A + h:LANE_DN_A + h + 1, :]
            decay[un] = jnp.exp(jnp.where(m_incl, gc_col[un] - gc_row, -jnp.inf))
            kb[un] = k[un] * beta
            vb[un] = v[un] * beta
    def gram(units):
        for un in units:
            kk[un] = _mm_nt(kb[un], k[un])
            qk[un] = _mm_nt(q[un], k[un])

    def solve(units):
        for un in units:
            low[un] = jnp.where(m_strict, kk[un] * decay[un], 0.0)
            x = -jnp.where(same_block(DN_BASE_BLOCK), low[un], 0.0)
            xs[un] = _split2(x)
            tinv[un] = eye + x
        for _ in range(2):
            for un in units:
                xs[un] = _split2(_mm_x3s(xs[un], xs[un]))
            for un in units:
                tinv[un] = tinv[un] + _mm_x3s(_split2(tinv[un]), xs[un])
        size = DN_BASE_BLOCK
        while size < CHUNK:
            off_diag = same_block(2 * size) & jnp.logical_not(same_block(size))
            for un in units:
                cd[un] = _mm(jnp.where(off_diag, low[un], 0.0), tinv[un])
            for un in units:
                tinv[un] = tinv[un] - _mm(tinv[un], cd[un])
            size *= 2
        for un in units:
            egc = jnp.exp(gc_col[un])
            u[un] = _mm(tinv[un], vb[un])
            w[un] = _mm(tinv[un], kb[un] * egc)
            attn[un] = _bf(jnp.where(m_incl, qk[un] * decay[un], 0.0))
            qg[un] = _bf(q[un] * egc)
            gl[un] = (gc_col[un][CHUNK - 1:CHUNK, :], gc_col[un][TBLK - 1:TBLK, :])
            kdt[un] = _bf((k[un] * jnp.exp(jnp.where(first, gl[un][0], gl[un][1]) - gc_col[un])).T)

    s = {h: state_scr[h] for h in heads}

    def scan(j):
        outs = {h: [] for h in heads}
        for c in range(2):
            r0, r1 = c * CHUNK, (c + 1) * CHUNK
            v_blk = {}
            for h in heads:
                un = (j, h)
                v_new = u[un][r0:r1] - _mm(w[un][r0:r1], s[h])
                v_blk[h] = _bf(jnp.concatenate([v_new, pad] if c == 0 else [pad, v_new], axis=0))
            for h in heads:
                un = (j, h)
                outs[h].append(_mm(qg[un][r0:r1], s[h]) + _mm(attn[un][r0:r1, :], v_blk[h]))
                s[h] = s[h] * jnp.exp(gl[un][c]) + _mm(kdt[un], v_blk[h])
        for h in heads:
            lanes = slice(h * DN_HEAD_DIM, (h + 1) * DN_HEAD_DIM)
            rows = slice(j * TBLK, (j + 1) * TBLK)
            o = jnp.concatenate(outs[h], axis=0)
            o_ref[rows, lanes] = _bf(_rms(o, nw_ref[...]) * _silu(gate_ref[rows, lanes].astype(F32)))

    for j in range(DN_TILES):
        prepare(j)
    gram(units)
    solve(units)
    for j in range(DN_TILES):
        scan(j)
    for h in heads:
        state_scr[h] = s[h]
    last = DN_TILES * TBLK
    halo_scr[...] = qkv_ref[last - 16:last, :].astype(F32)[16 - HALO:16]


def _dn_mixer(big, small, conv_w, prow, norm_w, batch, seq):
    rows = DN_TILES * TBLK
    nd = seq // rows
    row_map = lambda b, d: (b * nd + d, 0)
    return pl.pallas_call(
        _dn_kernel,
        grid=(batch, nd),
        in_specs=[
            pl.BlockSpec((rows, 3 * MIX_W), lambda b, d: (b * nd + d, COL_DN_QKV // (3 * MIX_W))),
            pl.BlockSpec((rows, MIX_W), lambda b, d: (b * nd + d, COL_DN_GATE // MIX_W)),
            pl.BlockSpec((rows, LANES), row_map),
            pl.BlockSpec((CONV_WIDTH, 3 * MIX_W), lambda b, d: (0, 0)),
            pl.BlockSpec((8, LANES), lambda b, d: (0, 0)),
            pl.BlockSpec((1, DN_HEAD_DIM), lambda b, d: (0, 0)),
        ],
        out_specs=pl.BlockSpec((rows, MIX_W), row_map),
        out_shape=jax.ShapeDtypeStruct((batch * seq, MIX_W), BF16),
        scratch_shapes=[pltpu.VMEM((HALO, 3 * MIX_W), F32), pltpu.VMEM((DN_HEADS, DN_HEAD_DIM, DN_HEAD_DIM), F32)],
        compiler_params=_cparams("parallel", "arbitrary"),
        name="deltanet",
    )(big, big, small, conv_w, prow, norm_w)


SB_TILE = 128
SB_TQ = 256
SB_SUB = SB_TQ // SB_TILE
SB_STRIP = 256
SB_HEADS_PER_STEP = 4
LOG2E = 1.4426950408889634
SB_LAG = 4


def _sb_kernel(q_ref, k_ref, v_ref, w_ref, o_ref, r_scr, acc_scr, za_scr, zb_scr, wgt_scr):
    qi = pl.program_id(2)
    row = lax.broadcasted_iota(jnp.int32, (SB_STRIP, SB_TILE), 0)
    col = lax.broadcasted_iota(jnp.int32, (SB_STRIP, SB_TILE), 1)
    wmat = w_ref[...]
    r_scr[...] = jnp.zeros_like(r_scr)
    acc_scr[...] = jnp.zeros_like(acc_scr)
    strips = range(SB_TQ // SB_STRIP)
    heads = range(SB_HEADS_PER_STEP)
    pairs = range(SB_HEADS_PER_STEP // 2)
    order = list(reversed(range(SB_SUB)))
    qh = {}
    for s in strips:
        for h in heads:
            lanes = slice((h // 2) * LANES, (h // 2 + 1) * LANES)
            q = q_ref[s * SB_STRIP:(s + 1) * SB_STRIP, lanes].astype(F32) * (SB_HEAD_DIM ** -0.5)
            qh[s, h] = _bf(jnp.where((col >= SB_HEAD_DIM) == (h % 2 == 1), q, 0.0))

    def chain_id(t, s, h):
        return (t * len(strips) + s) * SB_HEADS_PER_STEP + h

    def masked_out(t, s, diagonal):
        return diagonal and t * SB_TILE >= (s + 1) * SB_STRIP

    def block_rows(kb_lo):
        return pl.ds(pl.multiple_of(kb_lo * SB_TILE, SB_TILE), SB_SUB * SB_TILE)

    def score_ops(kb_lo, z_dst, diagonal):
        def one(t, s, p):
            def emit():
                rows = pl.ds(pl.multiple_of((kb_lo + t) * SB_TILE, SB_TILE), SB_TILE)
                kblk = k_ref[rows, p * LANES:(p + 1) * LANES]
                zz = _mm_nt(jnp.concatenate([qh[s, 2 * p], qh[s, 2 * p + 1]], axis=0), kblk) * LOG2E
                z_dst[chain_id(t, s, 2 * p)] = zz[:SB_STRIP]
                z_dst[chain_id(t, s, 2 * p + 1)] = zz[SB_STRIP:]
            return emit
        return [one(t, s, p) for t in order for s in strips for p in pairs if not masked_out(t, s, diagonal)]

    def value_ops(kb_lo):
        def one(p):
            def emit():
                vblk = v_ref[block_rows(kb_lo), p * LANES:(p + 1) * LANES]
                both = jnp.dot(jnp.concatenate([wgt_scr[2 * p], wgt_scr[2 * p + 1]], axis=0), vblk,
                               preferred_element_type=F32)
                acc_scr[2 * p] += both[:SB_TQ]
                acc_scr[2 * p + 1] += both[SB_TQ:]
            return emit
        return [one(p) for p in pairs]

    def weight_stage(z_src, diagonal, extras):
        chains = [(t, s, h) for t in order for s in strips for h in heads]
        cs, tri = {}, {}

        def cumsum(group):
            sps = []
            for c in group:
                t, s, _ = c
                z = z_src[chain_id(*c)]
                neg_abs = pltpu.bitcast(pltpu.bitcast(z, jnp.uint32) | jnp.uint32(0x80000000), F32)
                sp = jnp.maximum(z, 0.0) + jnp.log2(1.0 + jnp.exp2(neg_abs))
                if diagonal:
                    tri[c] = col + t * SB_TILE < row + s * SB_STRIP
                    sp = jnp.where(tri[c], sp, 0.0)
                sps.append(_bf(sp))
            both = jnp.dot(jnp.concatenate(sps, axis=0), wmat, preferred_element_type=F32)
            for i, c in enumerate(group):
                cs[c] = both[i * SB_STRIP:(i + 1) * SB_STRIP]

        def weigh(c):
            t, s, h = c
            rows = slice(s * SB_STRIP, (s + 1) * SB_STRIP)
            lanes = slice(t * SB_TILE, (t + 1) * SB_TILE)
            wgt = jnp.exp2(z_src[chain_id(*c)] - cs[c][:, :SB_TILE] - r_scr[h, rows])
            wgt_scr[h, rows, lanes] = _bf(jnp.where(tri[c], wgt, 0.0) if diagonal else wgt)
            r_scr[h, rows] += cs[c][:, SB_TILE:]

        live = [c for c in chains if not masked_out(c[0], c[1], diagonal)]
        for t, s, h in chains:
            if masked_out(t, s, diagonal):
                wgt_scr[h, s * SB_STRIP:(s + 1) * SB_STRIP, t * SB_TILE:(t + 1) * SB_TILE] = (
                    jnp.zeros((SB_STRIP, SB_TILE), BF16))
        groups = [live[i:i + 2] for i in range(0, len(live), 2)]
        lag = SB_LAG // 2
        per_slot = -(-len(extras) // len(groups))
        for i, group in enumerate(groups):
            cumsum(group)
            for emit in extras[i * per_slot:(i + 1) * per_slot]:
                emit()
            if i >= lag:
                for c in groups[i - lag]:
                    weigh(c)
        for group in groups[-lag:]:
            for c in group:
                weigh(c)

    def step(cur, z_src, z_dst, issue_next=True):
        extras = value_ops((cur + 1) * SB_SUB)
        if issue_next:
            extras = extras + score_ops(jnp.maximum(cur - 1, 0) * SB_SUB, z_dst, False)
        weight_stage(z_src, False, extras)

    for emit in score_ops(qi * SB_SUB, za_scr, True):
        emit()
    weight_stage(za_scr, True, score_ops(jnp.maximum(qi - 1, 0) * SB_SUB, zb_scr, False))

    def body(j, carry):
        step(qi - 1 - 2 * j, zb_scr, za_scr)
        step(qi - 2 - 2 * j, za_scr, zb_scr)
        return carry

    lax.fori_loop(0, qi // 2, body, 0)

    @pl.when(qi % 2 == 1)
    def _():
        step(0, zb_scr, za_scr, issue_next=False)

    for emit in value_ops(0):
        emit()
    lane = lax.broadcasted_iota(jnp.int32, (SB_TQ, SB_TILE), 1)
    for p in range(SB_HEADS_PER_STEP // 2):
        o_ref[:, p * LANES:(p + 1) * LANES] = _bf(jnp.where(lane < SB_HEAD_DIM, acc_scr[2 * p], acc_scr[2 * p + 1]))


def _sb_mixer(big, wmat, batch, seq):
    nq = seq // SB_TQ
    width = SB_HEADS_PER_STEP * SB_HEAD_DIM
    groups = MIX_W // width
    qoff = COL_SB_QKV // width
    return pl.pallas_call(
        _sb_kernel,
        grid=(batch, groups, nq),
        in_specs=[
            pl.BlockSpec((SB_TQ, width), lambda b, p, i: (b * nq + i, qoff + p)),
            pl.BlockSpec((seq, width), lambda b, p, i: (b, qoff + groups + p)),
            pl.BlockSpec((seq, width), lambda b, p, i: (b, qoff + 2 * groups + p)),
            pl.BlockSpec((SB_TILE, 2 * SB_TILE), lambda b, p, i: (0, 0)),
        ],
        out_specs=pl.BlockSpec((SB_TQ, width), lambda b, p, i: (b * nq + i, p)),
        out_shape=jax.ShapeDtypeStruct((batch * seq, MIX_W), BF16),
        scratch_shapes=[pltpu.VMEM((SB_HEADS_PER_STEP, SB_TQ, SB_TILE), F32),
                        pltpu.VMEM((SB_HEADS_PER_STEP, SB_TQ, SB_TILE), F32),
                        pltpu.VMEM((SB_SUB * (SB_TQ // SB_STRIP) * SB_HEADS_PER_STEP, SB_STRIP, SB_TILE), F32),
                        pltpu.VMEM((SB_SUB * (SB_TQ // SB_STRIP) * SB_HEADS_PER_STEP, SB_STRIP, SB_TILE), F32),
                        pltpu.VMEM((SB_HEADS_PER_STEP, SB_TQ, SB_SUB * SB_TILE), BF16)],
        compiler_params=_cparams("parallel", "parallel", "arbitrary"),
        name="stickbreak",
    )(big, big, big, wmat)


SSD_TILES = 2
GROUP_W = MIX_W // SSM_GROUPS
HEADS_PER_GROUP = SSM_HEADS // SSM_GROUPS


def _ssd_kernel(z_ref, xbc_ref, sm_ref, cw_ref, cb_ref, prow_ref, drow_ref, nw_ref, e_ref, o_ref,
                halo_scr, state_scr):
    @pl.when(pl.program_id(1) == 0)
    def _():
        halo_scr[...] = jnp.zeros_like(halo_scr)
        state_scr[...] = jnp.zeros_like(state_scr)

    row, col, m_incl, _ = _chunk_masks()
    tril = jnp.where(m_incl, 1.0, 0.0)
    first = row[:, 0:1] < CHUNK
    gcol = lax.broadcasted_iota(jnp.int32, (TBLK, GROUP_W), 1)

    gn = SSM_GROUPS * SSM_STATE
    e = e_ref[...]
    for j in range(SSD_TILES):
        rows = slice(j * TBLK, (j + 1) * TBLK)
        cur = xbc_ref[rows, :].astype(F32)
        halo = halo_scr[...] if j == 0 else xbc_ref[j * TBLK - 16:j * TBLK, :].astype(F32)[16 - HALO:16]
        xbc = _silu(_causal_conv(halo, cur, cw_ref[...], cb_ref[...]))
        xs = xbc[:, :MIX_W]
        bm = xbc[:, MIX_W:MIX_W + gn]
        cm = xbc[:, MIX_W + gn:]

        dt_all = _softplus(sm_ref[rows, :] + prow_ref[1:2, :])
        a_all = -jnp.exp(prow_ref[0:1, :]) * dt_all
        acum_all = _mm_sel_lhs(tril, a_all)
        acum_t = acum_all.T
        dt_exp = _mm_sel_rhs(dt_all, e)
        acum_exp = _mm_sel_rhs(acum_all, e)
        xdt = xs * dt_exp

        for g in range(SSM_GROUPS):
            c_g = cm[:, g * SSM_STATE:(g + 1) * SSM_STATE]
            b_g = bm[:, g * SSM_STATE:(g + 1) * SSM_STATE]
            scores = _mm_nt(c_g, b_g)
            lo, hi = g * GROUP_W, (g + 1) * GROUP_W
            x_g = xdt[:, lo:hi]
            a_g = acum_exp[:, lo:hi]
            y_g = jnp.zeros((TBLK, GROUP_W), F32)
            for hh in range(HEADS_PER_GROUP):
                h = g * HEADS_PER_GROUP + hh
                a_col = acum_exp[:, h * SSM_HEAD_DIM:h * SSM_HEAD_DIM + 1]
                a_row = acum_t[LANE_SSM_DT + h:LANE_SSM_DT + h + 1, :]
                lmat = jnp.exp(jnp.where(m_incl, a_col - a_row, -jnp.inf))
                y_g = jnp.where(gcol // SSM_HEAD_DIM == hh, _mm(scores * lmat, x_g), y_g)

            b_t = b_g.T
            a_last0 = a_g[CHUNK - 1:CHUNK, :]
            a_last1 = a_g[TBLK - 1:TBLK, :]
            x_sc = x_g * jnp.exp(jnp.where(first, a_last0, a_last1) - a_g)
            st = state_scr[g]
            yoff = []
            for c, a_last in ((0, a_last0), (1, a_last1)):
                r0, r1 = c * CHUNK, (c + 1) * CHUNK
                yoff.append(_mm(c_g[r0:r1], st) * jnp.exp(a_g[r0:r1]))
                st = st * jnp.exp(a_last) + _mm(b_t, jnp.where(first == (c == 0), x_sc, 0.0))
            state_scr[g] = st
            y_g = y_g + jnp.concatenate(yoff, axis=0) + xs[:, lo:hi] * drow_ref[:, lo:hi]
            y_g = y_g * _silu(z_ref[rows, lo:hi].astype(F32))
            o_ref[rows, lo:hi] = _bf(_rms(y_g, nw_ref[:, lo:hi]))
    last = SSD_TILES * TBLK
    halo_scr[...] = xbc_ref[last - 16:last, :].astype(F32)[16 - HALO:16]


def _ssd_mixer(big, small, conv_w, conv_b, prow, drow, norm_w, emat, batch, seq):
    rows = SSD_TILES * TBLK
    nd = seq // rows
    conv_dim = MIX_W + 2 * SSM_GROUPS * SSM_STATE
    row_map = lambda b, d: (b * nd + d, 0)
    const = lambda b, d: (0, 0)
    return pl.pallas_call(
        _ssd_kernel,
        grid=(batch, nd),
        in_specs=[
            pl.BlockSpec((rows, MIX_W), lambda b, d: (b * nd + d, COL_SSM_Z // MIX_W)),
            pl.BlockSpec((rows, conv_dim), lambda b, d: (b * nd + d, COL_SSM_XBC // conv_dim)),
            pl.BlockSpec((rows, LANES), row_map),
            pl.BlockSpec((CONV_WIDTH, conv_dim), const),
            pl.BlockSpec((1, conv_dim), const),
            pl.BlockSpec((8, LANES), const),
            pl.BlockSpec((1, MIX_W), const),
            pl.BlockSpec((1, MIX_W), const),
            pl.BlockSpec((LANES, MIX_W), const),
        ],
        out_specs=pl.BlockSpec((rows, MIX_W), row_map),
        out_shape=jax.ShapeDtypeStruct((batch * seq, MIX_W), BF16),
        scratch_shapes=[pltpu.VMEM((HALO, conv_dim), F32), pltpu.VMEM((SSM_GROUPS, SSM_STATE, GROUP_W), F32)],
        compiler_params=_cparams("parallel", "arbitrary"),
        name="ssd",
    )(big, big, small, conv_w, conv_b, prow, drow, norm_w, emat)


def _merge_kernel(x_ref, odn_ref, osb_ref, ossm_ref, g0_ref, g1_ref, g2_ref, wb_ref, wo_ref, o_ref):
    m = _sigmoid(g0_ref[...].astype(F32)) * jnp.dot(odn_ref[...], wb_ref[0], preferred_element_type=F32)
    m = m + _sigmoid(g1_ref[...].astype(F32)) * jnp.dot(osb_ref[...], wb_ref[1], preferred_element_type=F32)
    m = m + _sigmoid(g2_ref[...].astype(F32)) * jnp.dot(ossm_ref[...], wb_ref[2], preferred_element_type=F32)
    o_ref[...] = x_ref[...] + jnp.dot(_bf(m), wo_ref[...], preferred_element_type=F32)


def _merge(x, o_dn, o_sb, o_ssm, big, w_branch, w_out, layer, tm=512):
    t = x.shape[0]
    tm = min(tm, t)
    rows = lambda i: (i, 0)
    gcol = COL_GATES // D_MODEL
    return pl.pallas_call(
        _merge_kernel,
        grid=(t // tm,),
        in_specs=[
            pl.BlockSpec((tm, D_MODEL), rows),
            pl.BlockSpec((tm, MIX_W), rows),
            pl.BlockSpec((tm, MIX_W), rows),
            pl.BlockSpec((tm, MIX_W), rows),
            pl.BlockSpec((tm, D_MODEL), lambda i: (i, gcol)),
            pl.BlockSpec((tm, D_MODEL), lambda i: (i, gcol + 1)),
            pl.BlockSpec((tm, D_MODEL), lambda i: (i, gcol + 2)),
            pl.BlockSpec((None, 3, MIX_W, D_MODEL), lambda i: (layer, 0, 0, 0)),
            pl.BlockSpec((None, D_MODEL, D_MODEL), lambda i: (layer, 0, 0)),
        ],
        out_specs=pl.BlockSpec((tm, D_MODEL), rows),
        out_shape=jax.ShapeDtypeStruct((t, D_MODEL), F32),
        compiler_params=_cparams("parallel"),
        name="merge",
    )(x, o_dn, o_sb, o_ssm, big, big, big, w_branch, w_out)


def _mlp_kernel(x_ref, g_ref, wu_ref, wd_ref, gf_ref, o_ref, xn_scr, acc_scr, *, final_norm):
    j = pl.program_id(1)

    @pl.when(j == 0)
    def _():
        xn_scr[...] = _bf(_rms(x_ref[...], g_ref[...]))
        acc_scr[...] = jnp.zeros_like(acc_scr)

    h = jnp.dot(xn_scr[...], wu_ref[...], preferred_element_type=F32)
    h = jnp.square(jnp.maximum(h, 0.0))
    acc_scr[...] += jnp.dot(_bf(h), wd_ref[...], preferred_element_type=F32)

    @pl.when(j == pl.num_programs(1) - 1)
    def _():
        y = x_ref[...] + acc_scr[...]
        o_ref[...] = _rms(y, gf_ref[...]) if final_norm else y


def _mlp(x, g, w_up, w_down, g_final, layer, final_norm, tm=1024, tf=2048):
    t = x.shape[0]
    tm = min(tm, t)
    return pl.pallas_call(
        functools.partial(_mlp_kernel, final_norm=final_norm),
        grid=(t // tm, D_FF // tf),
        in_specs=[
            pl.BlockSpec((tm, D_MODEL), lambda i, j: (i, 0)),
            pl.BlockSpec((1, D_MODEL), lambda i, j: (0, 0)),
            pl.BlockSpec((None, D_MODEL, tf), lambda i, j: (layer, 0, j)),
            pl.BlockSpec((None, tf, D_MODEL), lambda i, j: (layer, j, 0)),
            pl.BlockSpec((1, D_MODEL), lambda i, j: (0, 0)),
        ],
        out_specs=pl.BlockSpec((tm, D_MODEL), lambda i, j: (i, 0)),
        out_shape=jax.ShapeDtypeStruct((t, D_MODEL), F32),
        scratch_shapes=[pltpu.VMEM((tm, D_MODEL), BF16), pltpu.VMEM((tm, D_MODEL), F32)],
        compiler_params=_cparams("parallel", "arbitrary"),
        name="mlp",
    )(x, g, w_up, w_down, g_final)


def _lane_row(values, lane0):
    return jnp.zeros((LANES,), F32).at[lane0:lane0 + values.shape[0]].set(values.astype(F32))


def _param_rows(a_log, dt_bias, lane0):
    rows = jnp.zeros((8, LANES), F32)
    return rows.at[0].set(_lane_row(a_log, lane0)).at[1].set(_lane_row(dt_bias, lane0))


def _sb_cumsum_matrix():
    j = jnp.arange(SB_TILE)[:, None]
    s = jnp.arange(SB_TILE)[None, :]
    return jnp.concatenate([(j >= s).astype(BF16), jnp.ones((SB_TILE, SB_TILE), BF16)], axis=1)


def _ssm_expand_matrix():
    lane = jnp.arange(LANES)[:, None]
    ch = jnp.arange(MIX_W)[None, :]
    return (lane == LANE_SSM_DT + ch // SSM_HEAD_DIM).astype(BF16)


def _split_w_in(w):
    w = _bf(w)
    big = jnp.concatenate([w[..., 0:4096], w[..., 4112:10256], w[..., 10272:13344]], axis=-1)
    pad = jnp.zeros(w.shape[:-1] + (LANES - 32,), w.dtype)
    small = jnp.concatenate([w[..., 4096:4112], w[..., 10256:10272], pad], axis=-1)
    return big, small


def kernel(x, norm_mix, w_in, dn_conv_w, dn_a_log, dn_dt_bias, dn_norm_w, ssm_conv_w, ssm_conv_b, ssm_a_log,
           ssm_dt_bias, ssm_d, ssm_norm_w, w_branch, w_out, norm_mlp, w_up, w_down, norm_final):
    batch, seq, _ = x.shape
    depth = w_in.shape[0]
    h = x.reshape(batch * seq, D_MODEL)
    sb_w = _sb_cumsum_matrix()
    emat = _ssm_expand_matrix()
    g_final = norm_final.reshape(1, D_MODEL)
    w_big, w_small = _split_w_in(w_in)
    w_branch, w_out, w_up, w_down = _bf(w_branch), _bf(w_out), _bf(w_up), _bf(w_down)
    for l in range(depth):
        big, small = _inproj(h, norm_mix[l].reshape(1, D_MODEL), w_big, w_small, l)
        o_dn = _dn_mixer(big, small, dn_conv_w[l], _param_rows(dn_a_log[l], dn_dt_bias[l], LANE_DN_A),
                         dn_norm_w[l].reshape(1, DN_HEAD_DIM), batch, seq)
        o_sb = _sb_mixer(big, sb_w, batch, seq)
        o_ssm = _ssd_mixer(big, small, ssm_conv_w[l], ssm_conv_b[l].reshape(1, -1),
                           _param_rows(ssm_a_log[l], ssm_dt_bias[l], LANE_SSM_DT),
                           jnp.repeat(ssm_d[l], SSM_HEAD_DIM).reshape(1, MIX_W),
                           ssm_norm_w[l].reshape(1, MIX_W), emat, batch, seq)
        h = _merge(h, o_dn, o_sb, o_ssm, big, w_branch, w_out, l)
        h = _mlp(h, norm_mlp[l].reshape(1, D_MODEL), w_up, w_down, g_final, l, final_norm=(l == depth - 1))
    return h.reshape(batch, seq, D_MODEL)
```

```python
import functools

import jax
import jax.numpy as jnp
from jax import lax
from jax.experimental import pallas as pl
from jax.experimental.pallas import tpu as pltpu

F32 = jnp.float32
BF16 = jnp.bfloat16

D_MODEL = 1024
MIX_W = D_MODEL
DN_HEADS = 8
DN_HEAD_DIM = 128
SB_HEADS = 16
SB_HEAD_DIM = 64
SSM_HEADS = 16
SSM_HEAD_DIM = 64
SSM_STATE = 128
SSM_GROUPS = 4
CHUNK = 64
D_FF = 4 * D_MODEL
EPS = 1e-6
CONV_WIDTH = 4

LANES = 128
TBLK = 2 * CHUNK
HALO = 8
DN_BASE_BLOCK = 8
DN_TILES = 2

COL_DN_QKV = 0
COL_DN_GATE = 3072
COL_SB_QKV = 4096
COL_SSM_Z = 7168
COL_SSM_XBC = 8192
COL_GATES = 10240
N_BIG = 13312
LANE_DN_A = 0
LANE_DN_B = 8
LANE_SSM_DT = 16

VMEM_LIMIT = 56 * 1024 * 1024


def _cparams(*sem):
    return pltpu.CompilerParams(dimension_semantics=sem, vmem_limit_bytes=VMEM_LIMIT)


def _bf(x):
    return x.astype(BF16)


def _mm(a, b):
    return jnp.dot(_bf(a), _bf(b), preferred_element_type=F32)


def _mm_nt(a, b):
    return lax.dot_general(_bf(a), _bf(b), (((1,), (1,)), ((), ())), preferred_element_type=F32)


def _split2(a):
    hi = _bf(a)
    lo = _bf(a - hi.astype(F32))
    return hi, lo


def _split3(a):
    hi = _bf(a)
    r = a - hi.astype(F32)
    mid = _bf(r)
    lo = _bf(r - mid.astype(F32))
    return hi, mid, lo


def _mm_x3s(a_split, b_split):
    ah, al = a_split
    bh, bl = b_split
    lhs = jnp.concatenate([ah, al, ah], axis=1)
    rhs = jnp.concatenate([bh, bh, bl], axis=0)
    return jnp.dot(lhs, rhs, preferred_element_type=F32)


def _mm_sel_lhs(sel, b):
    s = _bf(sel)
    bh, bm, bl = _split3(b)
    return jnp.dot(jnp.concatenate([s, s, s], axis=1), jnp.concatenate([bh, bm, bl], axis=0),
                   preferred_element_type=F32)


def _mm_sel_rhs(a, sel_bf):
    ah, am, al = _split3(a)
    return jnp.dot(jnp.concatenate([ah, am, al], axis=1), jnp.concatenate([sel_bf, sel_bf, sel_bf], axis=0),
                   preferred_element_type=F32)


def _softplus(x):
    return jnp.maximum(x, 0.0) + jnp.log(1.0 + jnp.exp(-jnp.abs(x)))


def _sigmoid(x):
    return 1.0 / (1.0 + jnp.exp(-x))


def _silu(x):
    return x * _sigmoid(x)


def _rms(x, w):
    return x * lax.rsqrt(jnp.mean(x * x, axis=-1, keepdims=True) + EPS) * w


def _causal_conv(halo, cur, w, bias):
    rows = cur.shape[0]
    xx = jnp.concatenate([halo, cur], axis=0)
    y = w[CONV_WIDTH - 1:CONV_WIDTH, :] * cur
    for k in range(CONV_WIDTH - 1):
        shifted = pltpu.roll(xx, CONV_WIDTH - 1 - k, 0)[HALO:HALO + rows]
        y = y + w[k:k + 1, :] * shifted
    if bias is not None:
        y = y + bias
    return y


def _chunk_masks():
    row = lax.broadcasted_iota(jnp.int32, (TBLK, TBLK), 0)
    col = lax.broadcasted_iota(jnp.int32, (TBLK, TBLK), 1)
    same = (row >= CHUNK) == (col >= CHUNK)
    return row, col, same & (col <= row), same & (col < row)


def _inproj_kernel(x_ref, g_ref, wb_ref, ws_ref, big_ref, small_ref, xn_scr):
    @pl.when(pl.program_id(1) == 0)
    def _():
        xn_scr[...] = _bf(_rms(x_ref[...], g_ref[...]))
        small_ref[...] = jnp.dot(xn_scr[...], ws_ref[...], preferred_element_type=F32)

    big_ref[...] = _bf(jnp.dot(xn_scr[...], wb_ref[...], preferred_element_type=F32))


def _inproj(x, g, w_big, w_small, layer, tm=2048, tn=1024):
    t = x.shape[0]
    tm = min(tm, t)
    return pl.pallas_call(
        _inproj_kernel,
        grid=(t // tm, N_BIG // tn),
        in_specs=[
            pl.BlockSpec((tm, D_MODEL), lambda i, j: (i, 0)),
            pl.BlockSpec((1, D_MODEL), lambda i, j: (0, 0)),
            pl.BlockSpec((None, D_MODEL, tn), lambda i, j: (layer, 0, j)),
            pl.BlockSpec((None, D_MODEL, LANES), lambda i, j: (layer, 0, 0)),
        ],
        out_specs=[
            pl.BlockSpec((tm, tn), lambda i, j: (i, j)),
            pl.BlockSpec((tm, LANES), lambda i, j: (i, 0)),
        ],
        out_shape=[jax.ShapeDtypeStruct((t, N_BIG), BF16), jax.ShapeDtypeStruct((t, LANES), F32)],
        scratch_shapes=[pltpu.VMEM((tm, D_MODEL), BF16)],
        compiler_params=_cparams("parallel", "arbitrary"),
        name="inproj",
    )(x, g, w_big, w_small)


def _dn_kernel(qkv_ref, gate_ref, sm_ref, cw_ref, prow_ref, nw_ref, o_ref, halo_scr, state_scr):
    @pl.when(pl.program_id(1) == 0)
    def _():
        halo_scr[...] = jnp.zeros_like(halo_scr)
        state_scr[...] = jnp.zeros_like(state_scr)

    row, col, m_incl, m_strict = _chunk_masks()
    tril = jnp.where(m_incl, 1.0, 0.0)
    eye = jnp.where(row == col, 1.0, 0.0)
    first = row[:, 0:1] < CHUNK

    units = [(j, h) for j in range(DN_TILES) for h in range(DN_HEADS)]
    heads = range(DN_HEADS)
    q, k, v, kb, vb, decay, gc_col, kk, qk = {}, {}, {}, {}, {}, {}, {}, {}, {}
    low, xs, tinv, cd = {}, {}, {}, {}
    u, w, attn, qg, kdt, gl = {}, {}, {}, {}, {}, {}
    pad = jnp.zeros((CHUNK, DN_HEAD_DIM), F32)

    def same_block(size):
        shift = size.bit_length() - 1
        return (row >> shift) == (col >> shift)

    def prepare(j):
        rows = slice(j * TBLK, (j + 1) * TBLK)
        sm = sm_ref[rows, :]
        g_all = -jnp.exp(prow_ref[0:1, :]) * _softplus(sm + prow_ref[1:2, :])
        beta_all = _sigmoid(sm)
        gc_all = _mm_sel_lhs(tril, g_all)
        gct_all = gc_all.T
        for h in heads:
            def conv_silu(seg):
                lanes = slice(seg * MIX_W + h * DN_HEAD_DIM, seg * MIX_W + (h + 1) * DN_HEAD_DIM)
                if j == 0:
                    halo = halo_scr[:, lanes]
                else:
                    halo = qkv_ref[j * TBLK - 16:j * TBLK, lanes].astype(F32)[16 - HALO:16]
                return _silu(_causal_conv(halo, qkv_ref[rows, lanes].astype(F32), cw_ref[:, lanes], None))

            un = (j, h)
            qh = conv_silu(0)
            kh = conv_silu(1)
            v[un] = conv_silu(2)
            q[un] = qh * lax.rsqrt(jnp.sum(qh * qh, axis=-1, keepdims=True) + EPS) * (DN_HEAD_DIM ** -0.5)
            k[un] = kh * lax.rsqrt(jnp.sum(kh * kh, axis=-1, keepdims=True) + EPS)
            gc_col[un] = gc_all[:, LANE_DN_A + h:LANE_DN_A + h + 1]
            beta = beta_all[:, LANE_DN_B + h:LANE_DN_B + h + 1]
            gc_row = gct_all[LANE_DN_A + h:LANE_DN_A + h + 1, :]
            decay[un] = jnp.exp(jnp.where(m_incl, gc_col[un] - gc_row, -jnp.inf))
            kb[un] = k[un] * beta
            vb[un] = v[un] * beta
    def gram(units):
        for un in units:
            kk[un] = _mm_nt(kb[un], k[un])
            qk[un] = _mm_nt(q[un], k[un])

    def solve(units):
        for un in units:
            low[un] = jnp.where(m_strict, kk[un] * decay[un], 0.0)
            x = -jnp.where(same_block(DN_BASE_BLOCK), low[un], 0.0)
            xs[un] = _split2(x)
            tinv[un] = eye + x
        for _ in range(2):
            for un in units:
                xs[un] = _split2(_mm_x3s(xs[un], xs[un]))
            for un in units:
                tinv[un] = tinv[un] + _mm_x3s(_split2(tinv[un]), xs[un])
        size = DN_BASE_BLOCK
        while size < CHUNK:
            off_diag = same_block(2 * size) & jnp.logical_not(same_block(size))
            for un in units:
                cd[un] = _mm(jnp.where(off_diag, low[un], 0.0), tinv[un])
            for un in units:
                tinv[un] = tinv[un] - _mm(tinv[un], cd[un])
            size *= 2
        for un in units:
            egc = jnp.exp(gc_col[un])
            u[un] = _mm(tinv[un], vb[un])
            w[un] = _mm(tinv[un], kb[un] * egc)
            attn[un] = _bf(jnp.where(m_incl, qk[un] * decay[un], 0.0))
            qg[un] = _bf(q[un] * egc)
            gl[un] = (gc_col[un][CHUNK - 1:CHUNK, :], gc_col[un][TBLK - 1:TBLK, :])
            kdt[un] = _bf((k[un] * jnp.exp(jnp.where(first, gl[un][0], gl[un][1]) - gc_col[un])).T)

    s = {h: state_scr[h] for h in heads}

    def scan(j):
        outs = {h: [] for h in heads}
        for c in range(2):
            r0, r1 = c * CHUNK, (c + 1) * CHUNK
            v_blk = {}
            for h in heads:
                un = (j, h)
                v_new = u[un][r0:r1] - _mm(w[un][r0:r1], s[h])
                v_blk[h] = _bf(jnp.concatenate([v_new, pad] if c == 0 else [pad, v_new], axis=0))
            for h in heads:
                un = (j, h)
                outs[h].append(_mm(qg[un][r0:r1], s[h]) + _mm(attn[un][r0:r1, :], v_blk[h]))
                s[h] = s[h] * jnp.exp(gl[un][c]) + _mm(kdt[un], v_blk[h])
        for h in heads:
            lanes = slice(h * DN_HEAD_DIM, (h + 1) * DN_HEAD_DIM)
            rows = slice(j * TBLK, (j + 1) * TBLK)
            o = jnp.concatenate(outs[h], axis=0)
            o_ref[rows, lanes] = _bf(_rms(o, nw_ref[...]) * _silu(gate_ref[rows, lanes].astype(F32)))

    for j in range(DN_TILES):
        prepare(j)
    gram(units)
    solve(units)
    for j in range(DN_TILES):
        scan(j)
    for h in heads:
        state_scr[h] = s[h]
    last = DN_TILES * TBLK
    halo_scr[...] = qkv_ref[last - 16:last, :].astype(F32)[16 - HALO:16]


def _dn_mixer(big, small, conv_w, prow, norm_w, batch, seq):
    rows = DN_TILES * TBLK
    nd = seq // rows
    row_map = lambda b, d: (b * nd + d, 0)
    return pl.pallas_call(
        _dn_kernel,
        grid=(batch, nd),
        in_specs=[
            pl.BlockSpec((rows, 3 * MIX_W), lambda b, d: (b * nd + d, COL_DN_QKV // (3 * MIX_W))),
            pl.BlockSpec((rows, MIX_W), lambda b, d: (b * nd + d, COL_DN_GATE // MIX_W)),
            pl.BlockSpec((rows, LANES), row_map),
            pl.BlockSpec((CONV_WIDTH, 3 * MIX_W), lambda b, d: (0, 0)),
            pl.BlockSpec((8, LANES), lambda b, d: (0, 0)),
            pl.BlockSpec((1, DN_HEAD_DIM), lambda b, d: (0, 0)),
        ],
        out_specs=pl.BlockSpec((rows, MIX_W), row_map),
        out_shape=jax.ShapeDtypeStruct((batch * seq, MIX_W), BF16),
        scratch_shapes=[pltpu.VMEM((HALO, 3 * MIX_W), F32), pltpu.VMEM((DN_HEADS, DN_HEAD_DIM, DN_HEAD_DIM), F32)],
        compiler_params=_cparams("parallel", "arbitrary"),
        name="deltanet",
    )(big, big, small, conv_w, prow, norm_w)


SB_TILE = 128
SB_TQ = 256
SB_SUB = SB_TQ // SB_TILE
SB_STRIP = 256
SB_HEADS_PER_STEP = 4
LOG2E = 1.4426950408889634
SB_LAG = 4


def _sb_kernel(q_ref, k_ref, v_ref, w_ref, o_ref, r_scr, acc_scr, za_scr, zb_scr, wgt_scr):
    qi = pl.program_id(2)
    row = lax.broadcasted_iota(jnp.int32, (SB_STRIP, SB_TILE), 0)
    col = lax.broadcasted_iota(jnp.int32, (SB_STRIP, SB_TILE), 1)
    wmat = w_ref[...]
    r_scr[...] = jnp.zeros_like(r_scr)
    acc_scr[...] = jnp.zeros_like(acc_scr)
    strips = range(SB_TQ // SB_STRIP)
    heads = range(SB_HEADS_PER_STEP)
    pairs = range(SB_HEADS_PER_STEP // 2)
    order = list(reversed(range(SB_SUB)))
    qh = {}
    for s in strips:
        for h in heads:
            lanes = slice((h // 2) * LANES, (h // 2 + 1) * LANES)
            q = q_ref[s * SB_STRIP:(s + 1) * SB_STRIP, lanes].astype(F32) * (SB_HEAD_DIM ** -0.5)
            qh[s, h] = _bf(jnp.where((col >= SB_HEAD_DIM) == (h % 2 == 1), q, 0.0))

    def chain_id(t, s, h):
        return (t * len(strips) + s) * SB_HEADS_PER_STEP + h

    def masked_out(t, s, diagonal):
        return diagonal and t * SB_TILE >= (s + 1) * SB_STRIP

    def block_rows(kb_lo):
        return pl.ds(pl.multiple_of(kb_lo * SB_TILE, SB_TILE), SB_SUB * SB_TILE)

    def score_ops(kb_lo, z_dst, diagonal):
        def one(t, s, p):
            def emit():
                rows = pl.ds(pl.multiple_of((kb_lo + t) * SB_TILE, SB_TILE), SB_TILE)
                kblk = k_ref[rows, p * LANES:(p + 1) * LANES]
                zz = _mm_nt(jnp.concatenate([qh[s, 2 * p], qh[s, 2 * p + 1]], axis=0), kblk) * LOG2E
                z_dst[chain_id(t, s, 2 * p)] = zz[:SB_STRIP]
                z_dst[chain_id(t, s, 2 * p + 1)] = zz[SB_STRIP:]
            return emit
        return [one(t, s, p) for t in order for s in strips for p in pairs if not masked_out(t, s, diagonal)]

    def value_ops(kb_lo):
        def one(p):
            def emit():
                vblk = v_ref[block_rows(kb_lo), p * LANES:(p + 1) * LANES]
                both = jnp.dot(jnp.concatenate([wgt_scr[2 * p], wgt_scr[2 * p + 1]], axis=0), vblk,
                               preferred_element_type=F32)
                acc_scr[2 * p] += both[:SB_TQ]
                acc_scr[2 * p + 1] += both[SB_TQ:]
            return emit
        return [one(p) for p in pairs]

    def weight_stage(z_src, diagonal, extras):
        chains = [(t, s, h) for t in order for s in strips for h in heads]
        cs, tri = {}, {}

        def cumsum(group):
            sps = []
            for c in group:
                t, s, _ = c
                z = z_src[chain_id(*c)]
                neg_abs = pltpu.bitcast(pltpu.bitcast(z, jnp.uint32) | jnp.uint32(0x80000000), F32)
                sp = jnp.maximum(z, 0.0) + jnp.log2(1.0 + jnp.exp2(neg_abs))
                if diagonal:
                    tri[c] = col + t * SB_TILE < row + s * SB_STRIP
                    sp = jnp.where(tri[c], sp, 0.0)
                sps.append(_bf(sp))
            both = jnp.dot(jnp.concatenate(sps, axis=0), wmat, preferred_element_type=F32)
            for i, c in enumerate(group):
                cs[c] = both[i * SB_STRIP:(i + 1) * SB_STRIP]

        def weigh(c):
            t, s, h = c
            rows = slice(s * SB_STRIP, (s + 1) * SB_STRIP)
            lanes = slice(t * SB_TILE, (t + 1) * SB_TILE)
            wgt = jnp.exp2(z_src[chain_id(*c)] - cs[c][:, :SB_TILE] - r_scr[h, rows])
            wgt_scr[h, rows, lanes] = _bf(jnp.where(tri[c], wgt, 0.0) if diagonal else wgt)
            r_scr[h, rows] += cs[c][:, SB_TILE:]

        live = [c for c in chains if not masked_out(c[0], c[1], diagonal)]
        for t, s, h in chains:
            if masked_out(t, s, diagonal):
                wgt_scr[h, s * SB_STRIP:(s + 1) * SB_STRIP, t * SB_TILE:(t + 1) * SB_TILE] = (
                    jnp.zeros((SB_STRIP, SB_TILE), BF16))
        groups = [live[i:i + 2] for i in range(0, len(live), 2)]
        lag = SB_LAG // 2
        per_slot = -(-len(extras) // len(groups))
        for i, group in enumerate(groups):
            cumsum(group)
            for emit in extras[i * per_slot:(i + 1) * per_slot]:
                emit()
            if i >= lag:
                for c in groups[i - lag]:
                    weigh(c)
        for group in groups[-lag:]:
            for c in group:
                weigh(c)

    def step(cur, z_src, z_dst, issue_next=True):
        extras = value_ops((cur + 1) * SB_SUB)
        if issue_next:
            extras = extras + score_ops(jnp.maximum(cur - 1, 0) * SB_SUB, z_dst, False)
        weight_stage(z_src, False, extras)

    for emit in score_ops(qi * SB_SUB, za_scr, True):
        emit()
    weight_stage(za_scr, True, score_ops(jnp.maximum(qi - 1, 0) * SB_SUB, zb_scr, False))

    def body(j, carry):
        step(qi - 1 - 2 * j, zb_scr, za_scr)
        step(qi - 2 - 2 * j, za_scr, zb_scr)
        return carry

    lax.fori_loop(0, qi // 2, body, 0)

    @pl.when(qi % 2 == 1)
    def _():
        step(0, zb_scr, za_scr, issue_next=False)

    for emit in value_ops(0):
        emit()
    lane = lax.broadcasted_iota(jnp.int32, (SB_TQ, SB_TILE), 1)
    for p in range(SB_HEADS_PER_STEP // 2):
        o_ref[:, p * LANES:(p + 1) * LANES] = _bf(jnp.where(lane < SB_HEAD_DIM, acc_scr[2 * p], acc_scr[2 * p + 1]))


def _sb_mixer(big, wmat, batch, seq):
    nq = seq // SB_TQ
    width = SB_HEADS_PER_STEP * SB_HEAD_DIM
    groups = MIX_W // width
    qoff = COL_SB_QKV // width
    return pl.pallas_call(
        _sb_kernel,
        grid=(batch, groups, nq),
        in_specs=[
            pl.BlockSpec((SB_TQ, width), lambda b, p, i: (b * nq + i, qoff + p)),
            pl.BlockSpec((seq, width), lambda b, p, i: (b, qoff + groups + p)),
            pl.BlockSpec((seq, width), lambda b, p, i: (b, qoff + 2 * groups + p)),
            pl.BlockSpec((SB_TILE, 2 * SB_TILE), lambda b, p, i: (0, 0)),
        ],
        out_specs=pl.BlockSpec((SB_TQ, width), lambda b, p, i: (b * nq + i, p)),
        out_shape=jax.ShapeDtypeStruct((batch * seq, MIX_W), BF16),
        scratch_shapes=[pltpu.VMEM((SB_HEADS_PER_STEP, SB_TQ, SB_TILE), F32),
                        pltpu.VMEM((SB_HEADS_PER_STEP, SB_TQ, SB_TILE), F32),
                        pltpu.VMEM((SB_SUB * (SB_TQ // SB_STRIP) * SB_HEADS_PER_STEP, SB_STRIP, SB_TILE), F32),
                        pltpu.VMEM((SB_SUB * (SB_TQ // SB_STRIP) * SB_HEADS_PER_STEP, SB_STRIP, SB_TILE), F32),
                        pltpu.VMEM((SB_HEADS_PER_STEP, SB_TQ, SB_SUB * SB_TILE), BF16)],
        compiler_params=_cparams("parallel", "parallel", "arbitrary"),
        name="stickbreak",
    )(big, big, big, wmat)


SSD_TILES = 2
GROUP_W = MIX_W // SSM_GROUPS
HEADS_PER_GROUP = SSM_HEADS // SSM_GROUPS


def _ssd_kernel(z_ref, xbc_ref, sm_ref, cw_ref, cb_ref, prow_ref, drow_ref, nw_ref, e_ref, o_ref,
                halo_scr, state_scr):
    @pl.when(pl.program_id(1) == 0)
    def _():
        halo_scr[...] = jnp.zeros_like(halo_scr)
        state_scr[...] = jnp.zeros_like(state_scr)

    row, col, m_incl, _ = _chunk_masks()
    tril = jnp.where(m_incl, 1.0, 0.0)
    first = row[:, 0:1] < CHUNK
    gcol = lax.broadcasted_iota(jnp.int32, (TBLK, GROUP_W), 1)

    gn = SSM_GROUPS * SSM_STATE
    e = e_ref[...]
    for j in range(SSD_TILES):
        rows = slice(j * TBLK, (j + 1) * TBLK)
        cur = xbc_ref[rows, :].astype(F32)
        halo = halo_scr[...] if j == 0 else xbc_ref[j * TBLK - 16:j * TBLK, :].astype(F32)[16 - HALO:16]
        xbc = _silu(_causal_conv(halo, cur, cw_ref[...], cb_ref[...]))
        xs = xbc[:, :MIX_W]
        bm = xbc[:, MIX_W:MIX_W + gn]
        cm = xbc[:, MIX_W + gn:]

        dt_all = _softplus(sm_ref[rows, :] + prow_ref[1:2, :])
        a_all = -jnp.exp(prow_ref[0:1, :]) * dt_all
        acum_all = _mm_sel_lhs(tril, a_all)
        acum_t = acum_all.T
        dt_exp = _mm_sel_rhs(dt_all, e)
        acum_exp = _mm_sel_rhs(acum_all, e)
        xdt = xs * dt_exp

        groups = range(SSM_GROUPS)
        c_g = {g: cm[:, g * SSM_STATE:(g + 1) * SSM_STATE] for g in groups}
        b_g = {g: bm[:, g * SSM_STATE:(g + 1) * SSM_STATE] for g in groups}
        x_g = {g: xdt[:, g * GROUP_W:(g + 1) * GROUP_W] for g in groups}
        a_g = {g: acum_exp[:, g * GROUP_W:(g + 1) * GROUP_W] for g in groups}
        scores = {g: _mm_nt(c_g[g], b_g[g]) for g in groups}
        y_g = {g: jnp.zeros((TBLK, GROUP_W), F32) for g in groups}
        for hh in range(HEADS_PER_GROUP):
            for g in groups:
                h = g * HEADS_PER_GROUP + hh
                a_col = acum_exp[:, h * SSM_HEAD_DIM:h * SSM_HEAD_DIM + 1]
                a_row = acum_t[LANE_SSM_DT + h:LANE_SSM_DT + h + 1, :]
                lmat = jnp.exp(jnp.where(m_incl, a_col - a_row, -jnp.inf))
                y_g[g] = jnp.where(gcol // SSM_HEAD_DIM == hh, _mm(scores[g] * lmat, x_g[g]), y_g[g])

        b_t = {g: b_g[g].T for g in groups}
        a_last = {g: (a_g[g][CHUNK - 1:CHUNK, :], a_g[g][TBLK - 1:TBLK, :]) for g in groups}
        x_sc = {g: x_g[g] * jnp.exp(jnp.where(first, a_last[g][0], a_last[g][1]) - a_g[g]) for g in groups}
        st = {g: state_scr[g] for g in groups}
        yoff = {g: [] for g in groups}
        for c in range(2):
            r0, r1 = c * CHUNK, (c + 1) * CHUNK
            for g in groups:
                yoff[g].append(_mm(c_g[g][r0:r1], st[g]) * jnp.exp(a_g[g][r0:r1]))
            for g in groups:
                st[g] = st[g] * jnp.exp(a_last[g][c]) + _mm(b_t[g], jnp.where(first == (c == 0), x_sc[g], 0.0))
        for g in groups:
            lo, hi = g * GROUP_W, (g + 1) * GROUP_W
            state_scr[g] = st[g]
            y = y_g[g] + jnp.concatenate(yoff[g], axis=0) + xs[:, lo:hi] * drow_ref[:, lo:hi]
            y = y * _silu(z_ref[rows, lo:hi].astype(F32))
            o_ref[rows, lo:hi] = _bf(_rms(y, nw_ref[:, lo:hi]))
    last = SSD_TILES * TBLK
    halo_scr[...] = xbc_ref[last - 16:last, :].astype(F32)[16 - HALO:16]


def _ssd_mixer(big, small, conv_w, conv_b, prow, drow, norm_w, emat, batch, seq):
    rows = SSD_TILES * TBLK
    nd = seq // rows
    conv_dim = MIX_W + 2 * SSM_GROUPS * SSM_STATE
    row_map = lambda b, d: (b * nd + d, 0)
    const = lambda b, d: (0, 0)
    return pl.pallas_call(
        _ssd_kernel,
        grid=(batch, nd),
        in_specs=[
            pl.BlockSpec((rows, MIX_W), lambda b, d: (b * nd + d, COL_SSM_Z // MIX_W)),
            pl.BlockSpec((rows, conv_dim), lambda b, d: (b * nd + d, COL_SSM_XBC // conv_dim)),
            pl.BlockSpec((rows, LANES), row_map),
            pl.BlockSpec((CONV_WIDTH, conv_dim), const),
            pl.BlockSpec((1, conv_dim), const),
            pl.BlockSpec((8, LANES), const),
            pl.BlockSpec((1, MIX_W), const),
            pl.BlockSpec((1, MIX_W), const),
            pl.BlockSpec((LANES, MIX_W), const),
        ],
        out_specs=pl.BlockSpec((rows, MIX_W), row_map),
        out_shape=jax.ShapeDtypeStruct((batch * seq, MIX_W), BF16),
        scratch_shapes=[pltpu.VMEM((HALO, conv_dim), F32), pltpu.VMEM((SSM_GROUPS, SSM_STATE, GROUP_W), F32)],
        compiler_params=_cparams("parallel", "arbitrary"),
        name="ssd",
    )(big, big, small, conv_w, conv_b, prow, drow, norm_w, emat)


def _merge_kernel(x_ref, odn_ref, osb_ref, ossm_ref, g0_ref, g1_ref, g2_ref, wb_ref, wo_ref, o_ref):
    m = _sigmoid(g0_ref[...].astype(F32)) * jnp.dot(odn_ref[...], wb_ref[0], preferred_element_type=F32)
    m = m + _sigmoid(g1_ref[...].astype(F32)) * jnp.dot(osb_ref[...], wb_ref[1], preferred_element_type=F32)
    m = m + _sigmoid(g2_ref[...].astype(F32)) * jnp.dot(ossm_ref[...], wb_ref[2], preferred_element_type=F32)
    o_ref[...] = x_ref[...] + jnp.dot(_bf(m), wo_ref[...], preferred_element_type=F32)


def _merge(x, o_dn, o_sb, o_ssm, big, w_branch, w_out, layer, tm=512):
    t = x.shape[0]
    tm = min(tm, t)
    rows = lambda i: (i, 0)
    gcol = COL_GATES // D_MODEL
    return pl.pallas_call(
        _merge_kernel,
        grid=(t // tm,),
        in_specs=[
            pl.BlockSpec((tm, D_MODEL), rows),
            pl.BlockSpec((tm, MIX_W), rows),
            pl.BlockSpec((tm, MIX_W), rows),
            pl.BlockSpec((tm, MIX_W), rows),
            pl.BlockSpec((tm, D_MODEL), lambda i: (i, gcol)),
            pl.BlockSpec((tm, D_MODEL), lambda i: (i, gcol + 1)),
            pl.BlockSpec((tm, D_MODEL), lambda i: (i, gcol + 2)),
            pl.BlockSpec((None, 3, MIX_W, D_MODEL), lambda i: (layer, 0, 0, 0)),
            pl.BlockSpec((None, D_MODEL, D_MODEL), lambda i: (layer, 0, 0)),
        ],
        out_specs=pl.BlockSpec((tm, D_MODEL), rows),
        out_shape=jax.ShapeDtypeStruct((t, D_MODEL), F32),
        compiler_params=_cparams("parallel"),
        name="merge",
    )(x, o_dn, o_sb, o_ssm, big, big, big, w_branch, w_out)


def _mlp_kernel(x_ref, g_ref, wu_ref, wd_ref, gf_ref, o_ref, xn_scr, acc_scr, *, final_norm):
    j = pl.program_id(1)

    @pl.when(j == 0)
    def _():
        xn_scr[...] = _bf(_rms(x_ref[...], g_ref[...]))
        acc_scr[...] = jnp.zeros_like(acc_scr)

    h = jnp.dot(xn_scr[...], wu_ref[...], preferred_element_type=F32)
    h = jnp.square(jnp.maximum(h, 0.0))
    acc_scr[...] += jnp.dot(_bf(h), wd_ref[...], preferred_element_type=F32)

    @pl.when(j == pl.num_programs(1) - 1)
    def _():
        y = x_ref[...] + acc_scr[...]
        o_ref[...] = _rms(y, gf_ref[...]) if final_norm else y


def _mlp(x, g, w_up, w_down, g_final, layer, final_norm, tm=1024, tf=2048):
    t = x.shape[0]
    tm = min(tm, t)
    return pl.pallas_call(
        functools.partial(_mlp_kernel, final_norm=final_norm),
        grid=(t // tm, D_FF // tf),
        in_specs=[
            pl.BlockSpec((tm, D_MODEL), lambda i, j: (i, 0)),
            pl.BlockSpec((1, D_MODEL), lambda i, j: (0, 0)),
            pl.BlockSpec((None, D_MODEL, tf), lambda i, j: (layer, 0, j)),
            pl.BlockSpec((None, tf, D_MODEL), lambda i, j: (layer, j, 0)),
            pl.BlockSpec((1, D_MODEL), lambda i, j: (0, 0)),
        ],
        out_specs=pl.BlockSpec((tm, D_MODEL), lambda i, j: (i, 0)),
        out_shape=jax.ShapeDtypeStruct((t, D_MODEL), F32),
        scratch_shapes=[pltpu.VMEM((tm, D_MODEL), BF16), pltpu.VMEM((tm, D_MODEL), F32)],
        compiler_params=_cparams("parallel", "arbitrary"),
        name="mlp",
    )(x, g, w_up, w_down, g_final)


def _lane_row(values, lane0):
    return jnp.zeros((LANES,), F32).at[lane0:lane0 + values.shape[0]].set(values.astype(F32))


def _param_rows(a_log, dt_bias, lane0):
    rows = jnp.zeros((8, LANES), F32)
    return rows.at[0].set(_lane_row(a_log, lane0)).at[1].set(_lane_row(dt_bias, lane0))


def _sb_cumsum_matrix():
    j = jnp.arange(SB_TILE)[:, None]
    s = jnp.arange(SB_TILE)[None, :]
    return jnp.concatenate([(j >= s).astype(BF16), jnp.ones((SB_TILE, SB_TILE), BF16)], axis=1)


def _ssm_expand_matrix():
    lane = jnp.arange(LANES)[:, None]
    ch = jnp.arange(MIX_W)[None, :]
    return (lane == LANE_SSM_DT + ch // SSM_HEAD_DIM).astype(BF16)


def _split_w_in(w):
    w = _bf(w)
    big = jnp.concatenate([w[..., 0:4096], w[..., 4112:10256], w[..., 10272:13344]], axis=-1)
    pad = jnp.zeros(w.shape[:-1] + (LANES - 32,), w.dtype)
    small = jnp.concatenate([w[..., 4096:4112], w[..., 10256:10272], pad], axis=-1)
    return big, small


def kernel(x, norm_mix, w_in, dn_conv_w, dn_a_log, dn_dt_bias, dn_norm_w, ssm_conv_w, ssm_conv_b, ssm_a_log,
           ssm_dt_bias, ssm_d, ssm_norm_w, w_branch, w_out, norm_mlp, w_up, w_down, norm_final):
    batch, seq, _ = x.shape
    depth = w_in.shape[0]
    h = x.reshape(batch * seq, D_MODEL)
    sb_w = _sb_cumsum_matrix()
    emat = _ssm_expand_matrix()
    g_final = norm_final.reshape(1, D_MODEL)
    w_big, w_small = _split_w_in(w_in)
    w_branch, w_out, w_up, w_down = _bf(w_branch), _bf(w_out), _bf(w_up), _bf(w_down)
    for l in range(depth):
        big, small = _inproj(h, norm_mix[l].reshape(1, D_MODEL), w_big, w_small, l)
        o_dn = _dn_mixer(big, small, dn_conv_w[l], _param_rows(dn_a_log[l], dn_dt_bias[l], LANE_DN_A),
                         dn_norm_w[l].reshape(1, DN_HEAD_DIM), batch, seq)
        o_sb = _sb_mixer(big, sb_w, batch, seq)
        o_ssm = _ssd_mixer(big, small, ssm_conv_w[l], ssm_conv_b[l].reshape(1, -1),
                           _param_rows(ssm_a_log[l], ssm_dt_bias[l], LANE_SSM_DT),
                           jnp.repeat(ssm_d[l], SSM_HEAD_DIM).reshape(1, MIX_W),
                           ssm_norm_w[l].reshape(1, MIX_W), emat, batch, seq)
        h = _merge(h, o_dn, o_sb, o_ssm, big, w_branch, w_out, l)
        h = _mlp(h, norm_mlp[l].reshape(1, D_MODEL), w_up, w_down, g_final, l, final_norm=(l == depth - 1))
    return h.reshape(batch, seq, D_MODEL)
```

```python
import functools

import jax
import jax.numpy as jnp
from jax import lax
from jax.experimental import pallas as pl
from jax.experimental.pallas import tpu as pltpu

F32 = jnp.float32
BF16 = jnp.bfloat16

D_MODEL = 1024
MIX_W = D_MODEL
DN_HEADS = 8
DN_HEAD_DIM = 128
SB_HEADS = 16
SB_HEAD_DIM = 64
SSM_HEADS = 16
SSM_HEAD_DIM = 64
SSM_STATE = 128
SSM_GROUPS = 4
CHUNK = 64
D_FF = 4 * D_MODEL
EPS = 1e-6
CONV_WIDTH = 4

LANES = 128
TBLK = 2 * CHUNK
HALO = 8
DN_BASE_BLOCK = 8
DN_TILES = 2

COL_DN_QKV = 0
COL_DN_GATE = 3072
COL_SB_QKV = 4096
COL_SSM_Z = 7168
COL_SSM_XBC = 8192
COL_GATES = 10240
N_BIG = 13312
LANE_DN_A = 0
LANE_DN_B = 8
LANE_SSM_DT = 16

VMEM_LIMIT = 56 * 1024 * 1024


def _cparams(*sem):
    return pltpu.CompilerParams(dimension_semantics=sem, vmem_limit_bytes=VMEM_LIMIT)


def _bf(x):
    return x.astype(BF16)


def _mm(a, b):
    return jnp.dot(_bf(a), _bf(b), preferred_element_type=F32)


def _mm_nt(a, b):
    return lax.dot_general(_bf(a), _bf(b), (((1,), (1,)), ((), ())), preferred_element_type=F32)


def _split2(a):
    hi = _bf(a)
    lo = _bf(a - hi.astype(F32))
    return hi, lo


def _split3(a):
    hi = _bf(a)
    r = a - hi.astype(F32)
    mid = _bf(r)
    lo = _bf(r - mid.astype(F32))
    return hi, mid, lo


def _mm_x3s(a_split, b_split):
    ah, al = a_split
    bh, bl = b_split
    lhs = jnp.concatenate([ah, al, ah], axis=1)
    rhs = jnp.concatenate([bh, bh, bl], axis=0)
    return jnp.dot(lhs, rhs, preferred_element_type=F32)


def _mm_sel_lhs(sel, b):
    s = _bf(sel)
    bh, bm, bl = _split3(b)
    return jnp.dot(jnp.concatenate([s, s, s], axis=1), jnp.concatenate([bh, bm, bl], axis=0),
                   preferred_element_type=F32)


def _mm_sel_rhs(a, sel_bf):
    ah, am, al = _split3(a)
    return jnp.dot(jnp.concatenate([ah, am, al], axis=1), jnp.concatenate([sel_bf, sel_bf, sel_bf], axis=0),
                   preferred_element_type=F32)


def _softplus(x):
    return jnp.maximum(x, 0.0) + jnp.log(1.0 + jnp.exp(-jnp.abs(x)))


def _sigmoid(x):
    return 1.0 / (1.0 + jnp.exp(-x))


def _silu(x):
    return x * _sigmoid(x)


def _rms(x, w):
    return x * lax.rsqrt(jnp.mean(x * x, axis=-1, keepdims=True) + EPS) * w


def _causal_conv(halo, cur, w, bias):
    rows = cur.shape[0]
    xx = jnp.concatenate([halo, cur], axis=0)
    y = w[CONV_WIDTH - 1:CONV_WIDTH, :] * cur
    for k in range(CONV_WIDTH - 1):
        shifted = pltpu.roll(xx, CONV_WIDTH - 1 - k, 0)[HALO:HALO + rows]
        y = y + w[k:k + 1, :] * shifted
    if bias is not None:
        y = y + bias
    return y


def _chunk_masks():
    row = lax.broadcasted_iota(jnp.int32, (TBLK, TBLK), 0)
    col = lax.broadcasted_iota(jnp.int32, (TBLK, TBLK), 1)
    same = (row >= CHUNK) == (col >= CHUNK)
    return row, col, same & (col <= row), same & (col < row)


def _inproj_kernel(x_ref, g_ref, wb_ref, ws_ref, big_ref, small_ref, xn_scr):
    @pl.when(pl.program_id(1) == 0)
    def _():
        xn_scr[...] = _bf(_rms(x_ref[...], g_ref[...]))
        small_ref[...] = jnp.dot(xn_scr[...], ws_ref[...], preferred_element_type=F32)

    big_ref[...] = _bf(jnp.dot(xn_scr[...], wb_ref[...], preferred_element_type=F32))


def _inproj(x, g, w_big, w_small, layer, tm=2048, tn=1024):
    t = x.shape[0]
    tm = min(tm, t)
    return pl.pallas_call(
        _inproj_kernel,
        grid=(t // tm, N_BIG // tn),
        in_specs=[
            pl.BlockSpec((tm, D_MODEL), lambda i, j: (i, 0)),
            pl.BlockSpec((1, D_MODEL), lambda i, j: (0, 0)),
            pl.BlockSpec((None, D_MODEL, tn), lambda i, j: (layer, 0, j)),
            pl.BlockSpec((None, D_MODEL, LANES), lambda i, j: (layer, 0, 0)),
        ],
        out_specs=[
            pl.BlockSpec((tm, tn), lambda i, j: (i, j)),
            pl.BlockSpec((tm, LANES), lambda i, j: (i, 0)),
        ],
        out_shape=[jax.ShapeDtypeStruct((t, N_BIG), BF16), jax.ShapeDtypeStruct((t, LANES), F32)],
        scratch_shapes=[pltpu.VMEM((tm, D_MODEL), BF16)],
        compiler_params=_cparams("parallel", "arbitrary"),
        name="inproj",
    )(x, g, w_big, w_small)


def _dn_kernel(qkv_ref, gate_ref, sm_ref, cw_ref, prow_ref, nw_ref, o_ref, halo_scr, state_scr):
    @pl.when(pl.program_id(1) == 0)
    def _():
        halo_scr[...] = jnp.zeros_like(halo_scr)
        state_scr[...] = jnp.zeros_like(state_scr)

    row, col, m_incl, m_strict = _chunk_masks()
    tril = jnp.where(m_incl, 1.0, 0.0)
    eye = jnp.where(row == col, 1.0, 0.0)
    first = row[:, 0:1] < CHUNK

    units = [(j, h) for j in range(DN_TILES) for h in range(DN_HEADS)]
    heads = range(DN_HEADS)
    q, k, v, kb, vb, decay, gc_col, kk, qk = {}, {}, {}, {}, {}, {}, {}, {}, {}
    low, xs, tinv, cd = {}, {}, {}, {}
    u, w, attn, qg, kdt, gl = {}, {}, {}, {}, {}, {}
    pad = jnp.zeros((CHUNK, DN_HEAD_DIM), F32)

    def same_block(size):
        shift = size.bit_length() - 1
        return (row >> shift) == (col >> shift)

    def prepare(j):
        rows = slice(j * TBLK, (j + 1) * TBLK)
        sm = sm_ref[rows, :]
        g_all = -jnp.exp(prow_ref[0:1, :]) * _softplus(sm + prow_ref[1:2, :])
        beta_all = _sigmoid(sm)
        gc_all = _mm_sel_lhs(tril, g_all)
        gct_all = gc_all.T
        for h in heads:
            def conv_silu(seg):
                lanes = slice(seg * MIX_W + h * DN_HEAD_DIM, seg * MIX_W + (h + 1) * DN_HEAD_DIM)
                if j == 0:
                    halo = halo_scr[:, lanes]
                else:
                    halo = qkv_ref[j * TBLK - 16:j * TBLK, lanes].astype(F32)[16 - HALO:16]
                return _silu(_causal_conv(halo, qkv_ref[rows, lanes].astype(F32), cw_ref[:, lanes], None))

            un = (j, h)
            qh = conv_silu(0)
            kh = conv_silu(1)
            v[un] = conv_silu(2)
            q[un] = qh * lax.rsqrt(jnp.sum(qh * qh, axis=-1, keepdims=True) + EPS) * (DN_HEAD_DIM ** -0.5)
            k[un] = kh * lax.rsqrt(jnp.sum(kh * kh, axis=-1, keepdims=True) + EPS)
            gc_col[un] = gc_all[:, LANE_DN_A + h:LANE_DN_A + h + 1]
            beta = beta_all[:, LANE_DN_B + h:LANE_DN_B + h + 1]
            gc_row = gct_all[LANE_DN_A + h:LANE_DN_A + h + 1, :]
            decay[un] = jnp.exp(jnp.where(m_incl, gc_col[un] - gc_row, -jnp.inf))
            kb[un] = k[un] * beta
            vb[un] = v[un] * beta
    def gram(units):
        for un in units:
            kk[un] = _mm_nt(kb[un], k[un])
            qk[un] = _mm_nt(q[un], k[un])

    def solve(units):
        for un in units:
            low[un] = jnp.where(m_strict, kk[un] * decay[un], 0.0)
            x = -jnp.where(same_block(DN_BASE_BLOCK), low[un], 0.0)
            xs[un] = _split2(x)
            tinv[un] = eye + x
        for _ in range(2):
            for un in units:
                xs[un] = _split2(_mm_x3s(xs[un], xs[un]))
            for un in units:
                tinv[un] = tinv[un] + _mm_x3s(_split2(tinv[un]), xs[un])
        size = DN_BASE_BLOCK
        while size < CHUNK:
            off_diag = same_block(2 * size) & jnp.logical_not(same_block(size))
            for un in units:
                cd[un] = _mm(jnp.where(off_diag, low[un], 0.0), tinv[un])
            for un in units:
                tinv[un] = tinv[un] - _mm(tinv[un], cd[un])
            size *= 2
        for un in units:
            egc = jnp.exp(gc_col[un])
            u[un] = _mm(tinv[un], vb[un])
            w[un] = _mm(tinv[un], kb[un] * egc)
            attn[un] = _bf(jnp.where(m_incl, qk[un] * decay[un], 0.0))
            qg[un] = _bf(q[un] * egc)
            gl[un] = (gc_col[un][CHUNK - 1:CHUNK, :], gc_col[un][TBLK - 1:TBLK, :])
            kdt[un] = _bf((k[un] * jnp.exp(jnp.where(first, gl[un][0], gl[un][1]) - gc_col[un])).T)

    s = {h: state_scr[h] for h in heads}

    def scan(j):
        outs = {h: [] for h in heads}
        for c in range(2):
            r0, r1 = c * CHUNK, (c + 1) * CHUNK
            v_blk = {}
            for h in heads:
                un = (j, h)
                v_new = u[un][r0:r1] - _mm(w[un][r0:r1], s[h])
                v_blk[h] = _bf(jnp.concatenate([v_new, pad] if c == 0 else [pad, v_new], axis=0))
            for h in heads:
                un = (j, h)
                outs[h].append(_mm(qg[un][r0:r1], s[h]) + _mm(attn[un][r0:r1, :], v_blk[h]))
                s[h] = s[h] * jnp.exp(gl[un][c]) + _mm(kdt[un], v_blk[h])
        for h in heads:
            lanes = slice(h * DN_HEAD_DIM, (h + 1) * DN_HEAD_DIM)
            rows = slice(j * TBLK, (j + 1) * TBLK)
            o = jnp.concatenate(outs[h], axis=0)
            o_ref[rows, lanes] = _bf(_rms(o, nw_ref[...]) * _silu(gate_ref[rows, lanes].astype(F32)))

    for j in range(DN_TILES):
        prepare(j)
    gram(units)
    solve(units)
    for j in range(DN_TILES):
        scan(j)
    for h in heads:
        state_scr[h] = s[h]
    last = DN_TILES * TBLK
    halo_scr[...] = qkv_ref[last - 16:last, :].astype(F32)[16 - HALO:16]


def _dn_mixer(big, small, conv_w, prow, norm_w, batch, seq):
    rows = DN_TILES * TBLK
    nd = seq // rows
    row_map = lambda b, d: (b * nd + d, 0)
    return pl.pallas_call(
        _dn_kernel,
        grid=(batch, nd),
        in_specs=[
            pl.BlockSpec((rows, 3 * MIX_W), lambda b, d: (b * nd + d, COL_DN_QKV // (3 * MIX_W))),
            pl.BlockSpec((rows, MIX_W), lambda b, d: (b * nd + d, COL_DN_GATE // MIX_W)),
            pl.BlockSpec((rows, LANES), row_map),
            pl.BlockSpec((CONV_WIDTH, 3 * MIX_W), lambda b, d: (0, 0)),
            pl.BlockSpec((8, LANES), lambda b, d: (0, 0)),
            pl.BlockSpec((1, DN_HEAD_DIM), lambda b, d: (0, 0)),
        ],
        out_specs=pl.BlockSpec((rows, MIX_W), row_map),
        out_shape=jax.ShapeDtypeStruct((batch * seq, MIX_W), BF16),
        scratch_shapes=[pltpu.VMEM((HALO, 3 * MIX_W), F32), pltpu.VMEM((DN_HEADS, DN_HEAD_DIM, DN_HEAD_DIM), F32)],
        compiler_params=_cparams("parallel", "arbitrary"),
        name="deltanet",
    )(big, big, small, conv_w, prow, norm_w)


SB_TILE = 128
SB_TQ = 256
SB_SUB = SB_TQ // SB_TILE
SB_STRIP = 256
SB_HEADS_PER_STEP = 4
LOG2E = 1.4426950408889634
SB_LAG = 4


def _sb_kernel(q_ref, k_ref, v_ref, w_ref, o_ref, r_scr, acc_scr, za_scr, zb_scr, wgt_scr):
    qi = pl.program_id(2)
    row = lax.broadcasted_iota(jnp.int32, (SB_STRIP, SB_TILE), 0)
    col = lax.broadcasted_iota(jnp.int32, (SB_STRIP, SB_TILE), 1)
    wmat = w_ref[...]
    r_scr[...] = jnp.zeros_like(r_scr)
    acc_scr[...] = jnp.zeros_like(acc_scr)
    strips = range(SB_TQ // SB_STRIP)
    heads = range(SB_HEADS_PER_STEP)
    pairs = range(SB_HEADS_PER_STEP // 2)
    order = list(reversed(range(SB_SUB)))
    qh = {}
    for s in strips:
        for h in heads:
            lanes = slice((h // 2) * LANES, (h // 2 + 1) * LANES)
            q = q_ref[s * SB_STRIP:(s + 1) * SB_STRIP, lanes].astype(F32) * (SB_HEAD_DIM ** -0.5)
            qh[s, h] = _bf(jnp.where((col >= SB_HEAD_DIM) == (h % 2 == 1), q, 0.0))

    def chain_id(t, s, h):
        return (t * len(strips) + s) * SB_HEADS_PER_STEP + h

    def masked_out(t, s, diagonal):
        return diagonal and t * SB_TILE >= (s + 1) * SB_STRIP

    def block_rows(kb_lo):
        return pl.ds(pl.multiple_of(kb_lo * SB_TILE, SB_TILE), SB_SUB * SB_TILE)

    def score_ops(kb_lo, z_dst, diagonal):
        def one(t, s, p):
            def emit():
                rows = pl.ds(pl.multiple_of((kb_lo + t) * SB_TILE, SB_TILE), SB_TILE)
                kblk = k_ref[rows, p * LANES:(p + 1) * LANES]
                zz = _mm_nt(jnp.concatenate([qh[s, 2 * p], qh[s, 2 * p + 1]], axis=0), kblk) * LOG2E
                z_dst[chain_id(t, s, 2 * p)] = zz[:SB_STRIP]
                z_dst[chain_id(t, s, 2 * p + 1)] = zz[SB_STRIP:]
            return emit
        return [one(t, s, p) for t in order for s in strips for p in pairs if not masked_out(t, s, diagonal)]

    def value_ops(kb_lo):
        def one(p):
            def emit():
                vblk = v_ref[block_rows(kb_lo), p * LANES:(p + 1) * LANES]
                both = jnp.dot(jnp.concatenate([wgt_scr[2 * p], wgt_scr[2 * p + 1]], axis=0), vblk,
                               preferred_element_type=F32)
                acc_scr[2 * p] += both[:SB_TQ]
                acc_scr[2 * p + 1] += both[SB_TQ:]
            return emit
        return [one(p) for p in pairs]

    def weight_stage(z_src, diagonal, extras):
        chains = [(t, s, h) for t in order for s in strips for h in heads]
        cs, tri = {}, {}

        def cumsum(group):
            sps = []
            for c in group:
                t, s, _ = c
                z = z_src[chain_id(*c)]
                neg_abs = pltpu.bitcast(pltpu.bitcast(z, jnp.uint32) | jnp.uint32(0x80000000), F32)
                sp = jnp.maximum(z, 0.0) + jnp.log2(1.0 + jnp.exp2(neg_abs))
                if diagonal:
                    tri[c] = col + t * SB_TILE < row + s * SB_STRIP
                    sp = jnp.where(tri[c], sp, 0.0)
                sps.append(_bf(sp))
            both = jnp.dot(jnp.concatenate(sps, axis=0), wmat, preferred_element_type=F32)
            for i, c in enumerate(group):
                cs[c] = both[i * SB_STRIP:(i + 1) * SB_STRIP]

        def weigh(c):
            t, s, h = c
            rows = slice(s * SB_STRIP, (s + 1) * SB_STRIP)
            lanes = slice(t * SB_TILE, (t + 1) * SB_TILE)
            wgt = jnp.exp2(z_src[chain_id(*c)] - cs[c][:, :SB_TILE] - r_scr[h, rows])
            wgt_scr[h, rows, lanes] = _bf(jnp.where(tri[c], wgt, 0.0) if diagonal else wgt)
            r_scr[h, rows] += cs[c][:, SB_TILE:]

        live = [c for c in chains if not masked_out(c[0], c[1], diagonal)]
        for t, s, h in chains:
            if masked_out(t, s, diagonal):
                wgt_scr[h, s * SB_STRIP:(s + 1) * SB_STRIP, t * SB_TILE:(t + 1) * SB_TILE] = (
                    jnp.zeros((SB_STRIP, SB_TILE), BF16))
        groups = [live[i:i + 2] for i in range(0, len(live), 2)]
        lag = SB_LAG // 2
        per_slot = -(-len(extras) // len(groups))
        for i, group in enumerate(groups):
            cumsum(group)
            for emit in extras[i * per_slot:(i + 1) * per_slot]:
                emit()
            if i >= lag:
                for c in groups[i - lag]:
                    weigh(c)
        for group in groups[-lag:]:
            for c in group:
                weigh(c)

    def step(cur, z_src, z_dst, issue_next=True):
        extras = value_ops((cur + 1) * SB_SUB)
        if issue_next:
            extras = extras + score_ops(jnp.maximum(cur - 1, 0) * SB_SUB, z_dst, False)
        weight_stage(z_src, False, extras)

    for emit in score_ops(qi * SB_SUB, za_scr, True):
        emit()
    weight_stage(za_scr, True, score_ops(jnp.maximum(qi - 1, 0) * SB_SUB, zb_scr, False))

    def body(j, carry):
        step(qi - 1 - 2 * j, zb_scr, za_scr)
        step(qi - 2 - 2 * j, za_scr, zb_scr)
        return carry

    lax.fori_loop(0, qi // 2, body, 0)

    @pl.when(qi % 2 == 1)
    def _():
        step(0, zb_scr, za_scr, issue_next=False)

    for emit in value_ops(0):
        emit()
    lane = lax.broadcasted_iota(jnp.int32, (SB_TQ, SB_TILE), 1)
    for p in range(SB_HEADS_PER_STEP // 2):
        o_ref[:, p * LANES:(p + 1) * LANES] = _bf(jnp.where(lane < SB_HEAD_DIM, acc_scr[2 * p], acc_scr[2 * p + 1]))


def _sb_mixer(big, wmat, batch, seq):
    nq = seq // SB_TQ
    width = SB_HEADS_PER_STEP * SB_HEAD_DIM
    groups = MIX_W // width
    qoff = COL_SB_QKV // width
    return pl.pallas_call(
        _sb_kernel,
        grid=(batch, groups, nq),
        in_specs=[
            pl.BlockSpec((SB_TQ, width), lambda b, p, i: (b * nq + i, qoff + p)),
            pl.BlockSpec((seq, width), lambda b, p, i: (b, qoff + groups + p)),
            pl.BlockSpec((seq, width), lambda b, p, i: (b, qoff + 2 * groups + p)),
            pl.BlockSpec((SB_TILE, 2 * SB_TILE), lambda b, p, i: (0, 0)),
        ],
        out_specs=pl.BlockSpec((SB_TQ, width), lambda b, p, i: (b * nq + i, p)),
        out_shape=jax.ShapeDtypeStruct((batch * seq, MIX_W), BF16),
        scratch_shapes=[pltpu.VMEM((SB_HEADS_PER_STEP, SB_TQ, SB_TILE), F32),
                        pltpu.VMEM((SB_HEADS_PER_STEP, SB_TQ, SB_TILE), F32),
                        pltpu.VMEM((SB_SUB * (SB_TQ // SB_STRIP) * SB_HEADS_PER_STEP, SB_STRIP, SB_TILE), F32),
                        pltpu.VMEM((SB_SUB * (SB_TQ // SB_STRIP) * SB_HEADS_PER_STEP, SB_STRIP, SB_TILE), F32),
                        pltpu.VMEM((SB_HEADS_PER_STEP, SB_TQ, SB_SUB * SB_TILE), BF16)],
        compiler_params=_cparams("parallel", "parallel", "arbitrary"),
        name="stickbreak",
    )(big, big, big, wmat)


SSD_TILES = 2
GROUP_W = MIX_W // SSM_GROUPS
HEADS_PER_GROUP = SSM_HEADS // SSM_GROUPS


def _ssd_kernel(z_ref, xbc_ref, sm_ref, cw_ref, cb_ref, prow_ref, drow_ref, nw_ref, e_ref, o_ref,
                halo_scr, state_scr):
    @pl.when(pl.program_id(1) == 0)
    def _():
        halo_scr[...] = jnp.zeros_like(halo_scr)
        state_scr[...] = jnp.zeros_like(state_scr)

    row, col, m_incl, _ = _chunk_masks()
    tril = jnp.where(m_incl, 1.0, 0.0)
    first = row[:, 0:1] < CHUNK
    gcol = lax.broadcasted_iota(jnp.int32, (TBLK, GROUP_W), 1)

    gn = SSM_GROUPS * SSM_STATE
    e = e_ref[...]
    for j in range(SSD_TILES):
        rows = slice(j * TBLK, (j + 1) * TBLK)
        cur = xbc_ref[rows, :].astype(F32)
        halo = halo_scr[...] if j == 0 else xbc_ref[j * TBLK - 16:j * TBLK, :].astype(F32)[16 - HALO:16]
        xbc = _silu(_causal_conv(halo, cur, cw_ref[...], cb_ref[...]))
        xs = xbc[:, :MIX_W]
        bm = xbc[:, MIX_W:MIX_W + gn]
        cm = xbc[:, MIX_W + gn:]

        dt_all = _softplus(sm_ref[rows, :] + prow_ref[1:2, :])
        a_all = -jnp.exp(prow_ref[0:1, :]) * dt_all
        acum_all = _mm_sel_lhs(tril, a_all)
        acum_t = acum_all.T
        dt_exp = _mm_sel_rhs(dt_all, e)
        acum_exp = _mm_sel_rhs(acum_all, e)
        xdt = xs * dt_exp

        groups = range(SSM_GROUPS)
        c_g = {g: cm[:, g * SSM_STATE:(g + 1) * SSM_STATE] for g in groups}
        b_g = {g: bm[:, g * SSM_STATE:(g + 1) * SSM_STATE] for g in groups}
        x_g = {g: xdt[:, g * GROUP_W:(g + 1) * GROUP_W] for g in groups}
        a_g = {g: acum_exp[:, g * GROUP_W:(g + 1) * GROUP_W] for g in groups}
        scores = {g: _mm_nt(c_g[g], b_g[g]) for g in groups}
        y_g = {g: jnp.zeros((TBLK, GROUP_W), F32) for g in groups}
        for hh in range(HEADS_PER_GROUP):
            for g in groups:
                h = g * HEADS_PER_GROUP + hh
                a_col = acum_exp[:, h * SSM_HEAD_DIM:h * SSM_HEAD_DIM + 1]
                a_row = acum_t[LANE_SSM_DT + h:LANE_SSM_DT + h + 1, :]
                lmat = jnp.exp(jnp.where(m_incl, a_col - a_row, -jnp.inf))
                y_g[g] = y_g[g] + _mm(scores[g] * lmat, jnp.where(gcol // SSM_HEAD_DIM == hh, x_g[g], 0.0))

        b_t = {g: b_g[g].T for g in groups}
        a_last = {g: (a_g[g][CHUNK - 1:CHUNK, :], a_g[g][TBLK - 1:TBLK, :]) for g in groups}
        x_sc = {g: x_g[g] * jnp.exp(jnp.where(first, a_last[g][0], a_last[g][1]) - a_g[g]) for g in groups}
        st = {g: state_scr[g] for g in groups}
        yoff = {g: [] for g in groups}
        for c in range(2):
            r0, r1 = c * CHUNK, (c + 1) * CHUNK
            for g in groups:
                yoff[g].append(_mm(c_g[g][r0:r1], st[g]) * jnp.exp(a_g[g][r0:r1]))
            for g in groups:
                st[g] = st[g] * jnp.exp(a_last[g][c]) + _mm(b_t[g], jnp.where(first == (c == 0), x_sc[g], 0.0))
        for g in groups:
            lo, hi = g * GROUP_W, (g + 1) * GROUP_W
            state_scr[g] = st[g]
            y = y_g[g] + jnp.concatenate(yoff[g], axis=0) + xs[:, lo:hi] * drow_ref[:, lo:hi]
            y = y * _silu(z_ref[rows, lo:hi].astype(F32))
            o_ref[rows, lo:hi] = _bf(_rms(y, nw_ref[:, lo:hi]))
    last = SSD_TILES * TBLK
    halo_scr[...] = xbc_ref[last - 16:last, :].astype(F32)[16 - HALO:16]


def _ssd_mixer(big, small, conv_w, conv_b, prow, drow, norm_w, emat, batch, seq):
    rows = SSD_TILES * TBLK
    nd = seq // rows
    conv_dim = MIX_W + 2 * SSM_GROUPS * SSM_STATE
    row_map = lambda b, d: (b * nd + d, 0)
    const = lambda b, d: (0, 0)
    return pl.pallas_call(
        _ssd_kernel,
        grid=(batch, nd),
        in_specs=[
            pl.BlockSpec((rows, MIX_W), lambda b, d: (b * nd + d, COL_SSM_Z // MIX_W)),
            pl.BlockSpec((rows, conv_dim), lambda b, d: (b * nd + d, COL_SSM_XBC // conv_dim)),
            pl.BlockSpec((rows, LANES), row_map),
            pl.BlockSpec((CONV_WIDTH, conv_dim), const),
            pl.BlockSpec((1, conv_dim), const),
            pl.BlockSpec((8, LANES), const),
            pl.BlockSpec((1, MIX_W), const),
            pl.BlockSpec((1, MIX_W), const),
            pl.BlockSpec((LANES, MIX_W), const),
        ],
        out_specs=pl.BlockSpec((rows, MIX_W), row_map),
        out_shape=jax.ShapeDtypeStruct((batch * seq, MIX_W), BF16),
        scratch_shapes=[pltpu.VMEM((HALO, conv_dim), F32), pltpu.VMEM((SSM_GROUPS, SSM_STATE, GROUP_W), F32)],
        compiler_params=_cparams("parallel", "arbitrary"),
        name="ssd",
    )(big, big, small, conv_w, conv_b, prow, drow, norm_w, emat)


def _merge_kernel(x_ref, odn_ref, osb_ref, ossm_ref, g0_ref, g1_ref, g2_ref, wb_ref, wo_ref, o_ref):
    m = _sigmoid(g0_ref[...].astype(F32)) * jnp.dot(odn_ref[...], wb_ref[0], preferred_element_type=F32)
    m = m + _sigmoid(g1_ref[...].astype(F32)) * jnp.dot(osb_ref[...], wb_ref[1], preferred_element_type=F32)
    m = m + _sigmoid(g2_ref[...].astype(F32)) * jnp.dot(ossm_ref[...], wb_ref[2], preferred_element_type=F32)
    o_ref[...] = x_ref[...] + jnp.dot(_bf(m), wo_ref[...], preferred_element_type=F32)


def _merge(x, o_dn, o_sb, o_ssm, big, w_branch, w_out, layer, tm=512):
    t = x.shape[0]
    tm = min(tm, t)
    rows = lambda i: (i, 0)
    gcol = COL_GATES // D_MODEL
    return pl.pallas_call(
        _merge_kernel,
        grid=(t // tm,),
        in_specs=[
            pl.BlockSpec((tm, D_MODEL), rows),
            pl.BlockSpec((tm, MIX_W), rows),
            pl.BlockSpec((tm, MIX_W), rows),
            pl.BlockSpec((tm, MIX_W), rows),
            pl.BlockSpec((tm, D_MODEL), lambda i: (i, gcol)),
            pl.BlockSpec((tm, D_MODEL), lambda i: (i, gcol + 1)),
            pl.BlockSpec((tm, D_MODEL), lambda i: (i, gcol + 2)),
            pl.BlockSpec((None, 3, MIX_W, D_MODEL), lambda i: (layer, 0, 0, 0)),
            pl.BlockSpec((None, D_MODEL, D_MODEL), lambda i: (layer, 0, 0)),
        ],
        out_specs=pl.BlockSpec((tm, D_MODEL), rows),
        out_shape=jax.ShapeDtypeStruct((t, D_MODEL), F32),
        compiler_params=_cparams("parallel"),
        name="merge",
    )(x, o_dn, o_sb, o_ssm, big, big, big, w_branch, w_out)


def _mlp_kernel(x_ref, g_ref, wu_ref, wd_ref, gf_ref, o_ref, xn_scr, acc_scr, *, final_norm):
    j = pl.program_id(1)

    @pl.when(j == 0)
    def _():
        xn_scr[...] = _bf(_rms(x_ref[...], g_ref[...]))
        acc_scr[...] = jnp.zeros_like(acc_scr)

    h = jnp.dot(xn_scr[...], wu_ref[...], preferred_element_type=F32)
    h = jnp.square(jnp.maximum(h, 0.0))
    acc_scr[...] += jnp.dot(_bf(h), wd_ref[...], preferred_element_type=F32)

    @pl.when(j == pl.num_programs(1) - 1)
    def _():
        y = x_ref[...] + acc_scr[...]
        o_ref[...] = _rms(y, gf_ref[...]) if final_norm else y


def _mlp(x, g, w_up, w_down, g_final, layer, final_norm, tm=1024, tf=2048):
    t = x.shape[0]
    tm = min(tm, t)
    return pl.pallas_call(
        functools.partial(_mlp_kernel, final_norm=final_norm),
        grid=(t // tm, D_FF // tf),
        in_specs=[
            pl.BlockSpec((tm, D_MODEL), lambda i, j: (i, 0)),
            pl.BlockSpec((1, D_MODEL), lambda i, j: (0, 0)),
            pl.BlockSpec((None, D_MODEL, tf), lambda i, j: (layer, 0, j)),
            pl.BlockSpec((None, tf, D_MODEL), lambda i, j: (layer, j, 0)),
            pl.BlockSpec((1, D_MODEL), lambda i, j: (0, 0)),
        ],
        out_specs=pl.BlockSpec((tm, D_MODEL), lambda i, j: (i, 0)),
        out_shape=jax.ShapeDtypeStruct((t, D_MODEL), F32),
        scratch_shapes=[pltpu.VMEM((tm, D_MODEL), BF16), pltpu.VMEM((tm, D_MODEL), F32)],
        compiler_params=_cparams("parallel", "arbitrary"),
        name="mlp",
    )(x, g, w_up, w_down, g_final)


def _lane_row(values, lane0):
    return jnp.zeros((LANES,), F32).at[lane0:lane0 + values.shape[0]].set(values.astype(F32))


def _param_rows(a_log, dt_bias, lane0):
    rows = jnp.zeros((8, LANES), F32)
    return rows.at[0].set(_lane_row(a_log, lane0)).at[1].set(_lane_row(dt_bias, lane0))


def _sb_cumsum_matrix():
    j = jnp.arange(SB_TILE)[:, None]
    s = jnp.arange(SB_TILE)[None, :]
    return jnp.concatenate([(j >= s).astype(BF16), jnp.ones((SB_TILE, SB_TILE), BF16)], axis=1)


def _ssm_expand_matrix():
    lane = jnp.arange(LANES)[:, None]
    ch = jnp.arange(MIX_W)[None, :]
    return (lane == LANE_SSM_DT + ch // SSM_HEAD_DIM).astype(BF16)


def _split_w_in(w):
    w = _bf(w)
    big = jnp.concatenate([w[..., 0:4096], w[..., 4112:10256], w[..., 10272:13344]], axis=-1)
    pad = jnp.zeros(w.shape[:-1] + (LANES - 32,), w.dtype)
    small = jnp.concatenate([w[..., 4096:4112], w[..., 10256:10272], pad], axis=-1)
    return big, small


def kernel(x, norm_mix, w_in, dn_conv_w, dn_a_log, dn_dt_bias, dn_norm_w, ssm_conv_w, ssm_conv_b, ssm_a_log,
           ssm_dt_bias, ssm_d, ssm_norm_w, w_branch, w_out, norm_mlp, w_up, w_down, norm_final):
    batch, seq, _ = x.shape
    depth = w_in.shape[0]
    h = x.reshape(batch * seq, D_MODEL)
    sb_w = _sb_cumsum_matrix()
    emat = _ssm_expand_matrix()
    g_final = norm_final.reshape(1, D_MODEL)
    w_big, w_small = _split_w_in(w_in)
    w_branch, w_out, w_up, w_down = _bf(w_branch), _bf(w_out), _bf(w_up), _bf(w_down)
    for l in range(depth):
        big, small = _inproj(h, norm_mix[l].reshape(1, D_MODEL), w_big, w_small, l)
        o_dn = _dn_mixer(big, small, dn_conv_w[l], _param_rows(dn_a_log[l], dn_dt_bias[l], LANE_DN_A),
                         dn_norm_w[l].reshape(1, DN_HEAD_DIM), batch, seq)
        o_sb = _sb_mixer(big, sb_w, batch, seq)
        o_ssm = _ssd_mixer(big, small, ssm_conv_w[l], ssm_conv_b[l].reshape(1, -1),
                           _param_rows(ssm_a_log[l], ssm_dt_bias[l], LANE_SSM_DT),
                           jnp.repeat(ssm_d[l], SSM_HEAD_DIM).reshape(1, MIX_W),
                           ssm_norm_w[l].reshape(1, MIX_W), emat, batch, seq)
        h = _merge(h, o_dn, o_sb, o_ssm, big, w_branch, w_out, l)
        h = _mlp(h, norm_mlp[l].reshape(1, D_MODEL), w_up, w_down, g_final, l, final_norm=(l == depth - 1))
    return h.reshape(batch, seq, D_MODEL)
```
